```python
import math
import jax, jax.numpy as jnp
from jax import lax
import numpy as np

D_MODEL = 1024
BATCH = 8
SEQ = 8192
DEPTH = 2

CHUNK = 64
LEFT_CHUNKS = 8
BAND = (LEFT_CHUNKS + 1) * CHUNK
HEAD_DIM = 64
A_HEADS = 8
B_HEADS = 4
C_HEADS = 8
A_WIDTH = A_HEADS * HEAD_DIM
B_WIDTH = B_HEADS * 2 * HEAD_DIM
C_WIDTH = C_HEADS * HEAD_DIM
IN_WIDTH = 3 * A_WIDTH + 3 * B_WIDTH + 3 * C_WIDTH + C_HEADS
N_BRANCH = 3
D_FF = 4 * D_MODEL
REL_CLIP = 128
ROPE_THETA = 500000.0
ROPE_DIM = HEAD_DIM // 4
Q_BLOCK = 128
EPS = 1e-6

kernel_name = "streaming_hybrid_gated_attention_trunk"


def rmsnorm(x, g):
    xf = x.astype(jnp.float32)
    y = xf * lax.rsqrt(jnp.mean(xf * xf, axis=-1, keepdims=True) + EPS)
    return (y * g.astype(jnp.float32)).astype(x.dtype)


def partial_rope(x, positions):
    half = ROPE_DIM // 2
    inv_freq = jnp.power(ROPE_THETA, -jnp.arange(0, ROPE_DIM, 2, dtype=jnp.float32) / ROPE_DIM)
    ang = positions.astype(jnp.float32)[..., None] * inv_freq
    ang = ang.reshape(ang.shape[:2] + (1,) * (x.ndim - 3) + (half,))
    cos, sin = jnp.cos(ang), jnp.sin(ang)
    xf = x.astype(jnp.float32)
    x1, x2, rest = xf[..., :half], xf[..., half:ROPE_DIM], xf[..., ROPE_DIM:]
    out = jnp.concatenate([x1 * cos - x2 * sin, x2 * cos + x1 * sin, rest], axis=-1)
    return out.astype(x.dtype)


def chunked_relbias_attention(q, k, v, rel_bias):
    B, S, H, dh = q.shape
    nc = S // CHUNK
    pad = LEFT_CHUNKS * CHUNK
    k_pad = jnp.pad(k, ((0, 0), (pad, 0), (0, 0), (0, 0)))
    v_pad = jnp.pad(v, ((0, 0), (pad, 0), (0, 0), (0, 0)))
    qc = q.reshape(B, nc, CHUNK, H, dh).transpose(1, 0, 2, 3, 4)
    q_off = jnp.arange(CHUNK)
    k_off = jnp.arange(BAND) - pad
    rel = jnp.clip(k_off[None, :] - q_off[:, None], -REL_CLIP, REL_CLIP) + REL_CLIP
    bias = rel_bias[:, rel].astype(jnp.float32)
    scale = dh ** -0.5

    def one_chunk(args):
        c, qb = args
        kb = lax.dynamic_slice_in_dim(k_pad, c * CHUNK, BAND, axis=1)
        vb = lax.dynamic_slice_in_dim(v_pad, c * CHUNK, BAND, axis=1)
        s = jnp.einsum('bqhd,bkhd->bhqk', qb, kb, preferred_element_type=jnp.float32) * scale + bias
        valid = (c * CHUNK + k_off) >= 0
        s = jnp.where(valid[None, None, None, :], s, -jnp.inf)
        p = jax.nn.softmax(s, axis=-1)
        return jnp.einsum('bhqk,bkhd->bqhd', p.astype(vb.dtype), vb)

    out = lax.map(one_chunk, (jnp.arange(nc), qc))
    return out.transpose(1, 0, 2, 3, 4).reshape(B, S, H * dh)


def differential_attention(q, k, v, lambdas, sub_gain, lam_init):
    B, S, H, _, dh = q.shape
    nb = S // Q_BLOCK
    scale = dh ** -0.5
    lf = lambdas.astype(jnp.float32)
    lam = jnp.exp(jnp.sum(lf[0] * lf[1])) - jnp.exp(jnp.sum(lf[2] * lf[3])) + lam_init
    qb = q.reshape(B, nb, Q_BLOCK, H, 2, dh).transpose(1, 0, 2, 3, 4, 5)
    key_chunk = jnp.arange(S) // CHUNK

    def one_block(args):
        i, qi = args
        q_chunk = (i * Q_BLOCK + jnp.arange(Q_BLOCK)) // CHUNK
        mask = key_chunk[None, :] <= q_chunk[:, None]
        s = jnp.einsum('bqhcd,bkhcd->bhcqk', qi, k, preferred_element_type=jnp.float32) * scale
        s = jnp.where(mask, s, -jnp.inf)
        p = jax.nn.softmax(s, axis=-1)
        pd = p[:, :, 0] - lam * p[:, :, 1]
        return jnp.einsum('bhqk,bkhe->bqhe', pd.astype(v.dtype), v)

    o = lax.map(one_block, (jnp.arange(nb), qb))
    o = o.transpose(1, 0, 2, 3, 4).reshape(B, S, H, 2 * dh)
    o = rmsnorm(o, sub_gain) * (1.0 - lam_init)
    return o.reshape(B, S, H * 2 * dh)


def forgetting_attention(q, k, v, f_logit):
    B, S, H, dh = q.shape
    nb = S // Q_BLOCK
    scale = dh ** -0.5
    F = jnp.cumsum(jax.nn.log_sigmoid(f_logit.astype(jnp.float32)), axis=1)
    F_k = F.transpose(0, 2, 1)
    qb = q.reshape(B, nb, Q_BLOCK, H, dh).transpose(1, 0, 2, 3, 4)
    Fq = F.reshape(B, nb, Q_BLOCK, H).transpose(1, 0, 3, 2)
    k_pos = jnp.arange(S)

    def one_block(args):
        i, qi, fq = args
        q_pos = i * Q_BLOCK + jnp.arange(Q_BLOCK)
        mask = k_pos[None, :] <= q_pos[:, None]
        s = jnp.einsum('bqhd,bkhd->bhqk', qi, k, preferred_element_type=jnp.float32) * scale
        s = s + (fq[..., None] - F_k[:, :, None, :])
        s = jnp.where(mask, s, -jnp.inf)
        p = jax.nn.softmax(s, axis=-1)
        return jnp.einsum('bhqk,bkhd->bqhd', p.astype(v.dtype), v)

    o = lax.map(one_block, (jnp.arange(nb), qb, Fq))
    return o.transpose(1, 0, 2, 3, 4).reshape(B, S, H * dh)


def split_columns(proj):
    sizes = (A_WIDTH,) * 3 + (B_WIDTH,) * 3 + (C_WIDTH,) * 3 + (C_HEADS,)
    outs = []
    start = 0
    for n in sizes:
        outs.append(proj[..., start:start + n])
        start += n
    return outs


def hybrid_layer(x, positions, norm_mix, w_in, rel_bias, lambdas, sub_gain, b_forget,
                 w_br_a, w_br_b, w_br_c, w_gate, b_gate, w_out, norm_mlp, w_ff1, w_ff2, layer_idx):
    B, S, D = x.shape
    h = rmsnorm(x, norm_mix)
    proj = h @ w_in
    qa, ka, va, qb, kb, vb, qc, kc, vc, fl = split_columns(proj)
    ya = chunked_relbias_attention(qa.reshape(B, S, A_HEADS, HEAD_DIM), ka.reshape(B, S, A_HEADS, HEAD_DIM),
                                   va.reshape(B, S, A_HEADS, HEAD_DIM), rel_bias)
    lam_init = 0.8 - 0.6 * math.exp(-0.3 * layer_idx)
    qb = partial_rope(qb.reshape(B, S, B_HEADS, 2, HEAD_DIM), positions)
    kb = partial_rope(kb.reshape(B, S, B_HEADS, 2, HEAD_DIM), positions)
    yb = differential_attention(qb, kb, vb.reshape(B, S, B_HEADS, 2 * HEAD_DIM), lambdas, sub_gain, lam_init)
    yc = forgetting_attention(qc.reshape(B, S, C_HEADS, HEAD_DIM), kc.reshape(B, S, C_HEADS, HEAD_DIM),
                              vc.reshape(B, S, C_HEADS, HEAD_DIM), fl + b_forget)
    g = jax.nn.sigmoid((h @ w_gate + b_gate).astype(jnp.float32)).astype(x.dtype).reshape(B, S, N_BRANCH, D)
    merged = g[:, :, 0] * (ya @ w_br_a) + g[:, :, 1] * (yb @ w_br_b) + g[:, :, 2] * (yc @ w_br_c)
    x = x + merged @ w_out
    h2 = rmsnorm(x, norm_mlp)
    x = x + jnp.square(jax.nn.relu(h2 @ w_ff1)) @ w_ff2
    return x


def setup_inputs(seed: int = 0) -> dict:
    key = jax.random.key(seed)
    ks = jax.random.split(key, 20)
    f32 = jnp.float32
    nrm = lambda k, shape, s: jax.random.normal(k, shape, f32) * s
    x = nrm(ks[0], (BATCH, SEQ, D_MODEL), 1.0)
    offset = jax.random.randint(ks[1], (BATCH, 1), 0, 64, dtype=jnp.int32) * CHUNK
    positions = (offset + jnp.arange(SEQ, dtype=jnp.int32)[None, :]).astype(jnp.int32)
    return {
        "x": x,
        "positions": positions,
        "norm_mix": 1.0 + nrm(ks[2], (DEPTH, D_MODEL), 0.02),
        "w_in": nrm(ks[3], (DEPTH, D_MODEL, IN_WIDTH), D_MODEL ** -0.5),
        "rel_bias": nrm(ks[4], (DEPTH, A_HEADS, 2 * REL_CLIP + 1), 0.5),
        "lambdas": nrm(ks[5], (DEPTH, 4, HEAD_DIM), 0.1),
        "sub_gain": 1.0 + nrm(ks[6], (DEPTH, 2 * HEAD_DIM), 0.02),
        "b_forget": 2.0 + nrm(ks[7], (DEPTH, C_HEADS), 0.5),
        "w_br_a": nrm(ks[8], (DEPTH, A_WIDTH, D_MODEL), A_WIDTH ** -0.5),
        "w_br_b": nrm(ks[9], (DEPTH, B_WIDTH, D_MODEL), B_WIDTH ** -0.5),
        "w_br_c": nrm(ks[10], (DEPTH, C_WIDTH, D_MODEL), C_WIDTH ** -0.5),
        "w_gate": nrm(ks[11], (DEPTH, D_MODEL, N_BRANCH * D_MODEL), D_MODEL ** -0.5),
        "b_gate": nrm(ks[12], (DEPTH, N_BRANCH * D_MODEL), 0.02),
        "w_out": nrm(ks[13], (DEPTH, D_MODEL, D_MODEL), D_MODEL ** -0.5),
        "norm_mlp": 1.0 + nrm(ks[14], (DEPTH, D_MODEL), 0.02),
        "w_ff1": nrm(ks[15], (DEPTH, D_MODEL, D_FF), D_MODEL ** -0.5),
        "w_ff2": nrm(ks[16], (DEPTH, D_FF, D_MODEL), D_FF ** -0.5),
        "final_norm": 1.0 + nrm(ks[17], (D_MODEL,), 0.02),
    }


def reference(x, positions, norm_mix, w_in, rel_bias, lambdas, sub_gain, b_forget,
              w_br_a, w_br_b, w_br_c, w_gate, b_gate, w_out, norm_mlp, w_ff1, w_ff2, final_norm):
    for l in range(DEPTH):
        x = hybrid_layer(x, positions, norm_mix[l], w_in[l], rel_bias[l], lambdas[l], sub_gain[l], b_forget[l],
                         w_br_a[l], w_br_b[l], w_br_c[l], w_gate[l], b_gate[l], w_out[l],
                         norm_mlp[l], w_ff1[l], w_ff2[l], l)
    return rmsnorm(x, final_norm)
```

```python
import functools
import math

import jax
import jax.numpy as jnp
from jax import lax
from jax.experimental import pallas as pl
from jax.experimental.pallas import tpu as pltpu

D_MODEL = 1024
CHUNK = 64
LEFT_CHUNKS = 8
HEAD_DIM = 64
A_HEADS = 8
B_HEADS = 4
C_HEADS = 8
BRANCH_WIDTH = 512
N_BRANCH = 3
D_FF = 4 * D_MODEL
REL_CLIP = 128
ROPE_THETA = 500000.0
ROPE_DIM = HEAD_DIM // 4
ROPE_HALF = ROPE_DIM // 2
EPS = 1e-6

LANES = 128
N_HEAD_BLOCKS = 9 * BRANCH_WIDTH // LANES
LOG2E = 1.4426950408889634
NEG_BIG = -1e30

TM_DENSE = 512
TQ_FLASH = 512
TK_FLASH = 512
TQ_BAND = 512
SUB_BAND = 2 * CHUNK
WIN_BAND = SUB_BAND + LEFT_CHUNKS * CHUNK
VMEM_LIMIT = 56 * 1024 * 1024

BF16 = jnp.bfloat16
F32 = jnp.float32


def _params(*sem):
    return pltpu.CompilerParams(dimension_semantics=sem, vmem_limit_bytes=VMEM_LIMIT)


def _const_spec(shape):
    nd = len(shape)
    return pl.BlockSpec(shape, lambda *_: (0,) * nd, pipeline_mode=pl.Buffered(1))


def _rms(x, g):
    return x * lax.rsqrt(jnp.mean(x * x, axis=-1, keepdims=True) + EPS) * g


def _rope_table_kernel(ang_ref, cos_ref, sin_ref):
    ang = ang_ref[...]
    lane = lax.broadcasted_iota(jnp.int32, ang.shape, 1) % HEAD_DIM
    sign = jnp.where(lane < ROPE_HALF, -1.0, 1.0)
    cos_ref[...] = jnp.cos(ang)
    sin_ref[...] = jnp.sin(ang) * sign


def _rope_tables(positions):
    T = positions.size
    inv_freq = jnp.power(ROPE_THETA, -jnp.arange(0, ROPE_DIM, 2, dtype=F32) / ROPE_DIM)
    lane = jnp.arange(LANES) % HEAD_DIM
    freq_lane = jnp.where(lane < ROPE_DIM, inv_freq[lane % ROPE_HALF], 0.0)
    ang = positions.reshape(T, 1).astype(F32) * freq_lane[None, :]
    tm = min(1024, T)
    spec = pl.BlockSpec((tm, LANES), lambda i: (i, 0))
    return pl.pallas_call(
        _rope_table_kernel,
        out_shape=(jax.ShapeDtypeStruct((T, LANES), F32),) * 2,
        grid=(T // tm,), in_specs=[spec], out_specs=(spec, spec),
        compiler_params=_params("parallel"), name="rope_tables",
    )(ang)


def _rmsnorm_kernel(x_ref, g_ref, h_ref):
    h_ref[...] = _rms(x_ref[...], g_ref[...]).astype(h_ref.dtype)


def _rmsnorm(x2, g):
    T = x2.shape[0]
    tm = min(1024, T)
    return pl.pallas_call(
        _rmsnorm_kernel,
        out_shape=jax.ShapeDtypeStruct((T, D_MODEL), BF16),
        grid=(T // tm,),
        in_specs=[pl.BlockSpec((tm, D_MODEL), lambda i: (i, 0)), _const_spec((1, D_MODEL))],
        out_specs=pl.BlockSpec((tm, D_MODEL), lambda i: (i, 0)),
        compiler_params=_params("parallel"), name="rmsnorm",
    )(x2, g.reshape(1, D_MODEL))


ROPE_GROUPS = (3, 4)


def _inproj_kernel(h_ref, w_ref, cos_ref, sin_ref, o_ref):
    h = h_ref[...]
    cosf = cos_ref[...]
    sinf = sin_ref[...]
    lane = lax.broadcasted_iota(jnp.int32, cosf.shape, 1) % HEAD_DIM
    first_half = lane < ROPE_HALF
    per_group = BRANCH_WIDTH // LANES
    for c in range(9):
        res = jnp.dot(h, w_ref[:, c * BRANCH_WIDTH:(c + 1) * BRANCH_WIDTH],
                      preferred_element_type=F32)
        for s in range(per_group):
            blk = res[:, s * LANES:(s + 1) * LANES]
            if c in ROPE_GROUPS:
                partner = jnp.where(first_half,
                                    pltpu.roll(blk, LANES - ROPE_HALF, 1),
                                    pltpu.roll(blk, ROPE_HALF, 1))
                blk = blk * cosf + partner * sinf
            o_ref[c * per_group + s] = blk.astype(o_ref.dtype)


def _inproj(h, w_main, cosf, sinf):
    T = h.shape[0]
    tm = min(TM_DENSE, T)
    return pl.pallas_call(
        _inproj_kernel,
        out_shape=jax.ShapeDtypeStruct((N_HEAD_BLOCKS, T, LANES), BF16),
        grid=(T // tm,),
        in_specs=[pl.BlockSpec((tm, D_MODEL), lambda i: (i, 0)),
                  _const_spec(w_main.shape),
                  pl.BlockSpec((tm, LANES), lambda i: (i, 0)),
                  pl.BlockSpec((tm, LANES), lambda i: (i, 0))],
        out_specs=pl.BlockSpec((N_HEAD_BLOCKS, tm, LANES), lambda i: (0, i, 0)),
        compiler_params=_params("parallel"), name="inproj",
    )(h, w_main, cosf, sinf)


def _forget_kernel(h_ref, w_ref, b_ref, f_ref, carry_ref):
    @pl.when(pl.program_id(1) == 0)
    def _():
        carry_ref[...] = jnp.zeros_like(carry_ref)

    fl = jnp.dot(h_ref[0], w_ref[...], preferred_element_type=F32) + b_ref[...]
    ls = jnp.minimum(fl, 0.0) - jnp.log1p(jnp.exp(-jnp.abs(fl)))
    tm = ls.shape[0]
    row = lax.broadcasted_iota(jnp.int32, (tm, tm), 0)
    col = lax.broadcasted_iota(jnp.int32, (tm, tm), 1)
    tri = (col <= row).astype(BF16)
    hi = ls.astype(BF16)
    r1 = ls - hi.astype(F32)
    mid = r1.astype(BF16)
    lo = (r1 - mid.astype(F32)).astype(BF16)
    cs = (jnp.dot(tri, hi, preferred_element_type=F32)
          + jnp.dot(tri, mid, preferred_element_type=F32)
          + jnp.dot(tri, lo, preferred_element_type=F32))
    f = cs + carry_ref[...]
    carry_ref[...] = f[tm - 1:tm, :]
    f_ref[0] = f * LOG2E


def _forget_cumsum(h3, w_fl, b_fl):
    B, S, _ = h3.shape
    tm = min(TM_DENSE, S)
    return pl.pallas_call(
        _forget_kernel,
        out_shape=jax.ShapeDtypeStruct((B, S, LANES), F32),
        grid=(B, S // tm),
        in_specs=[pl.BlockSpec((1, tm, D_MODEL), lambda b, i: (b, i, 0)),
                  _const_spec(w_fl.shape), _const_spec(b_fl.shape)],
        out_specs=pl.BlockSpec((1, tm, LANES), lambda b, i: (b, i, 0)),
        scratch_shapes=[pltpu.VMEM((1, LANES), F32)],
        compiler_params=_params("parallel", "arbitrary"), name="forget_cumsum",
    )(h3, w_fl, b_fl)


def _band_kernel(q_ref, k_ref, v_ref, bias_ref, o_ref):
    qi = pl.program_id(2)
    tq = q_ref.shape[2]
    lane = lax.broadcasted_iota(jnp.int32, (1, LANES), 1)
    pad = WIN_BAND - SUB_BAND

    def sub_block(r, k, v, col0):
        w = k.shape[0]
        q = q_ref[0, 0, r * SUB_BAND:(r + 1) * SUB_BAND, :]
        outs = []
        for hh in range(2):
            qh = jnp.where((lane >= HEAD_DIM) == bool(hh), q, jnp.zeros_like(q))
            s = lax.dot_general(qh, k, (((1,), (1,)), ((), ())), preferred_element_type=F32)
            s = s + bias_ref[hh, :, col0:col0 + w]
            m = jnp.max(s, axis=-1, keepdims=True)
            p = jnp.exp2(s - m)
            l = jnp.sum(p, axis=-1, keepdims=True)
            outs.append(jnp.dot(p.astype(BF16), v, preferred_element_type=F32) / l)
        o = jnp.where(lane < HEAD_DIM, outs[0], outs[1])
        o_ref[0, 0, r * SUB_BAND:(r + 1) * SUB_BAND, :] = o.astype(o_ref.dtype)

    @pl.when(qi == 0)
    def _():
        for r in range(tq // SUB_BAND):
            lo = max(r * SUB_BAND - pad, 0)
            hi = (r + 1) * SUB_BAND
            sub_block(r, k_ref[0, 0, lo:hi, :], v_ref[0, 0, lo:hi, :], WIN_BAND - (hi - lo))

    @pl.when(qi > 0)
    def _():
        for r in range(tq // SUB_BAND):
            start = pl.multiple_of(qi * tq + (r * SUB_BAND - pad), SUB_BAND)
            sub_block(r, k_ref[0, 0, pl.ds(start, WIN_BAND), :],
                      v_ref[0, 0, pl.ds(start, WIN_BAND), :], 0)


def _band_attention(proj, bias, B, S):
    tq = min(TQ_BAND, S)
    nhp = A_HEADS // 2
    return pl.pallas_call(
        _band_kernel,
        out_shape=jax.ShapeDtypeStruct((nhp, B, S, LANES), BF16),
        grid=(B, nhp, S // tq),
        in_specs=[pl.BlockSpec((1, 1, tq, LANES), lambda b, p, i: (p, b, i, 0)),
                  pl.BlockSpec((1, 1, S, LANES), lambda b, p, i: (nhp + p, b, 0, 0)),
                  pl.BlockSpec((1, 1, S, LANES), lambda b, p, i: (2 * nhp + p, b, 0, 0)),
                  pl.BlockSpec((2, SUB_BAND, WIN_BAND), lambda b, p, i: (p, 0, 0))],
        out_specs=pl.BlockSpec((1, 1, tq, LANES), lambda b, p, i: (p, b, i, 0)),
        compiler_params=_params("parallel", "parallel", "arbitrary"), name="band_attention",
    )(proj, proj, proj, bias)


def _band_bias(rel_bias):
    i = jnp.arange(SUB_BAND)
    ci, off = i // CHUNK, i % CHUNK
    j = jnp.arange(WIN_BAND)
    k_off = j[None, :] - LEFT_CHUNKS * CHUNK - ci[:, None] * CHUNK
    valid = (k_off >= -LEFT_CHUNKS * CHUNK) & (k_off < CHUNK)
    rel = jnp.clip(k_off - off[:, None], -REL_CLIP, REL_CLIP) + REL_CLIP
    table = rel_bias.astype(F32)[:, rel] * LOG2E
    return jnp.where(valid[None], table, NEG_BIG)


def _flash_update(qh, k, v, col_bias, row_shift, mask, m_ref, l_ref, acc_ref, c):
    s = lax.dot_general(qh, k, (((1,), (1,)), ((), ())), preferred_element_type=F32)
    if col_bias is not None:
        s = s - col_bias
    if mask is not None:
        s = jnp.where(mask, s, NEG_BIG)
    row_max = jnp.max(s, axis=-1, keepdims=True)
    if row_shift is not None:
        row_max = row_max + row_shift
    m_prev = m_ref[c]
    m_new = jnp.maximum(m_prev, row_max)
    alpha = jnp.exp2(m_prev - m_new)
    t = m_new if row_shift is None else m_new - row_shift
    p = jnp.exp2(s - t)
    l_ref[c] = alpha * l_ref[c] + jnp.sum(p, axis=-1, keepdims=True)
    acc_ref[c] = alpha * acc_ref[c] + jnp.dot(p.astype(BF16), v, preferred_element_type=F32)
    m_ref[c] = m_new


def _flash_body(q_ref, k_ref, v_ref, fq, fk_ref, m_ref, l_ref, acc_ref, *, chunk_causal):
    qi = pl.program_id(2)
    tq = q_ref.shape[2]
    tk = tq
    lane = lax.broadcasted_iota(jnp.int32, (1, LANES), 1)
    q = q_ref[0, 0]
    qs = [jnp.where((lane >= HEAD_DIM) == bool(c), q, jnp.zeros_like(q)) for c in range(2)]
    m_ref[...] = jnp.full(m_ref.shape, NEG_BIG, F32)
    l_ref[...] = jnp.zeros(l_ref.shape, F32)
    acc_ref[...] = jnp.zeros(acc_ref.shape, F32)

    def block(off, mask):
        k = k_ref[0, 0, pl.ds(off, tk), :]
        v = v_ref[0, 0, pl.ds(off, tk), :]
        for c in range(2):
            col_bias = None if fk_ref is None else fk_ref[0, 0, c:c + 1, pl.ds(off, tk)]
            row_shift = None if fq is None else fq[c]
            _flash_update(qs[c], k, v, col_bias, row_shift, mask, m_ref, l_ref, acc_ref, c)

    def full_block(j, carry):
        block(pl.multiple_of(j * tk, tk), None)
        return carry

    lax.fori_loop(0, qi, full_block, 0)

    row = lax.broadcasted_iota(jnp.int32, (tq, tk), 0)
    col = lax.broadcasted_iota(jnp.int32, (tq, tk), 1)
    if chunk_causal:
        mask = (col // CHUNK) <= (row // CHUNK)
    else:
        mask = col <= row
    block(pl.multiple_of(qi * tk, tk), mask)


def _diff_kernel(q_ref, k_ref, v_ref, lam_ref, gain_ref, o_ref, m_ref, l_ref, acc_ref, *, lam_init):
    _flash_body(q_ref, k_ref, v_ref, None, None, m_ref, l_ref, acc_ref, chunk_causal=True)
    lf = lam_ref[...]
    lam = (jnp.exp(jnp.sum(lf[0:1] * lf[1:2], axis=-1, keepdims=True))
           - jnp.exp(jnp.sum(lf[2:3] * lf[3:4], axis=-1, keepdims=True)) + lam_init)
    o = acc_ref[0] / l_ref[0] - lam * (acc_ref[1] / l_ref[1])
    o_ref[0, 0] = (_rms(o, gain_ref[...]) * (1.0 - lam_init)).astype(o_ref.dtype)


def _forget_attn_kernel(q_ref, k_ref, v_ref, fq_ref, fk_ref, o_ref, m_ref, l_ref, acc_ref):
    p = pl.program_id(1)
    lane = lax.broadcasted_iota(jnp.int32, (1, LANES), 1)
    fblk = fq_ref[0]
    fq = [jnp.sum(jnp.where(lane == 2 * p + c, fblk, 0.0), axis=-1, keepdims=True) for c in range(2)]
    _flash_body(q_ref, k_ref, v_ref, fq, fk_ref, m_ref, l_ref, acc_ref, chunk_causal=False)
    o = jnp.where(lane < HEAD_DIM, acc_ref[0] / l_ref[0], acc_ref[1] / l_ref[1])
    o_ref[0, 0] = o.astype(o_ref.dtype)


def _flash_scratch(tq):
    return [pltpu.VMEM((2, tq, 1), F32), pltpu.VMEM((2, tq, 1), F32), pltpu.VMEM((2, tq, LANES), F32)]


def _flash_specs(first_block, nblk, tq, S):
    return [pl.BlockSpec((1, 1, tq, LANES), lambda b, p, i: (first_block + p, b, i, 0)),
            pl.BlockSpec((1, 1, S, LANES), lambda b, p, i: (first_block + nblk + p, b, 0, 0)),
            pl.BlockSpec((1, 1, S, LANES), lambda b, p, i: (first_block + 2 * nblk + p, b, 0, 0))]


def _diff_attention(proj, lambdas, sub_gain, lam_init, B, S):
    tq = min(TQ_FLASH, S)
    nblk = BRANCH_WIDTH // LANES
    return pl.pallas_call(
        functools.partial(_diff_kernel, lam_init=lam_init),
        out_shape=jax.ShapeDtypeStruct((nblk, B, S, LANES), BF16),
        grid=(B, nblk, S // tq),
        in_specs=_flash_specs(3 * nblk, nblk, tq, S) + [
            _const_spec(lambdas.shape), _const_spec((1, LANES))],
        out_specs=pl.BlockSpec((1, 1, tq, LANES), lambda b, p, i: (p, b, i, 0)),
        scratch_shapes=_flash_scratch(tq),
        compiler_params=_params("parallel", "parallel", "arbitrary"), name="diff_attention",
    )(proj, proj, proj, lambdas.astype(F32), sub_gain.astype(F32).reshape(1, LANES))


def _forget_attention(proj, f_tok, f_row, B, S):
    tq = min(TQ_FLASH, S)
    nblk = BRANCH_WIDTH // LANES
    return pl.pallas_call(
        _forget_attn_kernel,
        out_shape=jax.ShapeDtypeStruct((nblk, B, S, LANES), BF16),
        grid=(B, nblk, S // tq),
        in_specs=_flash_specs(6 * nblk, nblk, tq, S) + [
            pl.BlockSpec((1, tq, LANES), lambda b, p, i: (b, i, 0)),
            pl.BlockSpec((1, 1, 2, S), lambda b, p, i: (b, p, 0, 0))],
        out_specs=pl.BlockSpec((1, 1, tq, LANES), lambda b, p, i: (p, b, i, 0)),
        scratch_shapes=_flash_scratch(tq),
        compiler_params=_params("parallel", "parallel", "arbitrary"), name="forget_attention",
    )(proj, proj, proj, f_tok, f_row)


def _merge_kernel(x_ref, h_ref, ya_ref, yb_ref, yc_ref, wg_ref, bg_ref, wbr_ref, wo_ref, o_ref):
    h = h_ref[...]
    merged = None
    for b, y_ref in enumerate((ya_ref, yb_ref, yc_ref)):
        y = jnp.concatenate([y_ref[i] for i in range(y_ref.shape[0])], axis=1)
        t = jnp.dot(y, wbr_ref[b], preferred_element_type=F32)
        z = jnp.dot(h, wg_ref[:, b * D_MODEL:(b + 1) * D_MODEL], preferred_element_type=F32)
        g = jax.nn.sigmoid(z + bg_ref[:, b * D_MODEL:(b + 1) * D_MODEL])
        merged = g * t if merged is None else merged + g * t
    o_ref[...] = x_ref[...] + jnp.dot(merged.astype(BF16), wo_ref[...], preferred_element_type=F32)


def _merge(x2, h, ya, yb, yc, w_gate, b_gate, w_br, w_out):
    T = x2.shape[0]
    tm = min(TM_DENSE, T)
    nblk = ya.shape[0]
    row = pl.BlockSpec((tm, D_MODEL), lambda i: (i, 0))
    yspec = pl.BlockSpec((nblk, tm, LANES), lambda i: (0, i, 0))
    return pl.pallas_call(
        _merge_kernel,
        out_shape=jax.ShapeDtypeStruct((T, D_MODEL), F32),
        grid=(T // tm,),
        in_specs=[row, row, yspec, yspec, yspec, _const_spec(w_gate.shape), _const_spec(b_gate.shape),
                  _const_spec(w_br.shape), _const_spec(w_out.shape)],
        out_specs=row,
        compiler_params=_params("parallel"), name="merge_out",
    )(x2, h, ya, yb, yc, w_gate, b_gate, w_br, w_out)


FF_CHUNK = 1024


def _mlp_kernel(x_ref, gn_ref, w1_ref, w2_ref, gnext_ref, *out_refs):
    x = x_ref[...]
    h2 = _rms(x, gn_ref[...]).astype(BF16)
    acc = x
    for c in range(D_FF // FF_CHUNK):
        a = jnp.dot(h2, w1_ref[:, c * FF_CHUNK:(c + 1) * FF_CHUNK], preferred_element_type=F32)
        a = jnp.square(jnp.maximum(a, 0.0))
        acc = acc + jnp.dot(a.astype(BF16), w2_ref[c * FF_CHUNK:(c + 1) * FF_CHUNK, :],
                            preferred_element_type=F32)
    normed = _rms(acc, gnext_ref[...])
    if len(out_refs) == 2:
        out_refs[0][...] = acc
        out_refs[1][...] = normed.astype(out_refs[1].dtype)
    else:
        out_refs[0][...] = normed.astype(out_refs[0].dtype)


def _mlp(x2, g_mlp, w1, w2, g_next, last):
    T = x2.shape[0]
    tm = min(TM_DENSE, T)
    row = pl.BlockSpec((tm, D_MODEL), lambda i: (i, 0))
    if last:
        out_shape, out_specs = jax.ShapeDtypeStruct((T, D_MODEL), F32), row
    else:
        out_shape = (jax.ShapeDtypeStruct((T, D_MODEL), F32), jax.ShapeDtypeStruct((T, D_MODEL), BF16))
        out_specs = (row, row)
    return pl.pallas_call(
        _mlp_kernel, out_shape=out_shape, grid=(T // tm,),
        in_specs=[row, _const_spec((1, D_MODEL)), _const_spec(w1.shape), _const_spec(w2.shape),
                  _const_spec((1, D_MODEL))],
        out_specs=out_specs,
        compiler_params=_params("parallel"), name="mlp",
    )(x2, g_mlp.reshape(1, D_MODEL), w1, w2, g_next.reshape(1, D_MODEL))


def kernel(x, positions, norm_mix, w_in, rel_bias, lambdas, sub_gain, b_forget, w_br_a, w_br_b, w_br_c,
           w_gate, b_gate, w_out, norm_mlp, w_ff1, w_ff2, final_norm):
    B, S, D = x.shape
    T = B * S
    depth = w_in.shape[0]
    n_main = 9 * BRANCH_WIDTH
    q_scale = HEAD_DIM ** -0.5 * LOG2E
    col_scale = jnp.ones((n_main,), F32)
    for grp in (0, 3, 6):
        col_scale = col_scale.at[grp * BRANCH_WIDTH:(grp + 1) * BRANCH_WIDTH].set(q_scale)

    cosf, sinf = _rope_tables(positions)
    x2 = x.reshape(T, D)
    h = _rmsnorm(x2, norm_mix[0])
    out = None
    for l in range(depth):
        lam_init = 0.8 - 0.6 * math.exp(-0.3 * l)
        w_main = (w_in[l, :, :n_main] * col_scale[None, :]).astype(BF16)
        w_fl = jnp.zeros((D, LANES), F32).at[:, :C_HEADS].set(w_in[l, :, n_main:]).astype(BF16)
        b_fl = jnp.zeros((1, LANES), F32).at[0, :C_HEADS].set(b_forget[l].astype(F32))

        proj = _inproj(h, w_main, cosf, sinf).reshape(N_HEAD_BLOCKS, B, S, LANES)
        f_tok = _forget_cumsum(h.reshape(B, S, D), w_fl, b_fl)
        f_row = f_tok[:, :, :C_HEADS].transpose(0, 2, 1).reshape(B, C_HEADS // 2, 2, S)

        ya = _band_attention(proj, _band_bias(rel_bias[l]), B, S)
        yb = _diff_attention(proj, lambdas[l], sub_gain[l], lam_init, B, S)
        yc = _forget_attention(proj, f_tok, f_row, B, S)

        w_br = jnp.stack([w_br_a[l], w_br_b[l], w_br_c[l]]).astype(BF16)
        nblk = BRANCH_WIDTH // LANES
        x2 = _merge(x2, h, ya.reshape(nblk, T, LANES), yb.reshape(nblk, T, LANES), yc.reshape(nblk, T, LANES),
                    w_gate[l].astype(BF16), b_gate[l].astype(F32).reshape(1, -1), w_br, w_out[l].astype(BF16))
        last = l == depth - 1
        g_next = final_norm if last else norm_mix[l + 1]
        res = _mlp(x2, norm_mlp[l], w_ff1[l].astype(BF16), w_ff2[l].astype(BF16), g_next, last)
        if last:
            out = res
        else:
            x2, h = res
    return out.reshape(B, S, D)
```

```python
import functools
import math

import jax
import jax.numpy as jnp
from jax import lax
from jax.experimental import pallas as pl
from jax.experimental.pallas import tpu as pltpu

D_MODEL = 1024
CHUNK = 64
LEFT_CHUNKS = 8
HEAD_DIM = 64
A_HEADS = 8
B_HEADS = 4
C_HEADS = 8
BRANCH_WIDTH = 512
N_BRANCH = 3
D_FF = 4 * D_MODEL
REL_CLIP = 128
ROPE_THETA = 500000.0
ROPE_DIM = HEAD_DIM // 4
ROPE_HALF = ROPE_DIM // 2
EPS = 1e-6

LANES = 128
LOG2E = 1.4426950408889634
NEG_BIG = -1e30

TM_DENSE = 512
TQ_FLASH = 512
TQ_BAND = 512
SUB_BAND = 2 * CHUNK
WIN_BAND = SUB_BAND + LEFT_CHUNKS * CHUNK
VMEM_LIMIT = 56 * 1024 * 1024

BF16 = jnp.bfloat16
F32 = jnp.float32


def _params(*sem):
    return pltpu.CompilerParams(dimension_semantics=sem, vmem_limit_bytes=VMEM_LIMIT)


def _const_spec(shape):
    nd = len(shape)
    return pl.BlockSpec(shape, lambda *_: (0,) * nd, pipeline_mode=pl.Buffered(1))


def _rms(x, g):
    return x * lax.rsqrt(jnp.mean(x * x, axis=-1, keepdims=True) + EPS) * g


def _rope_table_kernel(ang_ref, cos_ref, sin_ref):
    ang = ang_ref[...]
    lane = lax.broadcasted_iota(jnp.int32, ang.shape, 1) % HEAD_DIM
    sign = jnp.where(lane < ROPE_HALF, -1.0, 1.0)
    cos_ref[...] = jnp.cos(ang)
    sin_ref[...] = jnp.sin(ang) * sign


def _rope_tables(positions):
    T = positions.size
    inv_freq = jnp.power(ROPE_THETA, -jnp.arange(0, ROPE_DIM, 2, dtype=F32) / ROPE_DIM)
    lane = jnp.arange(LANES) % HEAD_DIM
    freq_lane = jnp.where(lane < ROPE_DIM, inv_freq[lane % ROPE_HALF], 0.0)
    ang = positions.reshape(T, 1).astype(F32) * freq_lane[None, :]
    tm = min(1024, T)
    spec = pl.BlockSpec((tm, LANES), lambda i: (i, 0))
    return pl.pallas_call(
        _rope_table_kernel,
        out_shape=(jax.ShapeDtypeStruct((T, LANES), F32),) * 2,
        grid=(T // tm,), in_specs=[spec], out_specs=(spec, spec),
        compiler_params=_params("parallel"), name="rope_tables",
    )(ang)


def _rmsnorm_kernel(x_ref, g_ref, h_ref):
    h_ref[...] = _rms(x_ref[...], g_ref[...]).astype(h_ref.dtype)


def _rmsnorm(x2, g):
    T = x2.shape[0]
    tm = min(1024, T)
    return pl.pallas_call(
        _rmsnorm_kernel,
        out_shape=jax.ShapeDtypeStruct((T, D_MODEL), BF16),
        grid=(T // tm,),
        in_specs=[pl.BlockSpec((tm, D_MODEL), lambda i: (i, 0)), _const_spec((1, D_MODEL))],
        out_specs=pl.BlockSpec((tm, D_MODEL), lambda i: (i, 0)),
        compiler_params=_params("parallel"), name="rmsnorm",
    )(x2, g.reshape(1, D_MODEL))


N_ROW_GROUPS = 7
ROPE_GROUPS = (3, 4)
GROUP_BLOCKS = BRANCH_WIDTH // LANES


def _inproj_kernel(h_ref, w_ref, wvt_ref, cos_ref, sin_ref, o_ref, vt_ref):
    h = h_ref[...]
    cosf = cos_ref[...]
    sinf = sin_ref[...]
    lane = lax.broadcasted_iota(jnp.int32, cosf.shape, 1) % HEAD_DIM
    first_half = lane < ROPE_HALF
    for c in range(N_ROW_GROUPS):
        res = jnp.dot(h, w_ref[:, c * BRANCH_WIDTH:(c + 1) * BRANCH_WIDTH],
                      preferred_element_type=F32)
        for s in range(GROUP_BLOCKS):
            blk = res[:, s * LANES:(s + 1) * LANES]
            if c in ROPE_GROUPS:
                partner = jnp.where(first_half,
                                    pltpu.roll(blk, LANES - ROPE_HALF, 1),
                                    pltpu.roll(blk, ROPE_HALF, 1))
                blk = blk * cosf + partner * sinf
            o_ref[c * GROUP_BLOCKS + s] = blk.astype(o_ref.dtype)
    for c in range(2):
        res_t = lax.dot_general(wvt_ref[c * BRANCH_WIDTH:(c + 1) * BRANCH_WIDTH, :], h,
                                (((1,), (1,)), ((), ())), preferred_element_type=F32)
        for s in range(GROUP_BLOCKS):
            vt_ref[c * GROUP_BLOCKS + s] = res_t[s * LANES:(s + 1) * LANES, :].astype(vt_ref.dtype)


def _inproj(h, w_main, w_vt, cosf, sinf):
    T = h.shape[0]
    tm = min(TM_DENSE, T)
    n_row = N_ROW_GROUPS * GROUP_BLOCKS
    n_vt = 2 * GROUP_BLOCKS
    return pl.pallas_call(
        _inproj_kernel,
        out_shape=(jax.ShapeDtypeStruct((n_row, T, LANES), BF16),
                   jax.ShapeDtypeStruct((n_vt, LANES, T), BF16)),
        grid=(T // tm,),
        in_specs=[pl.BlockSpec((tm, D_MODEL), lambda i: (i, 0)),
                  _const_spec(w_main.shape), _const_spec(w_vt.shape),
                  pl.BlockSpec((tm, LANES), lambda i: (i, 0)),
                  pl.BlockSpec((tm, LANES), lambda i: (i, 0))],
        out_specs=(pl.BlockSpec((n_row, tm, LANES), lambda i: (0, i, 0)),
                   pl.BlockSpec((n_vt, LANES, tm), lambda i: (0, 0, i))),
        compiler_params=_params("parallel"), name="inproj",
    )(h, w_main, w_vt, cosf, sinf)


def _split3(x):
    hi = x.astype(BF16)
    r1 = x - hi.astype(F32)
    mid = r1.astype(BF16)
    lo = (r1 - mid.astype(F32)).astype(BF16)
    return hi, mid, lo


def _forget_kernel(h_ref, w_ref, b_ref, f_ref, frep_ref, carry_ref):
    @pl.when(pl.program_id(1) == 0)
    def _():
        carry_ref[...] = jnp.zeros_like(carry_ref)

    fl = jnp.dot(h_ref[0], w_ref[...], preferred_element_type=F32) + b_ref[...]
    ls = jnp.minimum(fl, 0.0) - jnp.log1p(jnp.exp(-jnp.abs(fl)))
    tm = ls.shape[0]
    row = lax.broadcasted_iota(jnp.int32, (tm, tm), 0)
    col = lax.broadcasted_iota(jnp.int32, (tm, tm), 1)
    tri = (col <= row).astype(BF16)
    cs = sum(jnp.dot(tri, part, preferred_element_type=F32) for part in _split3(ls))
    f = cs + carry_ref[...]
    carry_ref[...] = f[tm - 1:tm, :]
    f2 = f * LOG2E
    f_ref[0] = f2
    parts = jnp.concatenate(_split3(f2), axis=1)
    src = lax.broadcasted_iota(jnp.int32, (3 * LANES, C_HEADS * LANES), 0) % LANES
    dst = lax.broadcasted_iota(jnp.int32, (3 * LANES, C_HEADS * LANES), 1) // LANES
    spread = (src == dst).astype(BF16)
    rep = jnp.dot(parts, spread, preferred_element_type=F32)
    for hd in range(C_HEADS):
        frep_ref[0, hd] = rep[:, hd * LANES:(hd + 1) * LANES]


def _forget_cumsum(h3, w_fl, b_fl):
    B, S, _ = h3.shape
    tm = min(TM_DENSE, S)
    return pl.pallas_call(
        _forget_kernel,
        out_shape=(jax.ShapeDtypeStruct((B, S, LANES), F32),
                   jax.ShapeDtypeStruct((B, C_HEADS, S, LANES), F32)),
        grid=(B, S // tm),
        in_specs=[pl.BlockSpec((1, tm, D_MODEL), lambda b, i: (b, i, 0)),
                  _const_spec(w_fl.shape), _const_spec(b_fl.shape)],
        out_specs=(pl.BlockSpec((1, tm, LANES), lambda b, i: (b, i, 0)),
                   pl.BlockSpec((1, C_HEADS, tm, LANES), lambda b, i: (b, 0, i, 0))),
        scratch_shapes=[pltpu.VMEM((1, LANES), F32)],
        compiler_params=_params("parallel", "arbitrary"), name="forget_cumsum",
    )(h3, w_fl, b_fl)


def _band_kernel(q_ref, k_ref, v_ref, bias_ref, o_ref):
    qi = pl.program_id(2)
    tq = q_ref.shape[2]
    lane = lax.broadcasted_iota(jnp.int32, (1, LANES), 1)
    pad = WIN_BAND - SUB_BAND

    def sub_block(r, k, v, col0):
        w = k.shape[0]
        q = q_ref[0, 0, r * SUB_BAND:(r + 1) * SUB_BAND, :]
        outs = []
        for hh in range(2):
            qh = jnp.where((lane >= HEAD_DIM) == bool(hh), q, jnp.zeros_like(q))
            s = lax.dot_general(qh, k, (((1,), (1,)), ((), ())), preferred_element_type=F32)
            s = s + bias_ref[hh, :, col0:col0 + w]
            m = jnp.max(s, axis=-1, keepdims=True)
            p = jnp.exp2(s - m)
            l = jnp.sum(p, axis=-1, keepdims=True)
            outs.append(jnp.dot(p.astype(BF16), v, preferred_element_type=F32) / l)
        o = jnp.where(lane < HEAD_DIM, outs[0], outs[1])
        o_ref[0, 0, r * SUB_BAND:(r + 1) * SUB_BAND, :] = o.astype(o_ref.dtype)

    @pl.when(qi == 0)
    def _():
        for r in range(tq // SUB_BAND):
            lo = max(r * SUB_BAND - pad, 0)
            hi = (r + 1) * SUB_BAND
            sub_block(r, k_ref[0, 0, lo:hi, :], v_ref[0, 0, lo:hi, :], WIN_BAND - (hi - lo))

    @pl.when(qi > 0)
    def _():
        for r in range(tq // SUB_BAND):
            start = pl.multiple_of(qi * tq + (r * SUB_BAND - pad), SUB_BAND)
            sub_block(r, k_ref[0, 0, pl.ds(start, WIN_BAND), :],
                      v_ref[0, 0, pl.ds(start, WIN_BAND), :], 0)


def _band_attention(proj, bias, B, S):
    tq = min(TQ_BAND, S)
    nhp = A_HEADS // 2
    return pl.pallas_call(
        _band_kernel,
        out_shape=jax.ShapeDtypeStruct((nhp, B, S, LANES), BF16),
        grid=(B, nhp, S // tq),
        in_specs=[pl.BlockSpec((1, 1, tq, LANES), lambda b, p, i: (p, b, i, 0)),
                  pl.BlockSpec((1, 1, S, LANES), lambda b, p, i: (nhp + p, b, 0, 0)),
                  pl.BlockSpec((1, 1, S, LANES), lambda b, p, i: (2 * nhp + p, b, 0, 0)),
                  pl.BlockSpec((2, SUB_BAND, WIN_BAND), lambda b, p, i: (p, 0, 0))],
        out_specs=pl.BlockSpec((1, 1, tq, LANES), lambda b, p, i: (p, b, i, 0)),
        compiler_params=_params("parallel", "parallel", "arbitrary"), name="band_attention",
    )(proj, proj, proj, bias)


def _band_bias(rel_bias):
    i = jnp.arange(SUB_BAND)
    ci, off = i // CHUNK, i % CHUNK
    j = jnp.arange(WIN_BAND)
    k_off = j[None, :] - LEFT_CHUNKS * CHUNK - ci[:, None] * CHUNK
    valid = (k_off >= -LEFT_CHUNKS * CHUNK) & (k_off < CHUNK)
    rel = jnp.clip(k_off - off[:, None], -REL_CLIP, REL_CLIP) + REL_CLIP
    table = rel_bias.astype(F32)[:, rel] * LOG2E
    return jnp.where(valid[None], table, NEG_BIG)


def _flash_update(k, vt, qh, key_bias, query_shift, mask, m_ref, l_ref, acc_ref, c):
    s = lax.dot_general(k, qh, (((1,), (1,)), ((), ())), preferred_element_type=F32)
    if key_bias is not None:
        s = s - jnp.concatenate([key_bias] * (s.shape[1] // LANES), axis=1)
    if mask is not None:
        s = jnp.where(mask, s, NEG_BIG)
    col_max = jnp.max(s, axis=0, keepdims=True)
    if query_shift is not None:
        col_max = col_max + query_shift
    m_prev = m_ref[c]
    m_new = jnp.maximum(m_prev, col_max)
    alpha = jnp.exp2(m_prev - m_new)
    t = m_new if query_shift is None else m_new - query_shift
    p = jnp.exp2(s - t)
    l_ref[c] = alpha * l_ref[c] + jnp.sum(p, axis=0, keepdims=True)
    acc_ref[c] = alpha * acc_ref[c] + jnp.dot(vt, p.astype(BF16), preferred_element_type=F32)
    m_ref[c] = m_new


def _flash_body(q_ref, k_ref, vt_ref, fq_ref, frep_ref, m_ref, l_ref, acc_ref, *, chunk_causal):
    qi = pl.program_id(2)
    tq = q_ref.shape[2]
    tk = tq
    lane = lax.broadcasted_iota(jnp.int32, (1, LANES), 1)
    q = q_ref[0, 0]
    qs = [jnp.where(lane < HEAD_DIM, q, jnp.zeros_like(q)), jnp.where(lane >= HEAD_DIM, q, jnp.zeros_like(q))]
    m_ref[...] = jnp.full(m_ref.shape, NEG_BIG, F32)
    l_ref[...] = jnp.zeros(l_ref.shape, F32)
    acc_ref[...] = jnp.zeros(acc_ref.shape, F32)
    q_off = pl.multiple_of(qi * tq, tq)

    def block(off, mask):
        k = k_ref[0, 0, pl.ds(off, tk), :]
        vt = vt_ref[0, :, pl.ds(off, tk)]
        for c in range(2):
            key_bias = None if frep_ref is None else frep_ref[0, c, pl.ds(off, tk), :]
            query_shift = None if fq_ref is None else fq_ref[0, 0, c:c + 1, pl.ds(q_off, tq)]
            _flash_update(k, vt, qs[c], key_bias, query_shift, mask, m_ref, l_ref, acc_ref, c)

    def full_block(j, carry):
        block(pl.multiple_of(j * tk, tk), None)
        return carry

    lax.fori_loop(0, qi, full_block, 0)

    key = lax.broadcasted_iota(jnp.int32, (tk, tq), 0)
    qry = lax.broadcasted_iota(jnp.int32, (tk, tq), 1)
    if chunk_causal:
        mask = (key // CHUNK) <= (qry // CHUNK)
    else:
        mask = key <= qry
    block(q_off, mask)


def _diff_kernel(q_ref, k_ref, vt_ref, lam_ref, gain_ref, o_ref, m_ref, l_ref, acc_ref, *, lam_init):
    _flash_body(q_ref, k_ref, vt_ref, None, None, m_ref, l_ref, acc_ref, chunk_causal=True)
    lf = lam_ref[...]
    lam = (jnp.exp(jnp.sum(lf[0:1] * lf[1:2], axis=-1, keepdims=True))
           - jnp.exp(jnp.sum(lf[2:3] * lf[3:4], axis=-1, keepdims=True)) + lam_init)
    o_t = acc_ref[0] / l_ref[0] - lam * (acc_ref[1] / l_ref[1])
    o_ref[0, 0] = (_rms(o_t.T, gain_ref[...]) * (1.0 - lam_init)).astype(o_ref.dtype)


def _forget_attn_kernel(q_ref, k_ref, vt_ref, fq_ref, frep_ref, o_ref, m_ref, l_ref, acc_ref):
    _flash_body(q_ref, k_ref, vt_ref, fq_ref, frep_ref, m_ref, l_ref, acc_ref, chunk_causal=False)
    feat = lax.broadcasted_iota(jnp.int32, (LANES, 1), 0)
    o_t = jnp.where(feat < HEAD_DIM, acc_ref[0] / l_ref[0], acc_ref[1] / l_ref[1])
    o_ref[0, 0] = o_t.T.astype(o_ref.dtype)


def _flash_scratch(tq):
    return [pltpu.VMEM((2, 1, tq), F32), pltpu.VMEM((2, 1, tq), F32), pltpu.VMEM((2, LANES, tq), F32)]


def _flash_specs(q_block, k_block, vt_block, tq, S):
    return [pl.BlockSpec((1, 1, tq, LANES), lambda b, p, i: (q_block + p, b, i, 0)),
            pl.BlockSpec((1, 1, S, LANES), lambda b, p, i: (k_block + p, b, 0, 0)),
            pl.BlockSpec((1, LANES, S), lambda b, p, i: (vt_block + p, 0, b))]


def _diff_attention(proj, vt, lambdas, sub_gain, lam_init, B, S):
    tq = min(TQ_FLASH, S)
    return pl.pallas_call(
        functools.partial(_diff_kernel, lam_init=lam_init),
        out_shape=jax.ShapeDtypeStruct((GROUP_BLOCKS, B, S, LANES), BF16),
        grid=(B, GROUP_BLOCKS, S // tq),
        in_specs=_flash_specs(3 * GROUP_BLOCKS, 4 * GROUP_BLOCKS, 0, tq, S) + [
            _const_spec(lambdas.shape), _const_spec((1, LANES))],
        out_specs=pl.BlockSpec((1, 1, tq, LANES), lambda b, p, i: (p, b, i, 0)),
        scratch_shapes=_flash_scratch(tq),
        compiler_params=_params("parallel", "parallel", "arbitrary"), name="diff_attention",
    )(proj, proj, vt, lambdas.astype(F32), sub_gain.astype(F32).reshape(1, LANES))


def _forget_attention(proj, vt, f_row, f_rep, B, S):
    tq = min(TQ_FLASH, S)
    return pl.pallas_call(
        _forget_attn_kernel,
        out_shape=jax.ShapeDtypeStruct((GROUP_BLOCKS, B, S, LANES), BF16),
        grid=(B, GROUP_BLOCKS, S // tq),
        in_specs=_flash_specs(5 * GROUP_BLOCKS, 6 * GROUP_BLOCKS, GROUP_BLOCKS, tq, S) + [
            pl.BlockSpec((1, 1, 2, S), lambda b, p, i: (b, p, 0, 0)),
            pl.BlockSpec((1, 2, S, LANES), lambda b, p, i: (b, p, 0, 0), pipeline_mode=pl.Buffered(1))],
        out_specs=pl.BlockSpec((1, 1, tq, LANES), lambda b, p, i: (p, b, i, 0)),
        scratch_shapes=_flash_scratch(tq),
        compiler_params=_params("parallel", "parallel", "arbitrary"), name="forget_attention",
    )(proj, proj, vt, f_row, f_rep)


def _merge_kernel(x_ref, h_ref, ya_ref, yb_ref, yc_ref, wg_ref, bg_ref, wbr_ref, wo_ref, o_ref):
    h = h_ref[...]
    merged = None
    for b, y_ref in enumerate((ya_ref, yb_ref, yc_ref)):
        y = jnp.concatenate([y_ref[i] for i in range(y_ref.shape[0])], axis=1)
        t = jnp.dot(y, wbr_ref[b], preferred_element_type=F32)
        z = jnp.dot(h, wg_ref[:, b * D_MODEL:(b + 1) * D_MODEL], preferred_element_type=F32)
        g = jax.nn.sigmoid(z + bg_ref[:, b * D_MODEL:(b + 1) * D_MODEL])
        merged = g * t if merged is None else merged + g * t
    o_ref[...] = x_ref[...] + jnp.dot(merged.astype(BF16), wo_ref[...], preferred_element_type=F32)


def _merge(x2, h, ya, yb, yc, w_gate, b_gate, w_br, w_out):
    T = x2.shape[0]
    tm = min(TM_DENSE, T)
    nblk = ya.shape[0]
    row = pl.BlockSpec((tm, D_MODEL), lambda i: (i, 0))
    yspec = pl.BlockSpec((nblk, tm, LANES), lambda i: (0, i, 0))
    return pl.pallas_call(
        _merge_kernel,
        out_shape=jax.ShapeDtypeStruct((T, D_MODEL), F32),
        grid=(T // tm,),
        in_specs=[row, row, yspec, yspec, yspec, _const_spec(w_gate.shape), _const_spec(b_gate.shape),
                  _const_spec(w_br.shape), _const_spec(w_out.shape)],
        out_specs=row,
        compiler_params=_params("parallel"), name="merge_out",
    )(x2, h, ya, yb, yc, w_gate, b_gate, w_br, w_out)


FF_CHUNK = 1024


def _mlp_kernel(x_ref, gn_ref, w1_ref, w2_ref, gnext_ref, *out_refs):
    x = x_ref[...]
    h2 = _rms(x, gn_ref[...]).astype(BF16)
    acc = x
    for c in range(D_FF // FF_CHUNK):
        a = jnp.dot(h2, w1_ref[:, c * FF_CHUNK:(c + 1) * FF_CHUNK], preferred_element_type=F32)
        a = jnp.square(jnp.maximum(a, 0.0))
        acc = acc + jnp.dot(a.astype(BF16), w2_ref[c * FF_CHUNK:(c + 1) * FF_CHUNK, :],
                            preferred_element_type=F32)
    normed = _rms(acc, gnext_ref[...])
    if len(out_refs) == 2:
        out_refs[0][...] = acc
        out_refs[1][...] = normed.astype(out_refs[1].dtype)
    else:
        out_refs[0][...] = normed.astype(out_refs[0].dtype)


def _mlp(x2, g_mlp, w1, w2, g_next, last):
    T = x2.shape[0]
    tm = min(TM_DENSE, T)
    row = pl.BlockSpec((tm, D_MODEL), lambda i: (i, 0))
    if last:
        out_shape, out_specs = jax.ShapeDtypeStruct((T, D_MODEL), F32), row
    else:
        out_shape = (jax.ShapeDtypeStruct((T, D_MODEL), F32), jax.ShapeDtypeStruct((T, D_MODEL), BF16))
        out_specs = (row, row)
    return pl.pallas_call(
        _mlp_kernel, out_shape=out_shape, grid=(T // tm,),
        in_specs=[row, _const_spec((1, D_MODEL)), _const_spec(w1.shape), _const_spec(w2.shape),
                  _const_spec((1, D_MODEL))],
        out_specs=out_specs,
        compiler_params=_params("parallel"), name="mlp",
    )(x2, g_mlp.reshape(1, D_MODEL), w1, w2, g_next.reshape(1, D_MODEL))


def kernel(x, positions, norm_mix, w_in, rel_bias, lambdas, sub_gain, b_forget, w_br_a, w_br_b, w_br_c,
           w_gate, b_gate, w_out, norm_mlp, w_ff1, w_ff2, final_norm):
    B, S, D = x.shape
    T = B * S
    depth = w_in.shape[0]
    n_main = 9 * BRANCH_WIDTH
    q_scale = HEAD_DIM ** -0.5 * LOG2E
    col_scale = jnp.ones((n_main,), F32)
    for grp in (0, 3, 6):
        col_scale = col_scale.at[grp * BRANCH_WIDTH:(grp + 1) * BRANCH_WIDTH].set(q_scale)

    cosf, sinf = _rope_tables(positions)
    x2 = x.reshape(T, D)
    h = _rmsnorm(x2, norm_mix[0])
    out = None
    for l in range(depth):
        lam_init = 0.8 - 0.6 * math.exp(-0.3 * l)
        w_all = w_in[l, :, :n_main] * col_scale[None, :]
        row_groups = [0, 1, 2, 3, 4, 6, 7]
        w_main = jnp.concatenate([w_all[:, g * BRANCH_WIDTH:(g + 1) * BRANCH_WIDTH] for g in row_groups],
                                 axis=1).astype(BF16)
        w_vt = jnp.concatenate([w_all[:, g * BRANCH_WIDTH:(g + 1) * BRANCH_WIDTH] for g in (5, 8)],
                               axis=1).T.astype(BF16)
        w_fl = jnp.zeros((D, LANES), F32).at[:, :C_HEADS].set(w_in[l, :, n_main:]).astype(BF16)
        b_fl = jnp.zeros((1, LANES), F32).at[0, :C_HEADS].set(b_forget[l].astype(F32))

        proj, vt = _inproj(h, w_main, w_vt, cosf, sinf)
        proj = proj.reshape(N_ROW_GROUPS * GROUP_BLOCKS, B, S, LANES)
        f_tok, f_rep = _forget_cumsum(h.reshape(B, S, D), w_fl, b_fl)
        f_row = f_tok[:, :, :C_HEADS].transpose(0, 2, 1).reshape(B, C_HEADS // 2, 2, S)

        ya = _band_attention(proj, _band_bias(rel_bias[l]), B, S)
        yb = _diff_attention(proj, vt, lambdas[l], sub_gain[l], lam_init, B, S)
        yc = _forget_attention(proj, vt, f_row, f_rep, B, S)

        w_br = jnp.stack([w_br_a[l], w_br_b[l], w_br_c[l]]).astype(BF16)
        x2 = _merge(x2, h, ya.reshape(GROUP_BLOCKS, T, LANES), yb.reshape(GROUP_BLOCKS, T, LANES),
                    yc.reshape(GROUP_BLOCKS, T, LANES), w_gate[l].astype(BF16),
                    b_gate[l].astype(F32).reshape(1, -1), w_br, w_out[l].astype(BF16))
        last = l == depth - 1
        g_next = final_norm if last else norm_mix[l + 1]
        res = _mlp(x2, norm_mlp[l], w_ff1[l].astype(BF16), w_ff2[l].astype(BF16), g_next, last)
        if last:
            out = res
        else:
            x2, h = res
    return out.reshape(B, S, D)
```

```python
import functools
import math

import jax
import jax.numpy as jnp
import numpy as np
from jax import lax
from jax.experimental import pallas as pl
from jax.experimental.pallas import tpu as pltpu

D_MODEL = 1024
CHUNK = 64
LEFT_CHUNKS = 8
HEAD_DIM = 64
A_HEADS = 8
B_HEADS = 4
C_HEADS = 8
BRANCH_WIDTH = 512
N_BRANCH = 3
D_FF = 4 * D_MODEL
REL_CLIP = 128
ROPE_THETA = 500000.0
ROPE_DIM = HEAD_DIM // 4
ROPE_HALF = ROPE_DIM // 2
EPS = 1e-6

LANES = 128
LOG2E = 1.4426950408889634
NEG_BIG = -1e30

TM_DENSE = 512
TQ_FLASH = 512
TQ_BAND = 512
SUB_BAND = 2 * CHUNK
WIN_BAND = SUB_BAND + LEFT_CHUNKS * CHUNK
VMEM_LIMIT = 56 * 1024 * 1024

BF16 = jnp.bfloat16
F32 = jnp.float32


def _params(*sem):
    return pltpu.CompilerParams(dimension_semantics=sem, vmem_limit_bytes=VMEM_LIMIT)


def _const_spec(shape):
    nd = len(shape)
    return pl.BlockSpec(shape, lambda *_: (0,) * nd, pipeline_mode=pl.Buffered(1))


def _rms(x, g):
    return x * lax.rsqrt(jnp.mean(x * x, axis=-1, keepdims=True) + EPS) * g


def _rope_table_kernel(ang_ref, cos_ref, sin_ref):
    ang = ang_ref[...]
    lane = lax.broadcasted_iota(jnp.int32, ang.shape, 1) % HEAD_DIM
    sign = jnp.where(lane < ROPE_HALF, -1.0, 1.0)
    cos_ref[...] = jnp.cos(ang)
    sin_ref[...] = jnp.sin(ang) * sign


def _rope_tables(positions):
    T = positions.size
    inv_freq = jnp.power(ROPE_THETA, -jnp.arange(0, ROPE_DIM, 2, dtype=F32) / ROPE_DIM)
    lane = jnp.arange(LANES) % HEAD_DIM
    freq_lane = jnp.where(lane < ROPE_DIM, inv_freq[lane % ROPE_HALF], 0.0)
    ang = positions.reshape(T, 1).astype(F32) * freq_lane[None, :]
    tm = min(1024, T)
    spec = pl.BlockSpec((tm, LANES), lambda i: (i, 0))
    return pl.pallas_call(
        _rope_table_kernel,
        out_shape=(jax.ShapeDtypeStruct((T, LANES), F32),) * 2,
        grid=(T // tm,), in_specs=[spec], out_specs=(spec, spec),
        compiler_params=_params("parallel"), name="rope_tables",
    )(ang)


def _rmsnorm_kernel(x_ref, g_ref, h_ref):
    h_ref[...] = _rms(x_ref[...], g_ref[...]).astype(h_ref.dtype)


def _rmsnorm(x2, g):
    T = x2.shape[0]
    tm = min(1024, T)
    return pl.pallas_call(
        _rmsnorm_kernel,
        out_shape=jax.ShapeDtypeStruct((T, D_MODEL), BF16),
        grid=(T // tm,),
        in_specs=[pl.BlockSpec((tm, D_MODEL), lambda i: (i, 0)), _const_spec((1, D_MODEL))],
        out_specs=pl.BlockSpec((tm, D_MODEL), lambda i: (i, 0)),
        compiler_params=_params("parallel"), name="rmsnorm",
    )(x2, g.reshape(1, D_MODEL))


ROW_GROUPS = (0, 1, 2, 4, 7)
FEAT_GROUPS = (5, 8, 3, 6)
ROW_ROPE_GROUP = 3
FEAT_ROPE_GROUP = 2
GROUP_BLOCKS = BRANCH_WIDTH // LANES


def _inproj_kernel(h_ref, w_ref, wft_ref, cos_ref, sin_ref, cos_t_ref, sin_t_ref, o_ref, ft_ref):
    h = h_ref[...]
    cosf = cos_ref[...]
    sinf = sin_ref[...]
    lane = lax.broadcasted_iota(jnp.int32, cosf.shape, 1) % HEAD_DIM
    first_half = lane < ROPE_HALF
    for c in range(len(ROW_GROUPS)):
        res = jnp.dot(h, w_ref[:, c * BRANCH_WIDTH:(c + 1) * BRANCH_WIDTH],
                      preferred_element_type=F32)
        for s in range(GROUP_BLOCKS):
            blk = res[:, s * LANES:(s + 1) * LANES]
            if c == ROW_ROPE_GROUP:
                partner = jnp.where(first_half,
                                    pltpu.roll(blk, LANES - ROPE_HALF, 1),
                                    pltpu.roll(blk, ROPE_HALF, 1))
                blk = blk * cosf + partner * sinf
            o_ref[c * GROUP_BLOCKS + s] = blk.astype(o_ref.dtype)
    for c in range(len(FEAT_GROUPS)):
        res_t = lax.dot_general(wft_ref[c * BRANCH_WIDTH:(c + 1) * BRANCH_WIDTH, :], h,
                                (((1,), (1,)), ((), ())), preferred_element_type=F32)
        for s in range(GROUP_BLOCKS):
            blk = res_t[s * LANES:(s + 1) * LANES, :]
            if c == FEAT_ROPE_GROUP:
                pieces = []
                for base in range(0, LANES, HEAD_DIM):
                    x1 = blk[base:base + ROPE_HALF]
                    x2 = blk[base + ROPE_HALF:base + ROPE_DIM]
                    rot = slice(base, base + ROPE_DIM)
                    pieces.append(jnp.concatenate([x1, x2], axis=0) * cos_t_ref[rot, :]
                                  + jnp.concatenate([x2, x1], axis=0) * sin_t_ref[rot, :])
                    pieces.append(blk[base + ROPE_DIM:base + HEAD_DIM])
                blk = jnp.concatenate(pieces, axis=0)
            ft_ref[c * GROUP_BLOCKS + s] = blk.astype(ft_ref.dtype)


def _inproj(h, w_main, w_ft, cosf, sinf, cos_t, sin_t):
    T = h.shape[0]
    tm = min(TM_DENSE, T)
    n_row = len(ROW_GROUPS) * GROUP_BLOCKS
    n_ft = len(FEAT_GROUPS) * GROUP_BLOCKS
    tok = pl.BlockSpec((tm, LANES), lambda i: (i, 0))
    feat = pl.BlockSpec((LANES, tm), lambda i: (0, i))
    return pl.pallas_call(
        _inproj_kernel,
        out_shape=(jax.ShapeDtypeStruct((n_row, T, LANES), BF16),
                   jax.ShapeDtypeStruct((n_ft, LANES, T), BF16)),
        grid=(T // tm,),
        in_specs=[pl.BlockSpec((tm, D_MODEL), lambda i: (i, 0)),
                  _const_spec(w_main.shape), _const_spec(w_ft.shape), tok, tok, feat, feat],
        out_specs=(pl.BlockSpec((n_row, tm, LANES), lambda i: (0, i, 0)),
                   pl.BlockSpec((n_ft, LANES, tm), lambda i: (0, 0, i))),
        compiler_params=_params("parallel"), name="inproj",
    )(h, w_main, w_ft, cosf, sinf, cos_t, sin_t)


def _split3(x):
    hi = x.astype(BF16)
    r1 = x - hi.astype(F32)
    mid = r1.astype(BF16)
    lo = (r1 - mid.astype(F32)).astype(BF16)
    return hi, mid, lo


def _forget_kernel(h_ref, w_ref, b_ref, f_ref, frep_ref, carry_ref):
    @pl.when(pl.program_id(1) == 0)
    def _():
        carry_ref[...] = jnp.zeros_like(carry_ref)

    fl = jnp.dot(h_ref[0], w_ref[...], preferred_element_type=F32) + b_ref[...]
    ls = jnp.minimum(fl, 0.0) - jnp.log1p(jnp.exp(-jnp.abs(fl)))
    tm = ls.shape[0]
    row = lax.broadcasted_iota(jnp.int32, (tm, tm), 0)
    col = lax.broadcasted_iota(jnp.int32, (tm, tm), 1)
    tri = (col <= row).astype(BF16)
    cs = sum(jnp.dot(tri, part, preferred_element_type=F32) for part in _split3(ls))
    f = cs + carry_ref[...]
    carry_ref[...] = f[tm - 1:tm, :]
    f2 = f * LOG2E
    f_ref[0] = f2
    parts = jnp.concatenate(_split3(f2), axis=1)
    src = lax.broadcasted_iota(jnp.int32, (3 * LANES, C_HEADS * LANES), 0) % LANES
    dst = lax.broadcasted_iota(jnp.int32, (3 * LANES, C_HEADS * LANES), 1) // LANES
    spread = (src == dst).astype(BF16)
    rep = jnp.dot(parts, spread, preferred_element_type=F32)
    for hd in range(C_HEADS):
        frep_ref[0, hd] = rep[:, hd * LANES:(hd + 1) * LANES]


def _forget_cumsum(h3, w_fl, b_fl):
    B, S, _ = h3.shape
    tm = min(TM_DENSE, S)
    return pl.pallas_call(
        _forget_kernel,
        out_shape=(jax.ShapeDtypeStruct((B, S, LANES), F32),
                   jax.ShapeDtypeStruct((B, C_HEADS, S, LANES), F32)),
        grid=(B, S // tm),
        in_specs=[pl.BlockSpec((1, tm, D_MODEL), lambda b, i: (b, i, 0)),
                  _const_spec(w_fl.shape), _const_spec(b_fl.shape)],
        out_specs=(pl.BlockSpec((1, tm, LANES), lambda b, i: (b, i, 0)),
                   pl.BlockSpec((1, C_HEADS, tm, LANES), lambda b, i: (b, 0, i, 0))),
        scratch_shapes=[pltpu.VMEM((1, LANES), F32)],
        compiler_params=_params("parallel", "arbitrary"), name="forget_cumsum",
    )(h3, w_fl, b_fl)


def _band_kernel(q_ref, k_ref, v_ref, bias_ref, o_ref):
    qi = pl.program_id(2)
    tq = q_ref.shape[2]
    lane = lax.broadcasted_iota(jnp.int32, (1, LANES), 1)
    pad = WIN_BAND - SUB_BAND

    def sub_block(r, k, v, col0):
        w = k.shape[0]
        q = q_ref[0, 0, r * SUB_BAND:(r + 1) * SUB_BAND, :]
        outs = []
        for hh in range(2):
            qh = jnp.where((lane >= HEAD_DIM) == bool(hh), q, jnp.zeros_like(q))
            s = lax.dot_general(qh, k, (((1,), (1,)), ((), ())), preferred_element_type=F32)
            s = s + bias_ref[hh, :, col0:col0 + w]
            m = jnp.max(s, axis=-1, keepdims=True)
            p = jnp.exp2(s - m)
            l = jnp.sum(p, axis=-1, keepdims=True)
            outs.append(jnp.dot(p.astype(BF16), v, preferred_element_type=F32) / l)
        o = jnp.where(lane < HEAD_DIM, outs[0], outs[1])
        o_ref[0, 0, r * SUB_BAND:(r + 1) * SUB_BAND, :] = o.astype(o_ref.dtype)

    @pl.when(qi == 0)
    def _():
        for r in range(tq // SUB_BAND):
            lo = max(r * SUB_BAND - pad, 0)
            hi = (r + 1) * SUB_BAND
            sub_block(r, k_ref[0, 0, lo:hi, :], v_ref[0, 0, lo:hi, :], WIN_BAND - (hi - lo))

    @pl.when(qi > 0)
    def _():
        for r in range(tq // SUB_BAND):
            start = pl.multiple_of(qi * tq + (r * SUB_BAND - pad), SUB_BAND)
            sub_block(r, k_ref[0, 0, pl.ds(start, WIN_BAND), :],
                      v_ref[0, 0, pl.ds(start, WIN_BAND), :], 0)


def _band_attention(proj, bias, B, S):
    tq = min(TQ_BAND, S)
    nhp = A_HEADS // 2
    return pl.pallas_call(
        _band_kernel,
        out_shape=jax.ShapeDtypeStruct((nhp, B, S, LANES), BF16),
        grid=(B, nhp, S // tq),
        in_specs=[pl.BlockSpec((1, 1, tq, LANES), lambda b, p, i: (p, b, i, 0)),
                  pl.BlockSpec((1, 1, S, LANES), lambda b, p, i: (nhp + p, b, 0, 0)),
                  pl.BlockSpec((1, 1, S, LANES), lambda b, p, i: (2 * nhp + p, b, 0, 0)),
                  pl.BlockSpec((2, SUB_BAND, WIN_BAND), lambda b, p, i: (p, 0, 0))],
        out_specs=pl.BlockSpec((1, 1, tq, LANES), lambda b, p, i: (p, b, i, 0)),
        compiler_params=_params("parallel", "parallel", "arbitrary"), name="band_attention",
    )(proj, proj, proj, bias)


def _band_bias(rel_bias):
    period = 768
    assert period > WIN_BAND + SUB_BAND - 1
    u = np.arange(period)
    diag = np.where(u < WIN_BAND, u, u - period)
    rel = np.clip(diag - LEFT_CHUNKS * CHUNK, -REL_CLIP, REL_CLIP) + REL_CLIP
    profile = rel_bias.astype(F32)[:, rel] * LOG2E
    heads = profile.shape[0]
    table = jnp.tile(profile, (1, SUB_BAND))[:, :SUB_BAND * (period - 1)]
    table = table.reshape(heads, SUB_BAND, period - 1)[:, :, :WIN_BAND]
    i = np.arange(SUB_BAND)
    k_off = np.arange(WIN_BAND)[None, :] - LEFT_CHUNKS * CHUNK - (i // CHUNK)[:, None] * CHUNK
    valid = (k_off >= -LEFT_CHUNKS * CHUNK) & (k_off < CHUNK)
    return jnp.where(jnp.asarray(valid)[None], table, NEG_BIG)


ROW_T, ROW_ALPHA, ROW_MPREV = 0, 1, 2
FLASH_UNROLL = 4
SUM_ROWS = 16
assert FLASH_UNROLL % 2 == 0


def _flash_body(qt_ref, k_ref, vt_ref, fq_ref, frep_ref, s_refs, st_refs, mrun_ref, l_ref, acc_ref, *,
                chunk_causal, sum_on_mxu):
    qi = pl.program_id(2)
    tq = qt_ref.shape[2]
    tk = tq
    feat = lax.broadcasted_iota(jnp.int32, (LANES, 1), 0)
    own = [feat < HEAD_DIM, feat >= HEAD_DIM]
    qt = qt_ref[0]
    qs = [jnp.where(own[c], qt, jnp.zeros_like(qt)) for c in range(2)]
    mrun_ref[...] = jnp.full(mrun_ref.shape, NEG_BIG, F32)
    l_ref[...] = jnp.zeros(l_ref.shape, F32)
    acc_ref[...] = jnp.zeros(acc_ref.shape, F32)
    q_off = pl.multiple_of(qi * tq, tq)
    shifts = [None if fq_ref is None else fq_ref[0, 0, c:c + 1, pl.ds(q_off, tq)] for c in range(2)]

    def stats(c, col_max, m_prev):
        if shifts[c] is not None:
            col_max = col_max + shifts[c]
        m_new = jnp.maximum(m_prev, col_max)
        alpha = jnp.exp2(m_prev - m_new)
        t = m_new if shifts[c] is None else m_new - shifts[c]
        return m_new, alpha, t

    def scores(j, slot):
        off = pl.multiple_of(j * tk, tk)
        k = k_ref[0, 0, pl.ds(off, tk), :]
        for c in range(2):
            s = jnp.dot(k, qs[c], preferred_element_type=F32)
            if frep_ref is not None:
                s = s - jnp.concatenate([frep_ref[0, c, pl.ds(off, tk), :]] * (tq // LANES), axis=1)
            s_refs[slot][c] = s
            m_prev = mrun_ref[c]
            m_new, alpha, t = stats(c, jnp.max(s, axis=0, keepdims=True), m_prev)
            mrun_ref[c] = m_new
            st_refs[slot][c, ROW_T:ROW_T + 1] = t
            st_refs[slot][c, ROW_ALPHA:ROW_ALPHA + 1] = alpha
            st_refs[slot][c, ROW_MPREV:ROW_MPREV + 1] = m_prev

    def update(j, slot, mask):
        off = pl.multiple_of(j * tk, tk)
        vt = vt_ref[0, :, pl.ds(off, tk)]
        for c in range(2):
            s = s_refs[slot][c]
            if mask is None:
                t = st_refs[slot][c, ROW_T:ROW_T + 1]
                alpha = st_refs[slot][c, ROW_ALPHA:ROW_ALPHA + 1]
            else:
                s = jnp.where(mask, s, NEG_BIG)
                _, alpha, t = stats(c, jnp.max(s, axis=0, keepdims=True),
                                    st_refs[slot][c, ROW_MPREV:ROW_MPREV + 1])
            p = jnp.exp2(s - t)
            if sum_on_mxu:
                vt_c = jnp.concatenate([vt[c * HEAD_DIM:(c + 1) * HEAD_DIM],
                                        jnp.ones((SUM_ROWS, tk), vt.dtype)], axis=0)
            else:
                vt_c = vt
                l_ref[c] = alpha * l_ref[c] + jnp.sum(p, axis=0, keepdims=True)
            acc_ref[c] = alpha * acc_ref[c] + jnp.dot(vt_c, p.astype(BF16), preferred_element_type=F32)

    key = lax.broadcasted_iota(jnp.int32, (tk, tq), 0)
    qry = lax.broadcasted_iota(jnp.int32, (tk, tq), 1)
    if chunk_causal:
        mask = (key // CHUNK) <= (qry // CHUNK)
    else:
        mask = key <= qry

    def run(j0, n_full, then_diagonal):
        for u in range(n_full):
            scores(j0 + u + 1, (u + 1) % 2)
            update(j0 + u, u % 2, None)
        if then_diagonal:
            update(j0 + n_full, n_full % 2, mask)

    scores(0, 0)

    def group(jj, carry):
        run(jj * FLASH_UNROLL, FLASH_UNROLL, False)
        return carry

    lax.fori_loop(0, qi // FLASH_UNROLL, group, 0)
    for rem in range(FLASH_UNROLL):
        @pl.when(qi % FLASH_UNROLL == rem)
        def _():
            run(qi - rem, rem, True)


def _diff_kernel(qt_ref, k_ref, vt_ref, lam_ref, gain_ref, o_ref, s0_ref, s1_ref, st0_ref, st1_ref,
                 mrun_ref, l_ref, acc_ref, *, lam_init):
    _flash_body(qt_ref, k_ref, vt_ref, None, None, (s0_ref, s1_ref), (st0_ref, st1_ref), mrun_ref, l_ref, acc_ref,
                chunk_causal=True, sum_on_mxu=False)
    lf = lam_ref[...]
    lam = (jnp.exp(jnp.sum(lf[0:1] * lf[1:2], axis=-1, keepdims=True))
           - jnp.exp(jnp.sum(lf[2:3] * lf[3:4], axis=-1, keepdims=True)) + lam_init)
    o_t = acc_ref[0] / l_ref[0] - lam * (acc_ref[1] / l_ref[1])
    o_ref[0, 0] = (_rms(o_t.T, gain_ref[...]) * (1.0 - lam_init)).astype(o_ref.dtype)


def _forget_attn_kernel(qt_ref, k_ref, vt_ref, fq_ref, frep_ref, o_ref, s0_ref, s1_ref, st0_ref, st1_ref,
                        mrun_ref, l_ref, acc_ref):
    _flash_body(qt_ref, k_ref, vt_ref, fq_ref, frep_ref, (s0_ref, s1_ref), (st0_ref, st1_ref), mrun_ref, l_ref,
                acc_ref, chunk_causal=False, sum_on_mxu=True)
    heads = [acc_ref[c, :HEAD_DIM] / acc_ref[c, HEAD_DIM:HEAD_DIM + 1] for c in range(2)]
    o_t = jnp.concatenate(heads, axis=0)
    o_ref[0, 0] = o_t.T.astype(o_ref.dtype)


def _flash_scratch(tq, acc_rows):
    return [pltpu.VMEM((2, tq, tq), F32), pltpu.VMEM((2, tq, tq), F32),
            pltpu.VMEM((2, 3, tq), F32), pltpu.VMEM((2, 3, tq), F32),
            pltpu.VMEM((2, 1, tq), F32), pltpu.VMEM((2, 1, tq), F32), pltpu.VMEM((2, acc_rows, tq), F32)]


def _flash_specs(qt_block, k_block, vt_block, tq, S):
    nq = S // tq
    return [pl.BlockSpec((1, LANES, tq), lambda b, p, i: (qt_block + p, 0, b * nq + i)),
            pl.BlockSpec((1, 1, S, LANES), lambda b, p, i: (k_block + p, b, 0, 0)),
            pl.BlockSpec((1, LANES, S), lambda b, p, i: (vt_block + p, 0, b))]


def _diff_attention(proj, ft, lambdas, sub_gain, lam_init, B, S):
    tq = min(TQ_FLASH, S)
    return pl.pallas_call(
        functools.partial(_diff_kernel, lam_init=lam_init),
        out_shape=jax.ShapeDtypeStruct((GROUP_BLOCKS, B, S, LANES), BF16),
        grid=(B, GROUP_BLOCKS, S // tq),
        in_specs=_flash_specs(2 * GROUP_BLOCKS, 3 * GROUP_BLOCKS, 0, tq, S) + [
            _const_spec(lambdas.shape), _const_spec((1, LANES))],
        out_specs=pl.BlockSpec((1, 1, tq, LANES), lambda b, p, i: (p, b, i, 0)),
        scratch_shapes=_flash_scratch(tq, LANES),
        compiler_params=_params("parallel", "parallel", "arbitrary"), name="diff_attention",
    )(ft, proj, ft, lambdas.astype(F32), sub_gain.astype(F32).reshape(1, LANES))


def _forget_attention(proj, ft, f_row, f_rep, B, S):
    tq = min(TQ_FLASH, S)
    return pl.pallas_call(
        _forget_attn_kernel,
        out_shape=jax.ShapeDtypeStruct((GROUP_BLOCKS, B, S, LANES), BF16),
        grid=(B, GROUP_BLOCKS, S // tq),
        in_specs=_flash_specs(3 * GROUP_BLOCKS, 4 * GROUP_BLOCKS, GROUP_BLOCKS, tq, S) + [
            pl.BlockSpec((1, 1, 2, S), lambda b, p, i: (b, p, 0, 0)),
            pl.BlockSpec((1, 2, S, LANES), lambda b, p, i: (b, p, 0, 0), pipeline_mode=pl.Buffered(1))],
        out_specs=pl.BlockSpec((1, 1, tq, LANES), lambda b, p, i: (p, b, i, 0)),
        scratch_shapes=_flash_scratch(tq, HEAD_DIM + SUM_ROWS),
        compiler_params=_params("parallel", "parallel", "arbitrary"), name="forget_attention",
    )(ft, proj, ft, f_row, f_rep)


def _merge_kernel(x_ref, h_ref, ya_ref, yb_ref, yc_ref, wg_ref, bg_ref, wbr_ref, wo_ref, o_ref):
    h = h_ref[...]
    merged = None
    for b, y_ref in enumerate((ya_ref, yb_ref, yc_ref)):
        y = jnp.concatenate([y_ref[i] for i in range(y_ref.shape[0])], axis=1)
        t = jnp.dot(y, wbr_ref[b], preferred_element_type=F32)
        z = jnp.dot(h, wg_ref[:, b * D_MODEL:(b + 1) * D_MODEL], preferred_element_type=F32)
        g = jax.nn.sigmoid(z + bg_ref[:, b * D_MODEL:(b + 1) * D_MODEL])
        merged = g * t if merged is None else merged + g * t
    o_ref[...] = x_ref[...] + jnp.dot(merged.astype(BF16), wo_ref[...], preferred_element_type=F32)


def _merge(x2, h, ya, yb, yc, w_gate, b_gate, w_br, w_out):
    T = x2.shape[0]
    tm = min(TM_DENSE, T)
    nblk = ya.shape[0]
    row = pl.BlockSpec((tm, D_MODEL), lambda i: (i, 0))
    yspec = pl.BlockSpec((nblk, tm, LANES), lambda i: (0, i, 0))
    return pl.pallas_call(
        _merge_kernel,
        out_shape=jax.ShapeDtypeStruct((T, D_MODEL), F32),
        grid=(T // tm,),
        in_specs=[row, row, yspec, yspec, yspec, _const_spec(w_gate.shape), _const_spec(b_gate.shape),
                  _const_spec(w_br.shape), _const_spec(w_out.shape)],
        out_specs=row,
        compiler_params=_params("parallel"), name="merge_out",
    )(x2, h, ya, yb, yc, w_gate, b_gate, w_br, w_out)


FF_CHUNK = 1024


def _mlp_kernel(x_ref, gn_ref, w1_ref, w2_ref, gnext_ref, *out_refs):
    x = x_ref[...]
    h2 = _rms(x, gn_ref[...]).astype(BF16)
    acc = x
    for c in range(D_FF // FF_CHUNK):
        a = jnp.dot(h2, w1_ref[:, c * FF_CHUNK:(c + 1) * FF_CHUNK], preferred_element_type=F32)
        a = jnp.square(jnp.maximum(a, 0.0))
        acc = acc + jnp.dot(a.astype(BF16), w2_ref[c * FF_CHUNK:(c + 1) * FF_CHUNK, :],
                            preferred_element_type=F32)
    normed = _rms(acc, gnext_ref[...])
    if len(out_refs) == 2:
        out_refs[0][...] = acc
        out_refs[1][...] = normed.astype(out_refs[1].dtype)
    else:
        out_refs[0][...] = normed.astype(out_refs[0].dtype)


def _mlp(x2, g_mlp, w1, w2, g_next, last):
    T = x2.shape[0]
    tm = min(TM_DENSE, T)
    row = pl.BlockSpec((tm, D_MODEL), lambda i: (i, 0))
    if last:
        out_shape, out_specs = jax.ShapeDtypeStruct((T, D_MODEL), F32), row
    else:
        out_shape = (jax.ShapeDtypeStruct((T, D_MODEL), F32), jax.ShapeDtypeStruct((T, D_MODEL), BF16))
        out_specs = (row, row)
    return pl.pallas_call(
        _mlp_kernel, out_shape=out_shape, grid=(T // tm,),
        in_specs=[row, _const_spec((1, D_MODEL)), _const_spec(w1.shape), _const_spec(w2.shape),
                  _const_spec((1, D_MODEL))],
        out_specs=out_specs,
        compiler_params=_params("parallel"), name="mlp",
    )(x2, g_mlp.reshape(1, D_MODEL), w1, w2, g_next.reshape(1, D_MODEL))


def kernel(x, positions, norm_mix, w_in, rel_bias, lambdas, sub_gain, b_forget, w_br_a, w_br_b, w_br_c,
           w_gate, b_gate, w_out, norm_mlp, w_ff1, w_ff2, final_norm):
    B, S, D = x.shape
    T = B * S
    depth = w_in.shape[0]
    n_main = 9 * BRANCH_WIDTH
    q_scale = HEAD_DIM ** -0.5 * LOG2E
    col_scale = jnp.ones((n_main,), F32)
    for grp in (0, 3, 6):
        col_scale = col_scale.at[grp * BRANCH_WIDTH:(grp + 1) * BRANCH_WIDTH].set(q_scale)

    cosf, sinf = _rope_tables(positions)
    cos_t, sin_t = cosf.T, sinf.T
    x2 = x.reshape(T, D)
    h = _rmsnorm(x2, norm_mix[0])
    out = None
    for l in range(depth):
        lam_init = 0.8 - 0.6 * math.exp(-0.3 * l)
        w_all = w_in[l, :, :n_main] * col_scale[None, :]
        group = lambda g: w_all[:, g * BRANCH_WIDTH:(g + 1) * BRANCH_WIDTH]
        w_main = jnp.concatenate([group(g) for g in ROW_GROUPS], axis=1).astype(BF16)
        w_ft = jnp.concatenate([group(g) for g in FEAT_GROUPS], axis=1).T.astype(BF16)
        w_fl = jnp.zeros((D, LANES), F32).at[:, :C_HEADS].set(w_in[l, :, n_main:]).astype(BF16)
        b_fl = jnp.zeros((1, LANES), F32).at[0, :C_HEADS].set(b_forget[l].astype(F32))

        proj, ft = _inproj(h, w_main, w_ft, cosf, sinf, cos_t, sin_t)
        proj = proj.reshape(len(ROW_GROUPS) * GROUP_BLOCKS, B, S, LANES)
        f_tok, f_rep = _forget_cumsum(h.reshape(B, S, D), w_fl, b_fl)
        f_row = f_tok[:, :, :C_HEADS].transpose(0, 2, 1).reshape(B, C_HEADS // 2, 2, S)

        ya = _band_attention(proj, _band_bias(rel_bias[l]), B, S)
        yb = _diff_attention(proj, ft, lambdas[l], sub_gain[l], lam_init, B, S)
        yc = _forget_attention(proj, ft, f_row, f_rep, B, S)

        w_br = jnp.stack([w_br_a[l], w_br_b[l], w_br_c[l]]).astype(BF16)
        x2 = _merge(x2, h, ya.reshape(GROUP_BLOCKS, T, LANES), yb.reshape(GROUP_BLOCKS, T, LANES),
                    yc.reshape(GROUP_BLOCKS, T, LANES), w_gate[l].astype(BF16),
                    b_gate[l].astype(F32).reshape(1, -1), w_br, w_out[l].astype(BF16))
        last = l == depth - 1
        g_next = final_norm if last else norm_mix[l + 1]
        res = _mlp(x2, norm_mlp[l], w_ff1[l].astype(BF16), w_ff2[l].astype(BF16), g_next, last)
        if last:
            out = res
        else:
            x2, h = res
    return out.reshape(B, S, D)
```

```python
import functools
import math

import jax
import jax.numpy as jnp
import numpy as np
from jax import lax
from jax.experimental import pallas as pl
from jax.experimental.pallas import tpu as pltpu

D_MODEL = 1024
CHUNK = 64
LEFT_CHUNKS = 8
HEAD_DIM = 64
A_HEADS = 8
B_HEADS = 4
C_HEADS = 8
BRANCH_WIDTH = 512
N_BRANCH = 3
D_FF = 4 * D_MODEL
REL_CLIP = 128
ROPE_THETA = 500000.0
ROPE_DIM = HEAD_DIM // 4
ROPE_HALF = ROPE_DIM // 2
EPS = 1e-6

LANES = 128
LOG2E = 1.4426950408889634
NEG_BIG = -1e30

TM_DENSE = 512
TQ_FLASH = 512
TQ_BAND = 1024
SUB_BAND = 4 * CHUNK
WIN_BAND = SUB_BAND + LEFT_CHUNKS * CHUNK
VMEM_LIMIT = 56 * 1024 * 1024

BF16 = jnp.bfloat16
F32 = jnp.float32


def _params(*sem):
    return pltpu.CompilerParams(dimension_semantics=sem, vmem_limit_bytes=VMEM_LIMIT)


def _const_spec(shape):
    nd = len(shape)
    return pl.BlockSpec(shape, lambda *_: (0,) * nd, pipeline_mode=pl.Buffered(1))


def _rms(x, g):
    return x * lax.rsqrt(jnp.mean(x * x, axis=-1, keepdims=True) + EPS) * g


def _rope_table_kernel(ang_ref, cos_ref, sin_ref):
    ang = ang_ref[...]
    lane = lax.broadcasted_iota(jnp.int32, ang.shape, 1) % HEAD_DIM
    sign = jnp.where(lane < ROPE_HALF, -1.0, 1.0)
    cos_ref[...] = jnp.cos(ang)
    sin_ref[...] = jnp.sin(ang) * sign


def _rope_tables(positions):
    T = positions.size
    inv_freq = jnp.power(ROPE_THETA, -jnp.arange(0, ROPE_DIM, 2, dtype=F32) / ROPE_DIM)
    lane = jnp.arange(LANES) % HEAD_DIM
    freq_lane = jnp.where(lane < ROPE_DIM, inv_freq[lane % ROPE_HALF], 0.0)
    ang = positions.reshape(T, 1).astype(F32) * freq_lane[None, :]
    tm = min(1024, T)
    spec = pl.BlockSpec((tm, LANES), lambda i: (i, 0))
    return pl.pallas_call(
        _rope_table_kernel,
        out_shape=(jax.ShapeDtypeStruct((T, LANES), F32),) * 2,
        grid=(T // tm,), in_specs=[spec], out_specs=(spec, spec),
        compiler_params=_params("parallel"), name="rope_tables",
    )(ang)


def _rmsnorm_kernel(x_ref, g_ref, h_ref):
    h_ref[...] = _rms(x_ref[...], g_ref[...]).astype(h_ref.dtype)


def _rmsnorm(x2, g):
    T = x2.shape[0]
    tm = min(1024, T)
    return pl.pallas_call(
        _rmsnorm_kernel,
        out_shape=jax.ShapeDtypeStruct((T, D_MODEL), BF16),
        grid=(T // tm,),
        in_specs=[pl.BlockSpec((tm, D_MODEL), lambda i: (i, 0)), _const_spec((1, D_MODEL))],
        out_specs=pl.BlockSpec((tm, D_MODEL), lambda i: (i, 0)),
        compiler_params=_params("parallel"), name="rmsnorm",
    )(x2, g.reshape(1, D_MODEL))


ROW_GROUPS = (1, 4, 7)
FEAT_GROUPS = (5, 8, 3, 6, 0, 2)
ROW_ROPE_GROUP = 1
FEAT_ROPE_GROUP = 2
GROUP_BLOCKS = BRANCH_WIDTH // LANES


def _inproj_kernel(h_ref, w_ref, wft_ref, cos_ref, sin_ref, cos_t_ref, sin_t_ref, o_ref, ft_ref):
    h = h_ref[...]
    cosf = cos_ref[...]
    sinf = sin_ref[...]
    lane = lax.broadcasted_iota(jnp.int32, cosf.shape, 1) % HEAD_DIM
    first_half = lane < ROPE_HALF
    for c in range(len(ROW_GROUPS)):
        res = jnp.dot(h, w_ref[:, c * BRANCH_WIDTH:(c + 1) * BRANCH_WIDTH],
                      preferred_element_type=F32)
        for s in range(GROUP_BLOCKS):
            blk = res[:, s * LANES:(s + 1) * LANES]
            if c == ROW_ROPE_GROUP:
                partner = jnp.where(first_half,
                                    pltpu.roll(blk, LANES - ROPE_HALF, 1),
                                    pltpu.roll(blk, ROPE_HALF, 1))
                blk = blk * cosf + partner * sinf
            o_ref[c * GROUP_BLOCKS + s] = blk.astype(o_ref.dtype)
    for c in range(len(FEAT_GROUPS)):
        res_t = lax.dot_general(wft_ref[c * BRANCH_WIDTH:(c + 1) * BRANCH_WIDTH, :], h,
                                (((1,), (1,)), ((), ())), preferred_element_type=F32)
        for s in range(GROUP_BLOCKS):
            blk = res_t[s * LANES:(s + 1) * LANES, :]
            if c == FEAT_ROPE_GROUP:
                pieces = []
                for base in range(0, LANES, HEAD_DIM):
                    x1 = blk[base:base + ROPE_HALF]
                    x2 = blk[base + ROPE_HALF:base + ROPE_DIM]
                    rot = slice(base, base + ROPE_DIM)
                    pieces.append(jnp.concatenate([x1, x2], axis=0) * cos_t_ref[rot, :]
                                  + jnp.concatenate([x2, x1], axis=0) * sin_t_ref[rot, :])
                    pieces.append(blk[base + ROPE_DIM:base + HEAD_DIM])
                blk = jnp.concatenate(pieces, axis=0)
            ft_ref[c * GROUP_BLOCKS + s] = blk.astype(ft_ref.dtype)


def _inproj(h, w_main, w_ft, cosf, sinf, cos_t, sin_t):
    T = h.shape[0]
    tm = min(TM_DENSE, T)
    n_row = len(ROW_GROUPS) * GROUP_BLOCKS
    n_ft = len(FEAT_GROUPS) * GROUP_BLOCKS
    tok = pl.BlockSpec((tm, LANES), lambda i: (i, 0))
    feat = pl.BlockSpec((LANES, tm), lambda i: (0, i))
    return pl.pallas_call(
        _inproj_kernel,
        out_shape=(jax.ShapeDtypeStruct((n_row, T, LANES), BF16),
                   jax.ShapeDtypeStruct((n_ft, LANES, T), BF16)),
        grid=(T // tm,),
        in_specs=[pl.BlockSpec((tm, D_MODEL), lambda i: (i, 0)),
                  _const_spec(w_main.shape), _const_spec(w_ft.shape), tok, tok, feat, feat],
        out_specs=(pl.BlockSpec((n_row, tm, LANES), lambda i: (0, i, 0)),
                   pl.BlockSpec((n_ft, LANES, tm), lambda i: (0, 0, i))),
        compiler_params=_params("parallel"), name="inproj",
    )(h, w_main, w_ft, cosf, sinf, cos_t, sin_t)


def _split3(x):
    hi = x.astype(BF16)
    r1 = x - hi.astype(F32)
    mid = r1.astype(BF16)
    lo = (r1 - mid.astype(F32)).astype(BF16)
    return hi, mid, lo


def _forget_kernel(h_ref, w_ref, b_ref, f_ref, frep_ref, carry_ref):
    @pl.when(pl.program_id(1) == 0)
    def _():
        carry_ref[...] = jnp.zeros_like(carry_ref)

    fl = jnp.dot(h_ref[0], w_ref[...], preferred_element_type=F32) + b_ref[...]
    ls = jnp.minimum(fl, 0.0) - jnp.log1p(jnp.exp(-jnp.abs(fl)))
    tm = ls.shape[0]
    row = lax.broadcasted_iota(jnp.int32, (tm, tm), 0)
    col = lax.broadcasted_iota(jnp.int32, (tm, tm), 1)
    tri = (col <= row).astype(BF16)
    cs = sum(jnp.dot(tri, part, preferred_element_type=F32) for part in _split3(ls))
    f = cs + carry_ref[...]
    carry_ref[...] = f[tm - 1:tm, :]
    f2 = f * LOG2E
    f_ref[0] = f2
    parts = jnp.concatenate(_split3(f2), axis=1)
    src = lax.broadcasted_iota(jnp.int32, (3 * LANES, C_HEADS * LANES), 0) % LANES
    dst = lax.broadcasted_iota(jnp.int32, (3 * LANES, C_HEADS * LANES), 1) // LANES
    spread = (src == dst).astype(BF16)
    rep = jnp.dot(parts, spread, preferred_element_type=F32)
    for hd in range(C_HEADS):
        frep_ref[0, hd] = rep[:, hd * LANES:(hd + 1) * LANES]


def _forget_cumsum(h3, w_fl, b_fl):
    B, S, _ = h3.shape
    tm = min(TM_DENSE, S)
    return pl.pallas_call(
        _forget_kernel,
        out_shape=(jax.ShapeDtypeStruct((B, S, LANES), F32),
                   jax.ShapeDtypeStruct((B, C_HEADS, S, LANES), F32)),
        grid=(B, S // tm),
        in_specs=[pl.BlockSpec((1, tm, D_MODEL), lambda b, i: (b, i, 0)),
                  _const_spec(w_fl.shape), _const_spec(b_fl.shape)],
        out_specs=(pl.BlockSpec((1, tm, LANES), lambda b, i: (b, i, 0)),
                   pl.BlockSpec((1, C_HEADS, tm, LANES), lambda b, i: (b, 0, i, 0))),
        scratch_shapes=[pltpu.VMEM((1, LANES), F32)],
        compiler_params=_params("parallel", "arbitrary"), name="forget_cumsum",
    )(h3, w_fl, b_fl)


def _band_kernel(qt_ref, k_ref, vt_ref, bias_ref, o_ref):
    qi = pl.program_id(2)
    tq = qt_ref.shape[2]
    feat = lax.broadcasted_iota(jnp.int32, (LANES, 1), 0)
    own = [feat < HEAD_DIM, feat >= HEAD_DIM]
    pad = WIN_BAND - SUB_BAND

    def window(r):
        start = qi * tq + (r * SUB_BAND - pad)
        clipped = pl.multiple_of(jnp.maximum(start, 0), SUB_BAND)
        return clipped, pl.multiple_of(clipped - start, SUB_BAND)

    def scores(r, c):
        start, row0 = window(r)
        qt = qt_ref[0, :, r * SUB_BAND:(r + 1) * SUB_BAND]
        qc = jnp.where(own[c], qt, jnp.zeros_like(qt))
        s = jnp.dot(k_ref[0, 0, pl.ds(start, WIN_BAND), :], qc, preferred_element_type=F32)
        s = s + bias_ref[c, pl.ds(row0, WIN_BAND), :]
        return s, jnp.max(s, axis=0, keepdims=True)

    def head_out(r, c, s, m):
        start, _ = window(r)
        p = jnp.exp2(s - m)
        vt_c = jnp.concatenate([vt_ref[0, c * HEAD_DIM:(c + 1) * HEAD_DIM, pl.ds(start, WIN_BAND)],
                                jnp.ones((SUM_ROWS, WIN_BAND), vt_ref.dtype)], axis=0)
        acc = jnp.dot(vt_c, p.astype(BF16), preferred_element_type=F32)
        return acc[:HEAD_DIM] / acc[HEAD_DIM:HEAD_DIM + 1]

    units = [(r, c) for r in range(tq // SUB_BAND) for c in range(2)]
    heads = {}
    pending = None
    for unit in units + [None]:
        current = None if unit is None else (unit, scores(*unit))
        if pending is not None:
            (r, c), (s, m) = pending
            heads[c] = head_out(r, c, s, m)
            if c == 1:
                o_t = jnp.concatenate([heads[0], heads[1]], axis=0)
                o_ref[0, 0, r * SUB_BAND:(r + 1) * SUB_BAND, :] = o_t.T.astype(o_ref.dtype)
        pending = current


def _band_attention(proj, ft, bias, B, S):
    tq = min(TQ_BAND, S)
    nq = S // tq
    return pl.pallas_call(
        _band_kernel,
        out_shape=jax.ShapeDtypeStruct((GROUP_BLOCKS, B, S, LANES), BF16),
        grid=(B, GROUP_BLOCKS, nq),
        in_specs=[pl.BlockSpec((1, LANES, tq), lambda b, p, i: (4 * GROUP_BLOCKS + p, 0, b * nq + i)),
                  pl.BlockSpec((1, 1, S, LANES), lambda b, p, i: (p, b, 0, 0)),
                  pl.BlockSpec((1, LANES, S), lambda b, p, i: (5 * GROUP_BLOCKS + p, 0, b)),
                  pl.BlockSpec((2,) + bias.shape[1:], lambda b, p, i: (p, 0, 0))],
        out_specs=pl.BlockSpec((1, 1, tq, LANES), lambda b, p, i: (p, b, i, 0)),
        compiler_params=_params("parallel", "parallel", "arbitrary"), name="band_attention",
    )(ft, proj, ft, bias)


def _band_bias(rel_bias):
    period = 2 * WIN_BAND
    assert period > WIN_BAND + SUB_BAND - 1
    u = np.arange(period)
    diag = np.where(u < WIN_BAND, u, u - period)
    rel = np.clip(diag - LEFT_CHUNKS * CHUNK, -REL_CLIP, REL_CLIP) + REL_CLIP
    profile = rel_bias.astype(F32)[:, rel] * LOG2E
    heads = profile.shape[0]
    table = jnp.tile(profile, (1, SUB_BAND))[:, :SUB_BAND * (period - 1)]
    table = table.reshape(heads, SUB_BAND, period - 1)[:, :, :WIN_BAND]
    i = np.arange(SUB_BAND)
    k_off = np.arange(WIN_BAND)[None, :] - LEFT_CHUNKS * CHUNK - (i // CHUNK)[:, None] * CHUNK
    valid = (k_off >= -LEFT_CHUNKS * CHUNK) & (k_off < CHUNK)
    table = jnp.where(jnp.asarray(valid)[None], table, NEG_BIG).transpose(0, 2, 1)
    tail = jnp.full((heads, WIN_BAND - SUB_BAND, SUB_BAND), NEG_BIG, F32)
    return jnp.concatenate([table, tail], axis=1)


ROW_T, ROW_ALPHA, ROW_MPREV = 0, 1, 2
FLASH_UNROLL = 4
SUM_ROWS = 16
assert FLASH_UNROLL % 2 == 0


def _flash_body(qt_ref, k_ref, vt_ref, fq_ref, frep_ref, s_refs, st_refs, mrun_ref, l_ref, acc_ref, *,
                chunk_causal, sum_on_mxu):
    qi = pl.program_id(2)
    tq = qt_ref.shape[2]
    tk = tq
    feat = lax.broadcasted_iota(jnp.int32, (LANES, 1), 0)
    own = [feat < HEAD_DIM, feat >= HEAD_DIM]
    qt = qt_ref[0]
    qs = [jnp.where(own[c], qt, jnp.zeros_like(qt)) for c in range(2)]
    mrun_ref[...] = jnp.full(mrun_ref.shape, NEG_BIG, F32)
    l_ref[...] = jnp.zeros(l_ref.shape, F32)
    acc_ref[...] = jnp.zeros(acc_ref.shape, F32)
    q_off = pl.multiple_of(qi * tq, tq)
    shifts = [None if fq_ref is None else fq_ref[0, 0, c:c + 1, pl.ds(q_off, tq)] for c in range(2)]

    def stats(c, col_max, m_prev):
        if shifts[c] is not None:
            col_max = col_max + shifts[c]
        m_new = jnp.maximum(m_prev, col_max)
        alpha = jnp.exp2(m_prev - m_new)
        t = m_new if shifts[c] is None else m_new - shifts[c]
        return m_new, alpha, t

    def scores(j, slot):
        off = pl.multiple_of(j * tk, tk)
        k = k_ref[0, 0, pl.ds(off, tk), :]
        for c in range(2):
            s = jnp.dot(k, qs[c], preferred_element_type=F32)
            if frep_ref is not None:
                s = s - jnp.concatenate([frep_ref[0, c, pl.ds(off, tk), :]] * (tq // LANES), axis=1)
            s_refs[slot][c] = s
            m_prev = mrun_ref[c]
            m_new, alpha, t = stats(c, jnp.max(s, axis=0, keepdims=True), m_prev)
            mrun_ref[c] = m_new
            st_refs[slot][c, ROW_T:ROW_T + 1] = t
            st_refs[slot][c, ROW_ALPHA:ROW_ALPHA + 1] = alpha
            st_refs[slot][c, ROW_MPREV:ROW_MPREV + 1] = m_prev

    def update(j, slot, mask):
        off = pl.multiple_of(j * tk, tk)
        vt = vt_ref[0, :, pl.ds(off, tk)]
        for c in range(2):
            s = s_refs[slot][c]
            if mask is None:
                t = st_refs[slot][c, ROW_T:ROW_T + 1]
                alpha = st_refs[slot][c, ROW_ALPHA:ROW_ALPHA + 1]
            else:
                s = jnp.where(mask, s, NEG_BIG)
                _, alpha, t = stats(c, jnp.max(s, axis=0, keepdims=True),
                                    st_refs[slot][c, ROW_MPREV:ROW_MPREV + 1])
            p = jnp.exp2(s - t)
            if sum_on_mxu:
                vt_c = jnp.concatenate([vt[c * HEAD_DIM:(c + 1) * HEAD_DIM],
                                        jnp.ones((SUM_ROWS, tk), vt.dtype)], axis=0)
            else:
                vt_c = vt
                l_ref[c] = alpha * l_ref[c] + jnp.sum(p, axis=0, keepdims=True)
            acc_ref[c] = alpha * acc_ref[c] + jnp.dot(vt_c, p.astype(BF16), preferred_element_type=F32)

    key = lax.broadcasted_iota(jnp.int32, (tk, tq), 0)
    qry = lax.broadcasted_iota(jnp.int32, (tk, tq), 1)
    if chunk_causal:
        mask = (key // CHUNK) <= (qry // CHUNK)
    else:
        mask = key <= qry

    def run(j0, n_full, then_diagonal):
        for u in range(n_full):
            scores(j0 + u + 1, (u + 1) % 2)
            update(j0 + u, u % 2, None)
        if then_diagonal:
            update(j0 + n_full, n_full % 2, mask)

    scores(0, 0)

    def group(jj, carry):
        run(jj * FLASH_UNROLL, FLASH_UNROLL, False)
        return carry

    lax.fori_loop(0, qi // FLASH_UNROLL, group, 0)
    for rem in range(FLASH_UNROLL):
        @pl.when(qi % FLASH_UNROLL == rem)
        def _():
            run(qi - rem, rem, True)


def _diff_kernel(qt_ref, k_ref, vt_ref, lam_ref, gain_ref, o_ref, s0_ref, s1_ref, st0_ref, st1_ref,
                 mrun_ref, l_ref, acc_ref, *, lam_init):
    _flash_body(qt_ref, k_ref, vt_ref, None, None, (s0_ref, s1_ref), (st0_ref, st1_ref), mrun_ref, l_ref, acc_ref,
                chunk_causal=True, sum_on_mxu=False)
    lf = lam_ref[...]
    lam = (jnp.exp(jnp.sum(lf[0:1] * lf[1:2], axis=-1, keepdims=True))
           - jnp.exp(jnp.sum(lf[2:3] * lf[3:4], axis=-1, keepdims=True)) + lam_init)
    o_t = acc_ref[0] / l_ref[0] - lam * (acc_ref[1] / l_ref[1])
    o_ref[0, 0] = (_rms(o_t.T, gain_ref[...]) * (1.0 - lam_init)).astype(o_ref.dtype)


def _forget_attn_kernel(qt_ref, k_ref, vt_ref, fq_ref, frep_ref, o_ref, s0_ref, s1_ref, st0_ref, st1_ref,
                        mrun_ref, l_ref, acc_ref):
    _flash_body(qt_ref, k_ref, vt_ref, fq_ref, frep_ref, (s0_ref, s1_ref), (st0_ref, st1_ref), mrun_ref, l_ref,
                acc_ref, chunk_causal=False, sum_on_mxu=True)
    heads = [acc_ref[c, :HEAD_DIM] / acc_ref[c, HEAD_DIM:HEAD_DIM + 1] for c in range(2)]
    o_t = jnp.concatenate(heads, axis=0)
    o_ref[0, 0] = o_t.T.astype(o_ref.dtype)


def _flash_scratch(tq, acc_rows):
    return [pltpu.VMEM((2, tq, tq), F32), pltpu.VMEM((2, tq, tq), F32),
            pltpu.VMEM((2, 3, tq), F32), pltpu.VMEM((2, 3, tq), F32),
            pltpu.VMEM((2, 1, tq), F32), pltpu.VMEM((2, 1, tq), F32), pltpu.VMEM((2, acc_rows, tq), F32)]


def _flash_specs(qt_block, k_block, vt_block, tq, S):
    nq = S // tq
    return [pl.BlockSpec((1, LANES, tq), lambda b, p, i: (qt_block + p, 0, b * nq + i)),
            pl.BlockSpec((1, 1, S, LANES), lambda b, p, i: (k_block + p, b, 0, 0)),
            pl.BlockSpec((1, LANES, S), lambda b, p, i: (vt_block + p, 0, b))]


def _diff_attention(proj, ft, lambdas, sub_gain, lam_init, B, S):
    tq = min(TQ_FLASH, S)
    return pl.pallas_call(
        functools.partial(_diff_kernel, lam_init=lam_init),
        out_shape=jax.ShapeDtypeStruct((GROUP_BLOCKS, B, S, LANES), BF16),
        grid=(B, GROUP_BLOCKS, S // tq),
        in_specs=_flash_specs(2 * GROUP_BLOCKS, GROUP_BLOCKS, 0, tq, S) + [
            _const_spec(lambdas.shape), _const_spec((1, LANES))],
        out_specs=pl.BlockSpec((1, 1, tq, LANES), lambda b, p, i: (p, b, i, 0)),
        scratch_shapes=_flash_scratch(tq, LANES),
        compiler_params=_params("parallel", "parallel", "arbitrary"), name="diff_attention",
    )(ft, proj, ft, lambdas.astype(F32), sub_gain.astype(F32).reshape(1, LANES))


def _forget_attention(proj, ft, f_row, f_rep, B, S):
    tq = min(TQ_FLASH, S)
    return pl.pallas_call(
        _forget_attn_kernel,
        out_shape=jax.ShapeDtypeStruct((GROUP_BLOCKS, B, S, LANES), BF16),
        grid=(B, GROUP_BLOCKS, S // tq),
        in_specs=_flash_specs(3 * GROUP_BLOCKS, 2 * GROUP_BLOCKS, GROUP_BLOCKS, tq, S) + [
            pl.BlockSpec((1, 1, 2, S), lambda b, p, i: (b, p, 0, 0)),
            pl.BlockSpec((1, 2, S, LANES), lambda b, p, i: (b, p, 0, 0), pipeline_mode=pl.Buffered(1))],
        out_specs=pl.BlockSpec((1, 1, tq, LANES), lambda b, p, i: (p, b, i, 0)),
        scratch_shapes=_flash_scratch(tq, HEAD_DIM + SUM_ROWS),
        compiler_params=_params("parallel", "parallel", "arbitrary"), name="forget_attention",
    )(ft, proj, ft, f_row, f_rep)


def _merge_kernel(x_ref, h_ref, ya_ref, yb_ref, yc_ref, wg_ref, bg_ref, wbr_ref, wo_ref, o_ref):
    h = h_ref[...]
    merged = None
    for b, y_ref in enumerate((ya_ref, yb_ref, yc_ref)):
        y = jnp.concatenate([y_ref[i] for i in range(y_ref.shape[0])], axis=1)
        t = jnp.dot(y, wbr_ref[b], preferred_element_type=F32)
        z = jnp.dot(h, wg_ref[:, b * D_MODEL:(b + 1) * D_MODEL], preferred_element_type=F32)
        g = jax.nn.sigmoid(z + bg_ref[:, b * D_MODEL:(b + 1) * D_MODEL])
        merged = g * t if merged is None else merged + g * t
    o_ref[...] = x_ref[...] + jnp.dot(merged.astype(BF16), wo_ref[...], preferred_element_type=F32)


def _merge(x2, h, ya, yb, yc, w_gate, b_gate, w_br, w_out):
    T = x2.shape[0]
    tm = min(TM_DENSE, T)
    nblk = ya.shape[0]
    row = pl.BlockSpec((tm, D_MODEL), lambda i: (i, 0))
    yspec = pl.BlockSpec((nblk, tm, LANES), lambda i: (0, i, 0))
    return pl.pallas_call(
        _merge_kernel,
        out_shape=jax.ShapeDtypeStruct((T, D_MODEL), F32),
        grid=(T // tm,),
        in_specs=[row, row, yspec, yspec, yspec, _const_spec(w_gate.shape), _const_spec(b_gate.shape),
                  _const_spec(w_br.shape), _const_spec(w_out.shape)],
        out_specs=row,
        compiler_params=_params("parallel"), name="merge_out",
    )(x2, h, ya, yb, yc, w_gate, b_gate, w_br, w_out)


FF_CHUNK = 1024


def _mlp_kernel(x_ref, gn_ref, w1_ref, w2_ref, gnext_ref, *out_refs):
    x = x_ref[...]
    h2 = _rms(x, gn_ref[...]).astype(BF16)
    acc = x
    for c in range(D_FF // FF_CHUNK):
        a = jnp.dot(h2, w1_ref[:, c * FF_CHUNK:(c + 1) * FF_CHUNK], preferred_element_type=F32)
        a = jnp.square(jnp.maximum(a, 0.0))
        acc = acc + jnp.dot(a.astype(BF16), w2_ref[c * FF_CHUNK:(c + 1) * FF_CHUNK, :],
                            preferred_element_type=F32)
    normed = _rms(acc, gnext_ref[...])
    if len(out_refs) == 2:
        out_refs[0][...] = acc
        out_refs[1][...] = normed.astype(out_refs[1].dtype)
    else:
        out_refs[0][...] = normed.astype(out_refs[0].dtype)


def _mlp(x2, g_mlp, w1, w2, g_next, last):
    T = x2.shape[0]
    tm = min(TM_DENSE, T)
    row = pl.BlockSpec((tm, D_MODEL), lambda i: (i, 0))
    if last:
        out_shape, out_specs = jax.ShapeDtypeStruct((T, D_MODEL), F32), row
    else:
        out_shape = (jax.ShapeDtypeStruct((T, D_MODEL), F32), jax.ShapeDtypeStruct((T, D_MODEL), BF16))
        out_specs = (row, row)
    return pl.pallas_call(
        _mlp_kernel, out_shape=out_shape, grid=(T // tm,),
        in_specs=[row, _const_spec((1, D_MODEL)), _const_spec(w1.shape), _const_spec(w2.shape),
                  _const_spec((1, D_MODEL))],
        out_specs=out_specs,
        compiler_params=_params("parallel"), name="mlp",
    )(x2, g_mlp.reshape(1, D_MODEL), w1, w2, g_next.reshape(1, D_MODEL))


def kernel(x, positions, norm_mix, w_in, rel_bias, lambdas, sub_gain, b_forget, w_br_a, w_br_b, w_br_c,
           w_gate, b_gate, w_out, norm_mlp, w_ff1, w_ff2, final_norm):
    B, S, D = x.shape
    T = B * S
    depth = w_in.shape[0]
    n_main = 9 * BRANCH_WIDTH
    q_scale = HEAD_DIM ** -0.5 * LOG2E
    col_scale = jnp.ones((n_main,), F32)
    for grp in (0, 3, 6):
        col_scale = col_scale.at[grp * BRANCH_WIDTH:(grp + 1) * BRANCH_WIDTH].set(q_scale)

    cosf, sinf = _rope_tables(positions)
    cos_t, sin_t = cosf.T, sinf.T
    x2 = x.reshape(T, D)
    h = _rmsnorm(x2, norm_mix[0])
    out = None
    for l in range(depth):
        lam_init = 0.8 - 0.6 * math.exp(-0.3 * l)
        w_all = w_in[l, :, :n_main] * col_scale[None, :]
        group = lambda g: w_all[:, g * BRANCH_WIDTH:(g + 1) * BRANCH_WIDTH]
        w_main = jnp.concatenate([group(g) for g in ROW_GROUPS], axis=1).astype(BF16)
        w_ft = jnp.concatenate([group(g) for g in FEAT_GROUPS], axis=1).T.astype(BF16)
        w_fl = jnp.zeros((D, LANES), F32).at[:, :C_HEADS].set(w_in[l, :, n_main:]).astype(BF16)
        b_fl = jnp.zeros((1, LANES), F32).at[0, :C_HEADS].set(b_forget[l].astype(F32))

        proj, ft = _inproj(h, w_main, w_ft, cosf, sinf, cos_t, sin_t)
        proj = proj.reshape(len(ROW_GROUPS) * GROUP_BLOCKS, B, S, LANES)
        f_tok, f_rep = _forget_cumsum(h.reshape(B, S, D), w_fl, b_fl)
        f_row = f_tok[:, :, :C_HEADS].transpose(0, 2, 1).reshape(B, C_HEADS // 2, 2, S)

        ya = _band_attention(proj, ft, _band_bias(rel_bias[l]), B, S)
        yb = _diff_attention(proj, ft, lambdas[l], sub_gain[l], lam_init, B, S)
        yc = _forget_attention(proj, ft, f_row, f_rep, B, S)

        w_br = jnp.stack([w_br_a[l], w_br_b[l], w_br_c[l]]).astype(BF16)
        x2 = _merge(x2, h, ya.reshape(GROUP_BLOCKS, T, LANES), yb.reshape(GROUP_BLOCKS, T, LANES),
                    yc.reshape(GROUP_BLOCKS, T, LANES), w_gate[l].astype(BF16),
                    b_gate[l].astype(F32).reshape(1, -1), w_br, w_out[l].astype(BF16))
        last = l == depth - 1
        g_next = final_norm if last else norm_mix[l + 1]
        res = _mlp(x2, norm_mlp[l], w_ff1[l].astype(BF16), w_ff2[l].astype(BF16), g_next, last)
        if last:
            out = res
        else:
            x2, h = res
    return out.reshape(B, S, D)
```

```python
import functools
import math

import jax
import jax.numpy as jnp
import numpy as np
from jax import lax
from jax.experimental import pallas as pl
from jax.experimental.pallas import tpu as pltpu

D_MODEL = 1024
CHUNK = 64
LEFT_CHUNKS = 8
HEAD_DIM = 64
A_HEADS = 8
B_HEADS = 4
C_HEADS = 8
BRANCH_WIDTH = 512
N_BRANCH = 3
D_FF = 4 * D_MODEL
REL_CLIP = 128
ROPE_THETA = 500000.0
ROPE_DIM = HEAD_DIM // 4
ROPE_HALF = ROPE_DIM // 2
EPS = 1e-6

LANES = 128
LOG2E = 1.4426950408889634
NEG_BIG = -1e30

TM_DENSE = 512
TQ_FLASH = 512
TQ_BAND = 1024
SUB_BAND = 4 * CHUNK
WIN_BAND = SUB_BAND + LEFT_CHUNKS * CHUNK
VMEM_LIMIT = 56 * 1024 * 1024

BF16 = jnp.bfloat16
F32 = jnp.float32


def _params(*sem, flags=None):
    return pltpu.CompilerParams(dimension_semantics=sem, vmem_limit_bytes=VMEM_LIMIT, flags=flags)


FLASH_FLAGS = None


def _const_spec(shape):
    nd = len(shape)
    return pl.BlockSpec(shape, lambda *_: (0,) * nd, pipeline_mode=pl.Buffered(1))


def _rms(x, g):
    return x * lax.rsqrt(jnp.mean(x * x, axis=-1, keepdims=True) + EPS) * g


def _rope_table_kernel(ang_ref, cos_ref, sin_ref):
    ang = ang_ref[...]
    lane = lax.broadcasted_iota(jnp.int32, ang.shape, 1) % HEAD_DIM
    sign = jnp.where(lane < ROPE_HALF, -1.0, 1.0)
    cos_ref[...] = jnp.cos(ang)
    sin_ref[...] = jnp.sin(ang) * sign


def _rope_tables(positions):
    T = positions.size
    inv_freq = jnp.power(ROPE_THETA, -jnp.arange(0, ROPE_DIM, 2, dtype=F32) / ROPE_DIM)
    lane = jnp.arange(LANES) % HEAD_DIM
    freq_lane = jnp.where(lane < ROPE_DIM, inv_freq[lane % ROPE_HALF], 0.0)
    ang = positions.reshape(T, 1).astype(F32) * freq_lane[None, :]
    tm = min(1024, T)
    spec = pl.BlockSpec((tm, LANES), lambda i: (i, 0))
    return pl.pallas_call(
        _rope_table_kernel,
        out_shape=(jax.ShapeDtypeStruct((T, LANES), F32),) * 2,
        grid=(T // tm,), in_specs=[spec], out_specs=(spec, spec),
        compiler_params=_params("parallel"), name="rope_tables",
    )(ang)


def _rmsnorm_kernel(x_ref, g_ref, h_ref):
    h_ref[...] = _rms(x_ref[...], g_ref[...]).astype(h_ref.dtype)


def _rmsnorm(x2, g):
    T = x2.shape[0]
    tm = min(1024, T)
    return pl.pallas_call(
        _rmsnorm_kernel,
        out_shape=jax.ShapeDtypeStruct((T, D_MODEL), BF16),
        grid=(T // tm,),
        in_specs=[pl.BlockSpec((tm, D_MODEL), lambda i: (i, 0)), _const_spec((1, D_MODEL))],
        out_specs=pl.BlockSpec((tm, D_MODEL), lambda i: (i, 0)),
        compiler_params=_params("parallel"), name="rmsnorm",
    )(x2, g.reshape(1, D_MODEL))


ROW_GROUPS = (1, 4, 7)
FEAT_GROUPS = (5, 8, 3, 6, 0, 2)
ROW_ROPE_GROUP = 1
FEAT_ROPE_GROUP = 2
GROUP_BLOCKS = BRANCH_WIDTH // LANES


def _inproj_kernel(h_ref, w_ref, wft_ref, cos_ref, sin_ref, cos_t_ref, sin_t_ref, o_ref, ft_ref):
    h = h_ref[...]
    cosf = cos_ref[...]
    sinf = sin_ref[...]
    lane = lax.broadcasted_iota(jnp.int32, cosf.shape, 1) % HEAD_DIM
    first_half = lane < ROPE_HALF
    for c in range(len(ROW_GROUPS)):
        res = jnp.dot(h, w_ref[:, c * BRANCH_WIDTH:(c + 1) * BRANCH_WIDTH],
                      preferred_element_type=F32)
        for s in range(GROUP_BLOCKS):
            blk = res[:, s * LANES:(s + 1) * LANES]
            if c == ROW_ROPE_GROUP:
                partner = jnp.where(first_half,
                                    pltpu.roll(blk, LANES - ROPE_HALF, 1),
                                    pltpu.roll(blk, ROPE_HALF, 1))
                blk = blk * cosf + partner * sinf
            o_ref[c * GROUP_BLOCKS + s] = blk.astype(o_ref.dtype)
    for c in range(len(FEAT_GROUPS)):
        res_t = lax.dot_general(wft_ref[c * BRANCH_WIDTH:(c + 1) * BRANCH_WIDTH, :], h,
                                (((1,), (1,)), ((), ())), preferred_element_type=F32)
        for s in range(GROUP_BLOCKS):
            blk = res_t[s * LANES:(s + 1) * LANES, :]
            if c == FEAT_ROPE_GROUP:
                pieces = []
                for base in range(0, LANES, HEAD_DIM):
                    x1 = blk[base:base + ROPE_HALF]
                    x2 = blk[base + ROPE_HALF:base + ROPE_DIM]
                    rot = slice(base, base + ROPE_DIM)
                    pieces.append(jnp.concatenate([x1, x2], axis=0) * cos_t_ref[rot, :]
                                  + jnp.concatenate([x2, x1], axis=0) * sin_t_ref[rot, :])
                    pieces.append(blk[base + ROPE_DIM:base + HEAD_DIM])
                blk = jnp.concatenate(pieces, axis=0)
            ft_ref[c * GROUP_BLOCKS + s] = blk.astype(ft_ref.dtype)


def _inproj(h, w_main, w_ft, cosf, sinf, cos_t, sin_t):
    T = h.shape[0]
    tm = min(TM_DENSE, T)
    n_row = len(ROW_GROUPS) * GROUP_BLOCKS
    n_ft = len(FEAT_GROUPS) * GROUP_BLOCKS
    tok = pl.BlockSpec((tm, LANES), lambda i: (i, 0))
    feat = pl.BlockSpec((LANES, tm), lambda i: (0, i))
    return pl.pallas_call(
        _inproj_kernel,
        out_shape=(jax.ShapeDtypeStruct((n_row, T, LANES), BF16),
                   jax.ShapeDtypeStruct((n_ft, LANES, T), BF16)),
        grid=(T // tm,),
        in_specs=[pl.BlockSpec((tm, D_MODEL), lambda i: (i, 0)),
                  _const_spec(w_main.shape), _const_spec(w_ft.shape), tok, tok, feat, feat],
        out_specs=(pl.BlockSpec((n_row, tm, LANES), lambda i: (0, i, 0)),
                   pl.BlockSpec((n_ft, LANES, tm), lambda i: (0, 0, i))),
        compiler_params=_params("parallel"), name="inproj",
    )(h, w_main, w_ft, cosf, sinf, cos_t, sin_t)


def _split3(x):
    hi = x.astype(BF16)
    r1 = x - hi.astype(F32)
    mid = r1.astype(BF16)
    lo = (r1 - mid.astype(F32)).astype(BF16)
    return hi, mid, lo


def _forget_kernel(h_ref, w_ref, b_ref, f_ref, kbias_ref, carry_ref):
    @pl.when(pl.program_id(1) == 0)
    def _():
        carry_ref[...] = jnp.zeros_like(carry_ref)

    fl = jnp.dot(h_ref[0], w_ref[...], preferred_element_type=F32) + b_ref[...]
    ls = jnp.minimum(fl, 0.0) - jnp.log1p(jnp.exp(-jnp.abs(fl)))
    tm = ls.shape[0]
    row = lax.broadcasted_iota(jnp.int32, (tm, tm), 0)
    col = lax.broadcasted_iota(jnp.int32, (tm, tm), 1)
    tri = (col <= row).astype(BF16)
    cs = sum(jnp.dot(tri, part, preferred_element_type=F32) for part in _split3(ls))
    f = cs + carry_ref[...]
    carry_ref[...] = f[tm - 1:tm, :]
    f2 = f * LOG2E
    f_ref[0] = f2
    parts = jnp.concatenate(_split3(f2), axis=1)
    n_pairs = C_HEADS // 2
    src = lax.broadcasted_iota(jnp.int32, (3 * LANES, n_pairs * LANES), 0)
    dst = lax.broadcasted_iota(jnp.int32, (3 * LANES, n_pairs * LANES), 1)
    pair, lane = dst // LANES, dst % LANES
    chain, piece = lane // 3, lane % 3
    place = (lane < 2 * 3) & (src == piece * LANES + 2 * pair + chain)
    placed = jnp.dot(parts, jnp.where(place, -1.0, 0.0).astype(BF16), preferred_element_type=F32)
    for p in range(n_pairs):
        kbias_ref[0, p] = placed[:, p * LANES:(p + 1) * LANES].astype(kbias_ref.dtype)


def _forget_cumsum(h3, w_fl, b_fl):
    B, S, _ = h3.shape
    tm = min(TM_DENSE, S)
    n_pairs = C_HEADS // 2
    return pl.pallas_call(
        _forget_kernel,
        out_shape=(jax.ShapeDtypeStruct((B, S, LANES), F32),
                   jax.ShapeDtypeStruct((B, n_pairs, S, LANES), BF16)),
        grid=(B, S // tm),
        in_specs=[pl.BlockSpec((1, tm, D_MODEL), lambda b, i: (b, i, 0)),
                  _const_spec(w_fl.shape), _const_spec(b_fl.shape)],
        out_specs=(pl.BlockSpec((1, tm, LANES), lambda b, i: (b, i, 0)),
                   pl.BlockSpec((1, n_pairs, tm, LANES), lambda b, i: (b, 0, i, 0))),
        scratch_shapes=[pltpu.VMEM((1, LANES), F32)],
        compiler_params=_params("parallel", "arbitrary"), name="forget_cumsum",
    )(h3, w_fl, b_fl)


def _band_kernel(qt_ref, k_ref, vt_ref, bias_ref, o_ref):
    qi = pl.program_id(2)
    tq = qt_ref.shape[2]
    feat = lax.broadcasted_iota(jnp.int32, (LANES, 1), 0)
    own = [feat < HEAD_DIM, feat >= HEAD_DIM]
    pad = WIN_BAND - SUB_BAND

    def window(r):
        start = qi * tq + (r * SUB_BAND - pad)
        clipped = pl.multiple_of(jnp.maximum(start, 0), SUB_BAND)
        return clipped, pl.multiple_of(clipped - start, SUB_BAND)

    def scores(r, c):
        start, row0 = window(r)
        qt = qt_ref[0, :, r * SUB_BAND:(r + 1) * SUB_BAND]
        qc = jnp.where(own[c], qt, jnp.zeros_like(qt))
        s = jnp.dot(k_ref[0, 0, pl.ds(start, WIN_BAND), :], qc, preferred_element_type=F32)
        s = s + bias_ref[c, pl.ds(row0, WIN_BAND), :]
        return s, jnp.max(s, axis=0, keepdims=True)

    def head_out(r, c, s, m):
        start, _ = window(r)
        p = jnp.exp2((s - m).astype(BF16))
        vt_c = jnp.concatenate([vt_ref[0, c * HEAD_DIM:(c + 1) * HEAD_DIM, pl.ds(start, WIN_BAND)],
                                jnp.ones((SUM_ROWS, WIN_BAND), vt_ref.dtype)], axis=0)
        acc = jnp.dot(vt_c, p, preferred_element_type=F32)
        return acc[:HEAD_DIM] / acc[HEAD_DIM:HEAD_DIM + 1]

    units = [(r, c) for r in range(tq // SUB_BAND) for c in range(2)]
    heads = {}
    pending = None
    for unit in units + [None]:
        current = None if unit is None else (unit, scores(*unit))
        if pending is not None:
            (r, c), (s, m) = pending
            heads[c] = head_out(r, c, s, m)
            if c == 1:
                o_t = jnp.concatenate([heads[0], heads[1]], axis=0)
                o_ref[0, 0, r * SUB_BAND:(r + 1) * SUB_BAND, :] = o_t.T.astype(o_ref.dtype)
        pending = current


def _band_attention(proj, ft, bias, B, S):
    tq = min(TQ_BAND, S)
    nq = S // tq
    return pl.pallas_call(
        _band_kernel,
        out_shape=jax.ShapeDtypeStruct((GROUP_BLOCKS, B, S, LANES), BF16),
        grid=(B, GROUP_BLOCKS, nq),
        in_specs=[pl.BlockSpec((1, LANES, tq), lambda b, p, i: (4 * GROUP_BLOCKS + p, 0, b * nq + i)),
                  pl.BlockSpec((1, 1, S, LANES), lambda b, p, i: (p, b, 0, 0)),
                  pl.BlockSpec((1, LANES, S), lambda b, p, i: (5 * GROUP_BLOCKS + p, 0, b)),
                  pl.BlockSpec((2,) + bias.shape[1:], lambda b, p, i: (p, 0, 0))],
        out_specs=pl.BlockSpec((1, 1, tq, LANES), lambda b, p, i: (p, b, i, 0)),
        compiler_params=_params("parallel", "parallel", "arbitrary"), name="band_attention",
    )(ft, proj, ft, bias)


def _band_bias(rel_bias):
    period = 2 * WIN_BAND
    assert period > WIN_BAND + SUB_BAND - 1
    u = np.arange(period)
    diag = np.where(u < WIN_BAND, u, u - period)
    rel = np.clip(diag - LEFT_CHUNKS * CHUNK, -REL_CLIP, REL_CLIP) + REL_CLIP
    profile = rel_bias.astype(F32)[:, rel] * LOG2E
    heads = profile.shape[0]
    table = jnp.tile(profile, (1, SUB_BAND))[:, :SUB_BAND * (period - 1)]
    table = table.reshape(heads, SUB_BAND, period - 1)[:, :, :WIN_BAND]
    i = np.arange(SUB_BAND)
    k_off = np.arange(WIN_BAND)[None, :] - LEFT_CHUNKS * CHUNK - (i // CHUNK)[:, None] * CHUNK
    valid = (k_off >= -LEFT_CHUNKS * CHUNK) & (k_off < CHUNK)
    table = jnp.where(jnp.asarray(valid)[None], table, NEG_BIG).transpose(0, 2, 1)
    tail = jnp.full((heads, WIN_BAND - SUB_BAND, SUB_BAND), NEG_BIG, F32)
    return jnp.concatenate([table, tail], axis=1)


ROW_T, ROW_ALPHA, ROW_MPREV = 0, 1, 2
FLASH_UNROLL = 4
SUM_ROWS = 16
assert FLASH_UNROLL % 2 == 0


def _flash_body(qt_ref, k_ref, vt_ref, fq_ref, kbias_ref, s_refs, st_refs, mrun_ref, acc_ref, *,
                chunk_causal, split_values):
    qi = pl.program_id(2)
    tq = qt_ref.shape[2]
    tk = tq
    feat = lax.broadcasted_iota(jnp.int32, (LANES, 1), 0)
    own = [feat < HEAD_DIM, feat >= HEAD_DIM]
    qt = qt_ref[0]
    qs = [jnp.where(own[c], qt, jnp.zeros_like(qt)) for c in range(2)]
    if kbias_ref is not None:
        qs = [jnp.concatenate([qs[c], ((feat >= 3 * c) & (feat < 3 * c + 3)).astype(qt.dtype)
                               * jnp.ones_like(qt)], axis=0) for c in range(2)]
    mrun_ref[...] = jnp.full(mrun_ref.shape, NEG_BIG, F32)
    acc_ref[...] = jnp.zeros(acc_ref.shape, F32)
    q_off = pl.multiple_of(qi * tq, tq)
    shifts = [None if fq_ref is None else fq_ref[0, 0, c:c + 1, pl.ds(q_off, tq)] for c in range(2)]

    def stats(c, col_max, m_prev):
        if shifts[c] is not None:
            col_max = col_max + shifts[c]
        m_new = jnp.maximum(m_prev, col_max)
        alpha = jnp.exp2(m_prev - m_new)
        t = m_new if shifts[c] is None else m_new - shifts[c]
        return m_new, alpha, t

    def scores(j, slot):
        off = pl.multiple_of(j * tk, tk)
        k = k_ref[0, 0, pl.ds(off, tk), :]
        if kbias_ref is not None:
            k = jnp.concatenate([k, kbias_ref[0, 0, pl.ds(off, tk), :]], axis=1)
        for c in range(2):
            s = jnp.dot(k, qs[c], preferred_element_type=F32)
            s_refs[slot][c] = s
            m_prev = mrun_ref[c]
            m_new, alpha, t = stats(c, jnp.max(s, axis=0, keepdims=True), m_prev)
            mrun_ref[c] = m_new
            st_refs[slot][c, ROW_T:ROW_T + 1] = t
            st_refs[slot][c, ROW_ALPHA:ROW_ALPHA + 1] = alpha
            st_refs[slot][c, ROW_MPREV:ROW_MPREV + 1] = m_prev

    def update(j, slot, mask):
        off = pl.multiple_of(j * tk, tk)
        vt = vt_ref[0, :, pl.ds(off, tk)]
        for c in range(2):
            s = s_refs[slot][c]
            if mask is None:
                t = st_refs[slot][c, ROW_T:ROW_T + 1]
                alpha = st_refs[slot][c, ROW_ALPHA:ROW_ALPHA + 1]
            else:
                s = jnp.where(mask, s, NEG_BIG)
                _, alpha, t = stats(c, jnp.max(s, axis=0, keepdims=True),
                                    st_refs[slot][c, ROW_MPREV:ROW_MPREV + 1])
            p = jnp.exp2(s - t).astype(BF16)
            values = vt[c * HEAD_DIM:(c + 1) * HEAD_DIM] if split_values else vt
            vt_c = jnp.concatenate([values, jnp.ones((SUM_ROWS, tk), vt.dtype)], axis=0)
            acc_ref[c] = alpha * acc_ref[c] + jnp.dot(vt_c, p, preferred_element_type=F32)

    key = lax.broadcasted_iota(jnp.int32, (tk, tq), 0)
    qry = lax.broadcasted_iota(jnp.int32, (tk, tq), 1)
    if chunk_causal:
        mask = (key // CHUNK) <= (qry // CHUNK)
    else:
        mask = key <= qry

    def run(j0, n_full, then_diagonal):
        for u in range(n_full):
            scores(j0 + u + 1, (u + 1) % 2)
            update(j0 + u, u % 2, None)
        if then_diagonal:
            update(j0 + n_full, n_full % 2, mask)

    scores(0, 0)

    def group(jj, carry):
        run(jj * FLASH_UNROLL, FLASH_UNROLL, False)
        return carry

    lax.fori_loop(0, qi // FLASH_UNROLL, group, 0)
    for rem in range(FLASH_UNROLL):
        @pl.when(qi % FLASH_UNROLL == rem)
        def _():
            run(qi - rem, rem, True)


def _diff_kernel(qt_ref, k_ref, vt_ref, lam_ref, gain_ref, o_ref, s0_ref, s1_ref, st0_ref, st1_ref,
                 mrun_ref, acc_ref, *, lam_init):
    _flash_body(qt_ref, k_ref, vt_ref, None, None, (s0_ref, s1_ref), (st0_ref, st1_ref), mrun_ref, acc_ref,
                chunk_causal=True, split_values=False)
    lf = lam_ref[...]
    lam = (jnp.exp(jnp.sum(lf[0:1] * lf[1:2], axis=-1, keepdims=True))
           - jnp.exp(jnp.sum(lf[2:3] * lf[3:4], axis=-1, keepdims=True)) + lam_init)
    maps = [acc_ref[c, :LANES] / acc_ref[c, LANES:LANES + 1] for c in range(2)]
    o_t = maps[0] - lam * maps[1]
    o_ref[0, 0] = (_rms(o_t.T, gain_ref[...]) * (1.0 - lam_init)).astype(o_ref.dtype)


def _forget_attn_kernel(qt_ref, k_ref, vt_ref, fq_ref, kbias_ref, o_ref, s0_ref, s1_ref, st0_ref, st1_ref,
                        mrun_ref, acc_ref):
    _flash_body(qt_ref, k_ref, vt_ref, fq_ref, kbias_ref, (s0_ref, s1_ref), (st0_ref, st1_ref), mrun_ref,
                acc_ref, chunk_causal=False, split_values=True)
    heads = [acc_ref[c, :HEAD_DIM] / acc_ref[c, HEAD_DIM:HEAD_DIM + 1] for c in range(2)]
    o_t = jnp.concatenate(heads, axis=0)
    o_ref[0, 0] = o_t.T.astype(o_ref.dtype)


def _flash_scratch(tq, acc_rows):
    return [pltpu.VMEM((2, tq, tq), F32), pltpu.VMEM((2, tq, tq), F32),
            pltpu.VMEM((2, 3, tq), F32), pltpu.VMEM((2, 3, tq), F32),
            pltpu.VMEM((2, 1, tq), F32), pltpu.VMEM((2, acc_rows, tq), F32)]


def _flash_specs(qt_block, k_block, vt_block, tq, S):
    nq = S // tq
    return [pl.BlockSpec((1, LANES, tq), lambda b, p, i: (qt_block + p, 0, b * nq + i)),
            pl.BlockSpec((1, 1, S, LANES), lambda b, p, i: (k_block + p, b, 0, 0)),
            pl.BlockSpec((1, LANES, S), lambda b, p, i: (vt_block + p, 0, b))]


def _diff_attention(proj, ft, lambdas, sub_gain, lam_init, B, S):
    tq = min(TQ_FLASH, S)
    return pl.pallas_call(
        functools.partial(_diff_kernel, lam_init=lam_init),
        out_shape=jax.ShapeDtypeStruct((GROUP_BLOCKS, B, S, LANES), BF16),
        grid=(B, GROUP_BLOCKS, S // tq),
        in_specs=_flash_specs(2 * GROUP_BLOCKS, GROUP_BLOCKS, 0, tq, S) + [
            _const_spec(lambdas.shape), _const_spec((1, LANES))],
        out_specs=pl.BlockSpec((1, 1, tq, LANES), lambda b, p, i: (p, b, i, 0)),
        scratch_shapes=_flash_scratch(tq, LANES + SUM_ROWS),
        compiler_params=_params("parallel", "parallel", "arbitrary", flags=FLASH_FLAGS), name="diff_attention",
    )(ft, proj, ft, lambdas.astype(F32), sub_gain.astype(F32).reshape(1, LANES))


def _forget_attention(proj, ft, f_row, k_bias, B, S):
    tq = min(TQ_FLASH, S)
    return pl.pallas_call(
        _forget_attn_kernel,
        out_shape=jax.ShapeDtypeStruct((GROUP_BLOCKS, B, S, LANES), BF16),
        grid=(B, GROUP_BLOCKS, S // tq),
        in_specs=_flash_specs(3 * GROUP_BLOCKS, 2 * GROUP_BLOCKS, GROUP_BLOCKS, tq, S) + [
            pl.BlockSpec((1, 1, 2, S), lambda b, p, i: (b, p, 0, 0)),
            pl.BlockSpec((1, 1, S, LANES), lambda b, p, i: (b, p, 0, 0))],
        out_specs=pl.BlockSpec((1, 1, tq, LANES), lambda b, p, i: (p, b, i, 0)),
        scratch_shapes=_flash_scratch(tq, HEAD_DIM + SUM_ROWS),
        compiler_params=_params("parallel", "parallel", "arbitrary", flags=FLASH_FLAGS), name="forget_attention",
    )(ft, proj, ft, f_row, k_bias)


def _merge_kernel(x_ref, h_ref, ya_ref, yb_ref, yc_ref, wg_ref, bg_ref, wbr_ref, wo_ref, o_ref):
    h = h_ref[...]
    merged = None
    for b, y_ref in enumerate((ya_ref, yb_ref, yc_ref)):
        y = jnp.concatenate([y_ref[i] for i in range(y_ref.shape[0])], axis=1)
        t = jnp.dot(y, wbr_ref[b], preferred_element_type=F32)
        z = jnp.dot(h, wg_ref[:, b * D_MODEL:(b + 1) * D_MODEL], preferred_element_type=F32)
        g = jax.nn.sigmoid(z + bg_ref[:, b * D_MODEL:(b + 1) * D_MODEL])
        merged = g * t if merged is None else merged + g * t
    o_ref[...] = x_ref[...] + jnp.dot(merged.astype(BF16), wo_ref[...], preferred_element_type=F32)


def _merge(x2, h, ya, yb, yc, w_gate, b_gate, w_br, w_out):
    T = x2.shape[0]
    tm = min(TM_DENSE, T)
    nblk = ya.shape[0]
    row = pl.BlockSpec((tm, D_MODEL), lambda i: (i, 0))
    yspec = pl.BlockSpec((nblk, tm, LANES), lambda i: (0, i, 0))
    return pl.pallas_call(
        _merge_kernel,
        out_shape=jax.ShapeDtypeStruct((T, D_MODEL), F32),
        grid=(T // tm,),
        in_specs=[row, row, yspec, yspec, yspec, _const_spec(w_gate.shape), _const_spec(b_gate.shape),
                  _const_spec(w_br.shape), _const_spec(w_out.shape)],
        out_specs=row,
        compiler_params=_params("parallel"), name="merge_out",
    )(x2, h, ya, yb, yc, w_gate, b_gate, w_br, w_out)


FF_CHUNK = 1024


def _mlp_kernel(x_ref, gn_ref, w1_ref, w2_ref, gnext_ref, *out_refs):
    x = x_ref[...]
    h2 = _rms(x, gn_ref[...]).astype(BF16)
    acc = x
    for c in range(D_FF // FF_CHUNK):
        a = jnp.dot(h2, w1_ref[:, c * FF_CHUNK:(c + 1) * FF_CHUNK], preferred_element_type=F32)
        a = jnp.square(jnp.maximum(a, 0.0))
        acc = acc + jnp.dot(a.astype(BF16), w2_ref[c * FF_CHUNK:(c + 1) * FF_CHUNK, :],
                            preferred_element_type=F32)
    normed = _rms(acc, gnext_ref[...])
    if len(out_refs) == 2:
        out_refs[0][...] = acc
        out_refs[1][...] = normed.astype(out_refs[1].dtype)
    else:
        out_refs[0][...] = normed.astype(out_refs[0].dtype)


def _mlp(x2, g_mlp, w1, w2, g_next, last):
    T = x2.shape[0]
    tm = min(TM_DENSE, T)
    row = pl.BlockSpec((tm, D_MODEL), lambda i: (i, 0))
    if last:
        out_shape, out_specs = jax.ShapeDtypeStruct((T, D_MODEL), F32), row
    else:
        out_shape = (jax.ShapeDtypeStruct((T, D_MODEL), F32), jax.ShapeDtypeStruct((T, D_MODEL), BF16))
        out_specs = (row, row)
    return pl.pallas_call(
        _mlp_kernel, out_shape=out_shape, grid=(T // tm,),
        in_specs=[row, _const_spec((1, D_MODEL)), _const_spec(w1.shape), _const_spec(w2.shape),
                  _const_spec((1, D_MODEL))],
        out_specs=out_specs,
        compiler_params=_params("parallel"), name="mlp",
    )(x2, g_mlp.reshape(1, D_MODEL), w1, w2, g_next.reshape(1, D_MODEL))


def kernel(x, positions, norm_mix, w_in, rel_bias, lambdas, sub_gain, b_forget, w_br_a, w_br_b, w_br_c,
           w_gate, b_gate, w_out, norm_mlp, w_ff1, w_ff2, final_norm):
    B, S, D = x.shape
    T = B * S
    depth = w_in.shape[0]
    n_main = 9 * BRANCH_WIDTH
    q_scale = HEAD_DIM ** -0.5 * LOG2E
    col_scale = jnp.ones((n_main,), F32)
    for grp in (0, 3, 6):
        col_scale = col_scale.at[grp * BRANCH_WIDTH:(grp + 1) * BRANCH_WIDTH].set(q_scale)

    cosf, sinf = _rope_tables(positions)
    cos_t, sin_t = cosf.T, sinf.T
    x2 = x.reshape(T, D)
    h = _rmsnorm(x2, norm_mix[0])
    out = None
    for l in range(depth):
        lam_init = 0.8 - 0.6 * math.exp(-0.3 * l)
        w_all = w_in[l, :, :n_main] * col_scale[None, :]
        group = lambda g: w_all[:, g * BRANCH_WIDTH:(g + 1) * BRANCH_WIDTH]
        w_main = jnp.concatenate([group(g) for g in ROW_GROUPS], axis=1).astype(BF16)
        w_ft = jnp.concatenate([group(g) for g in FEAT_GROUPS], axis=1).T.astype(BF16)
        w_fl = jnp.zeros((D, LANES), F32).at[:, :C_HEADS].set(w_in[l, :, n_main:]).astype(BF16)
        b_fl = jnp.zeros((1, LANES), F32).at[0, :C_HEADS].set(b_forget[l].astype(F32))

        proj, ft = _inproj(h, w_main, w_ft, cosf, sinf, cos_t, sin_t)
        proj = proj.reshape(len(ROW_GROUPS) * GROUP_BLOCKS, B, S, LANES)
        f_tok, k_bias = _forget_cumsum(h.reshape(B, S, D), w_fl, b_fl)
        f_row = f_tok[:, :, :C_HEADS].transpose(0, 2, 1).reshape(B, C_HEADS // 2, 2, S)

        ya = _band_attention(proj, ft, _band_bias(rel_bias[l]), B, S)
        yb = _diff_attention(proj, ft, lambdas[l], sub_gain[l], lam_init, B, S)
        yc = _forget_attention(proj, ft, f_row, k_bias, B, S)

        w_br = jnp.stack([w_br_a[l], w_br_b[l], w_br_c[l]]).astype(BF16)
        x2 = _merge(x2, h, ya.reshape(GROUP_BLOCKS, T, LANES), yb.reshape(GROUP_BLOCKS, T, LANES),
                    yc.reshape(GROUP_BLOCKS, T, LANES), w_gate[l].astype(BF16),
                    b_gate[l].astype(F32).reshape(1, -1), w_br, w_out[l].astype(BF16))
        last = l == depth - 1
        g_next = final_norm if last else norm_mix[l + 1]
        res = _mlp(x2, norm_mlp[l], w_ff1[l].astype(BF16), w_ff2[l].astype(BF16), g_next, last)
        if last:
            out = res
        else:
            x2, h = res
    return out.reshape(B, S, D)
```

```python
import functools
import math

import jax
import jax.numpy as jnp
import numpy as np
from jax import lax
from jax.experimental import pallas as pl
from jax.experimental.pallas import tpu as pltpu

D_MODEL = 1024
CHUNK = 64
LEFT_CHUNKS = 8
HEAD_DIM = 64
A_HEADS = 8
B_HEADS = 4
C_HEADS = 8
BRANCH_WIDTH = 512
N_BRANCH = 3
D_FF = 4 * D_MODEL
REL_CLIP = 128
ROPE_THETA = 500000.0
ROPE_DIM = HEAD_DIM // 4
ROPE_HALF = ROPE_DIM // 2
EPS = 1e-6

LANES = 128
LOG2E = 1.4426950408889634
NEG_BIG = -1e30

TM_DENSE = 512
TQ_FLASH = 512
TQ_BAND = 2048
SUB_BAND = 4 * CHUNK
WIN_BAND = SUB_BAND + LEFT_CHUNKS * CHUNK
VMEM_LIMIT = 56 * 1024 * 1024

BF16 = jnp.bfloat16
F32 = jnp.float32


def _params(*sem):
    return pltpu.CompilerParams(dimension_semantics=sem, vmem_limit_bytes=VMEM_LIMIT)


def _const_spec(shape):
    nd = len(shape)
    return pl.BlockSpec(shape, lambda *_: (0,) * nd, pipeline_mode=pl.Buffered(1))


def _rms(x, g):
    return x * lax.rsqrt(jnp.mean(x * x, axis=-1, keepdims=True) + EPS) * g


def _rope_table_kernel(ang_ref, cos_ref, sin_ref):
    ang = ang_ref[...]
    lane = lax.broadcasted_iota(jnp.int32, ang.shape, 1) % HEAD_DIM
    sign = jnp.where(lane < ROPE_HALF, -1.0, 1.0)
    cos_ref[...] = jnp.cos(ang)
    sin_ref[...] = jnp.sin(ang) * sign


def _rope_tables(positions):
    T = positions.size
    inv_freq = jnp.power(ROPE_THETA, -jnp.arange(0, ROPE_DIM, 2, dtype=F32) / ROPE_DIM)
    lane = jnp.arange(LANES) % HEAD_DIM
    freq_lane = jnp.where(lane < ROPE_DIM, inv_freq[lane % ROPE_HALF], 0.0)
    ang = positions.reshape(T, 1).astype(F32) * freq_lane[None, :]
    tm = min(1024, T)
    spec = pl.BlockSpec((tm, LANES), lambda i: (i, 0))
    return pl.pallas_call(
        _rope_table_kernel,
        out_shape=(jax.ShapeDtypeStruct((T, LANES), F32),) * 2,
        grid=(T // tm,), in_specs=[spec], out_specs=(spec, spec),
        compiler_params=_params("parallel"), name="rope_tables",
    )(ang)


def _rmsnorm_kernel(x_ref, g_ref, h_ref):
    h_ref[...] = _rms(x_ref[...], g_ref[...]).astype(h_ref.dtype)


def _rmsnorm(x2, g):
    T = x2.shape[0]
    tm = min(1024, T)
    return pl.pallas_call(
        _rmsnorm_kernel,
        out_shape=jax.ShapeDtypeStruct((T, D_MODEL), BF16),
        grid=(T // tm,),
        in_specs=[pl.BlockSpec((tm, D_MODEL), lambda i: (i, 0)), _const_spec((1, D_MODEL))],
        out_specs=pl.BlockSpec((tm, D_MODEL), lambda i: (i, 0)),
        compiler_params=_params("parallel"), name="rmsnorm",
    )(x2, g.reshape(1, D_MODEL))


ROW_GROUPS = (1, 4, 7)
FEAT_GROUPS = (5, 8, 3, 6, 0, 2)
ROW_ROPE_GROUP = 1
FEAT_ROPE_GROUP = 2
GROUP_BLOCKS = BRANCH_WIDTH // LANES


def _inproj_kernel(h_ref, w_ref, wft_ref, cos_ref, sin_ref, cos_t_ref, sin_t_ref, o_ref, ft_ref):
    h = h_ref[...]
    cosf = cos_ref[...]
    sinf = sin_ref[...]
    lane = lax.broadcasted_iota(jnp.int32, cosf.shape, 1) % HEAD_DIM
    first_half = lane < ROPE_HALF
    for c in range(len(ROW_GROUPS)):
        res = jnp.dot(h, w_ref[:, c * BRANCH_WIDTH:(c + 1) * BRANCH_WIDTH],
                      preferred_element_type=F32)
        for s in range(GROUP_BLOCKS):
            blk = res[:, s * LANES:(s + 1) * LANES]
            if c == ROW_ROPE_GROUP:
                partner = jnp.where(first_half,
                                    pltpu.roll(blk, LANES - ROPE_HALF, 1),
                                    pltpu.roll(blk, ROPE_HALF, 1))
                blk = blk * cosf + partner * sinf
            o_ref[c * GROUP_BLOCKS + s] = blk.astype(o_ref.dtype)
    for c in range(len(FEAT_GROUPS)):
        res_t = lax.dot_general(wft_ref[c * BRANCH_WIDTH:(c + 1) * BRANCH_WIDTH, :], h,
                                (((1,), (1,)), ((), ())), preferred_element_type=F32)
        for s in range(GROUP_BLOCKS):
            blk = res_t[s * LANES:(s + 1) * LANES, :]
            if c == FEAT_ROPE_GROUP:
                pieces = []
                for base in range(0, LANES, HEAD_DIM):
                    x1 = blk[base:base + ROPE_HALF]
                    x2 = blk[base + ROPE_HALF:base + ROPE_DIM]
                    rot = slice(base, base + ROPE_DIM)
                    pieces.append(jnp.concatenate([x1, x2], axis=0) * cos_t_ref[rot, :]
                                  + jnp.concatenate([x2, x1], axis=0) * sin_t_ref[rot, :])
                    pieces.append(blk[base + ROPE_DIM:base + HEAD_DIM])
                blk = jnp.concatenate(pieces, axis=0)
            ft_ref[c * GROUP_BLOCKS + s] = blk.astype(ft_ref.dtype)


def _inproj(h, w_main, w_ft, cosf, sinf, cos_t, sin_t):
    T = h.shape[0]
    tm = min(TM_DENSE, T)
    n_row = len(ROW_GROUPS) * GROUP_BLOCKS
    n_ft = len(FEAT_GROUPS) * GROUP_BLOCKS
    tok = pl.BlockSpec((tm, LANES), lambda i: (i, 0))
    feat = pl.BlockSpec((LANES, tm), lambda i: (0, i))
    return pl.pallas_call(
        _inproj_kernel,
        out_shape=(jax.ShapeDtypeStruct((n_row, T, LANES), BF16),
                   jax.ShapeDtypeStruct((n_ft, LANES, T), BF16)),
        grid=(T // tm,),
        in_specs=[pl.BlockSpec((tm, D_MODEL), lambda i: (i, 0)),
                  _const_spec(w_main.shape), _const_spec(w_ft.shape), tok, tok, feat, feat],
        out_specs=(pl.BlockSpec((n_row, tm, LANES), lambda i: (0, i, 0)),
                   pl.BlockSpec((n_ft, LANES, tm), lambda i: (0, 0, i))),
        compiler_params=_params("parallel"), name="inproj",
    )(h, w_main, w_ft, cosf, sinf, cos_t, sin_t)


N_PAIRS = C_HEADS // 2
FORGET_ROWS = 2


def _split3(x):
    hi = x.astype(BF16)
    r1 = x - hi.astype(F32)
    mid = r1.astype(BF16)
    lo = (r1 - mid.astype(F32)).astype(BF16)
    return hi, mid, lo


def _forget_constants(tri_ref, place_ref):
    tm = tri_ref.shape[0]
    row = lax.broadcasted_iota(jnp.int32, (tm, tm), 0)
    col = lax.broadcasted_iota(jnp.int32, (tm, tm), 1)
    tri_ref[...] = (col <= row).astype(BF16)
    src = lax.broadcasted_iota(jnp.int32, place_ref.shape, 0)
    dst = lax.broadcasted_iota(jnp.int32, place_ref.shape, 1)
    pair, lane = dst // LANES, dst % LANES
    chain, piece = lane // 3, lane % 3
    place = (lane < 2 * 3) & (src == piece * LANES + 2 * pair + chain)
    place_ref[...] = jnp.where(place, -1.0, 0.0).astype(BF16)


def _forget_kernel(h_ref, w_ref, b_ref, f_ref, kbias_ref, tri_ref, place_ref, carry_ref):
    @pl.when((pl.program_id(0) == 0) & (pl.program_id(1) == 0))
    def _():
        _forget_constants(tri_ref, place_ref)

    @pl.when(pl.program_id(1) == 0)
    def _():
        carry_ref[...] = jnp.zeros_like(carry_ref)

    tm = h_ref.shape[1]
    for r in range(h_ref.shape[0]):
        fl = jnp.dot(h_ref[r], w_ref[...], preferred_element_type=F32) + b_ref[...]
        ls = jnp.minimum(fl, 0.0) - jnp.log1p(jnp.exp(-jnp.abs(fl)))
        cs = sum(jnp.dot(tri_ref[...], part, preferred_element_type=F32) for part in _split3(ls))
        f = cs + carry_ref[r]
        carry_ref[r] = f[tm - 1:tm, :]
        f2 = f * LOG2E
        f_ref[r] = f2
        parts = jnp.concatenate(_split3(f2), axis=1)
        placed = jnp.dot(parts, place_ref[...], preferred_element_type=F32)
        for p in range(N_PAIRS):
            kbias_ref[r, p] = placed[:, p * LANES:(p + 1) * LANES].astype(kbias_ref.dtype)


def _forget_cumsum(h3, w_fl, b_fl):
    B, S, _ = h3.shape
    tm = min(TM_DENSE, S)
    rows = FORGET_ROWS if B % FORGET_ROWS == 0 else 1
    return pl.pallas_call(
        _forget_kernel,
        out_shape=(jax.ShapeDtypeStruct((B, S, LANES), F32),
                   jax.ShapeDtypeStruct((B, N_PAIRS, S, LANES), BF16)),
        grid=(B // rows, S // tm),
        in_specs=[pl.BlockSpec((rows, tm, D_MODEL), lambda b, i: (b, i, 0)),
                  _const_spec(w_fl.shape), _const_spec(b_fl.shape)],
        out_specs=(pl.BlockSpec((rows, tm, LANES), lambda b, i: (b, i, 0)),
                   pl.BlockSpec((rows, N_PAIRS, tm, LANES), lambda b, i: (b, 0, i, 0))),
        scratch_shapes=[pltpu.VMEM((tm, tm), BF16), pltpu.VMEM((3 * LANES, N_PAIRS * LANES), BF16),
                        pltpu.VMEM((rows, 1, LANES), F32)],
        compiler_params=_params("arbitrary", "arbitrary"), name="forget_cumsum",
    )(h3, w_fl, b_fl)


def _band_kernel(qt_ref, k_ref, vt_ref, bias_ref, o_ref):
    qi = pl.program_id(2)
    tq = qt_ref.shape[2]
    feat = lax.broadcasted_iota(jnp.int32, (LANES, 1), 0)
    own = [feat < HEAD_DIM, feat >= HEAD_DIM]
    pad = WIN_BAND - SUB_BAND

    def window(r):
        start = qi * tq + (r * SUB_BAND - pad)
        clipped = pl.multiple_of(jnp.maximum(start, 0), SUB_BAND)
        return clipped, pl.multiple_of(clipped - start, SUB_BAND)

    def scores(r, c):
        start, row0 = window(r)
        qt = qt_ref[0, :, r * SUB_BAND:(r + 1) * SUB_BAND]
        qc = jnp.where(own[c], qt, jnp.zeros_like(qt))
        s = jnp.dot(k_ref[0, 0, pl.ds(start, WIN_BAND), :], qc, preferred_element_type=F32)
        s = s + bias_ref[c, pl.ds(row0, WIN_BAND), :]
        return s, jnp.max(s, axis=0, keepdims=True)

    def head_out(r, c, s, m):
        start, _ = window(r)
        p = jnp.exp2((s - m).astype(BF16))
        vt_c = jnp.concatenate([vt_ref[0, c * HEAD_DIM:(c + 1) * HEAD_DIM, pl.ds(start, WIN_BAND)],
                                jnp.ones((SUM_ROWS, WIN_BAND), vt_ref.dtype)], axis=0)
        acc = jnp.dot(vt_c, p, preferred_element_type=F32)
        return acc[:HEAD_DIM] / acc[HEAD_DIM:HEAD_DIM + 1]

    units = [(r, c) for r in range(tq // SUB_BAND) for c in range(2)]
    heads = {}
    pending = None
    for unit in units + [None]:
        current = None if unit is None else (unit, scores(*unit))
        if pending is not None:
            (r, c), (s, m) = pending
            heads[c] = head_out(r, c, s, m)
            if c == 1:
                o_t = jnp.concatenate([heads[0], heads[1]], axis=0)
                o_ref[0, 0, r * SUB_BAND:(r + 1) * SUB_BAND, :] = o_t.T.astype(o_ref.dtype)
        pending = current


def _band_attention(proj, ft, bias, B, S):
    tq = min(TQ_BAND, S)
    nq = S // tq
    return pl.pallas_call(
        _band_kernel,
        out_shape=jax.ShapeDtypeStruct((GROUP_BLOCKS, B, S, LANES), BF16),
        grid=(B, GROUP_BLOCKS, nq),
        in_specs=[pl.BlockSpec((1, LANES, tq), lambda b, p, i: (4 * GROUP_BLOCKS + p, 0, b * nq + i)),
                  pl.BlockSpec((1, 1, S, LANES), lambda b, p, i: (p, b, 0, 0)),
                  pl.BlockSpec((1, LANES, S), lambda b, p, i: (5 * GROUP_BLOCKS + p, 0, b)),
                  pl.BlockSpec((2,) + bias.shape[1:], lambda b, p, i: (p, 0, 0))],
        out_specs=pl.BlockSpec((1, 1, tq, LANES), lambda b, p, i: (p, b, i, 0)),
        compiler_params=_params("parallel", "parallel", "arbitrary"), name="band_attention",
    )(ft, proj, ft, bias)


def _band_bias(rel_bias):
    period = 2 * WIN_BAND
    assert period > WIN_BAND + SUB_BAND - 1
    u = np.arange(period)
    diag = np.where(u < WIN_BAND, u, u - period)
    rel = np.clip(diag - LEFT_CHUNKS * CHUNK, -REL_CLIP, REL_CLIP) + REL_CLIP
    profile = rel_bias.astype(F32)[:, rel] * LOG2E
    heads = profile.shape[0]
    table = jnp.tile(profile, (1, SUB_BAND))[:, :SUB_BAND * (period - 1)]
    table = table.reshape(heads, SUB_BAND, period - 1)[:, :, :WIN_BAND]
    i = np.arange(SUB_BAND)
    k_off = np.arange(WIN_BAND)[None, :] - LEFT_CHUNKS * CHUNK - (i // CHUNK)[:, None] * CHUNK
    valid = (k_off >= -LEFT_CHUNKS * CHUNK) & (k_off < CHUNK)
    table = jnp.where(jnp.asarray(valid)[None], table, NEG_BIG).transpose(0, 2, 1)
    tail = jnp.full((heads, WIN_BAND - SUB_BAND, SUB_BAND), NEG_BIG, F32)
    return jnp.concatenate([table, tail], axis=1)


ROW_T, ROW_ALPHA, ROW_MPREV = 0, 1, 2
FLASH_UNROLL = 4
SUM_ROWS = 16
Q_SPLIT = 2
assert FLASH_UNROLL % 2 == 0


def _flash_body(qt_ref, k_ref, vt_ref, fq_ref, kbias_ref, o_ref, finalize, s_refs, st_refs, mrun_ref, acc_ref, *,
                tq, chunk_causal, split_values):
    tk = tq
    nq = qt_ref.shape[2] // tq
    width = tq // Q_SPLIT
    units = [(c, n) for c in range(2) for n in range(Q_SPLIT)]
    feat = lax.broadcasted_iota(jnp.int32, (LANES, 1), 0)
    own = [feat < HEAD_DIM, feat >= HEAD_DIM]

    def queries(qi):
        q_off = pl.multiple_of(qi * tq, tq)
        qt = qt_ref[0, :, pl.ds(q_off, tq)]
        qs = [jnp.where(own[c], qt, jnp.zeros_like(qt)) for c in range(2)]
        if kbias_ref is not None:
            qs = [jnp.concatenate([qs[c], ((feat >= 3 * c) & (feat < 3 * c + 3)).astype(qt.dtype)
                                   * jnp.ones_like(qt)], axis=0) for c in range(2)]
        shifts = [None if fq_ref is None else fq_ref[0, 0, c:c + 1, pl.ds(q_off, tq)] for c in range(2)]
        cols = lambda x, n: None if x is None else x[:, n * width:(n + 1) * width]
        return [(cols(qs[c], n), cols(shifts[c], n)) for c, n in units]

    def stats(shift, col_max, m_prev):
        if shift is not None:
            col_max = col_max + shift
        m_new = jnp.maximum(m_prev, col_max)
        alpha = jnp.exp2(m_prev - m_new)
        t = m_new if shift is None else m_new - shift
        return m_new, alpha, t

    def keys(j):
        off = pl.multiple_of(j * tk, tk)
        k = k_ref[0, 0, pl.ds(off, tk), :]
        if kbias_ref is not None:
            k = jnp.concatenate([k, kbias_ref[0, 0, pl.ds(off, tk), :]], axis=1)
        return k

    def values(j):
        off = pl.multiple_of(j * tk, tk)
        vt = vt_ref[0, :, pl.ds(off, tk)]
        ones = jnp.ones((SUM_ROWS, tk), vt.dtype)
        if split_values:
            return [jnp.concatenate([vt[c * HEAD_DIM:(c + 1) * HEAD_DIM], ones], axis=0) for c in range(2)]
        return [jnp.concatenate([vt, ones], axis=0)] * 2

    def scores(u, k, slot, query, first=False):
        q_u, shift = query[u]
        s = jnp.dot(k, q_u, preferred_element_type=F32)
        s_refs[slot][u] = s
        m_prev = jnp.full((1, width), NEG_BIG, F32) if first else mrun_ref[u]
        m_new, alpha, t = stats(shift, jnp.max(s, axis=0, keepdims=True), m_prev)
        mrun_ref[u] = m_new
        st_refs[slot][u, ROW_T:ROW_T + 1] = t
        st_refs[slot][u, ROW_ALPHA:ROW_ALPHA + 1] = alpha
        st_refs[slot][u, ROW_MPREV:ROW_MPREV + 1] = m_prev

    def update(u, vts, slot, mask, query):
        s = s_refs[slot][u]
        if mask is None:
            t = st_refs[slot][u, ROW_T:ROW_T + 1]
            alpha = st_refs[slot][u, ROW_ALPHA:ROW_ALPHA + 1]
        else:
            n = units[u][1]
            s = jnp.where(mask[:, n * width:(n + 1) * width], s, NEG_BIG)
            _, alpha, t = stats(query[u][1], jnp.max(s, axis=0, keepdims=True),
                                st_refs[slot][u, ROW_MPREV:ROW_MPREV + 1])
        p = jnp.exp2(s - t).astype(BF16)
        acc_ref[u] = alpha * acc_ref[u] + jnp.dot(vts[units[u][0]], p, preferred_element_type=F32)

    key = lax.broadcasted_iota(jnp.int32, (tk, tq), 0)
    qry = lax.broadcasted_iota(jnp.int32, (tk, tq), 1)
    if chunk_causal:
        mask = (key // CHUNK) <= (qry // CHUNK)
    else:
        mask = key <= qry

    def run(j0, n_full, query):
        for b in range(n_full):
            k, vts = keys(j0 + b + 1), values(j0 + b)
            for u in range(len(units)):
                scores(u, k, (b + 1) % 2, query)
                update(u, vts, b % 2, None, query)

    k0 = keys(0)
    first_query = queries(0)
    for u in range(len(units)):
        scores(u, k0, 0, first_query, first=True)

    def query_block(qi, carry):
        query = queries(qi)
        acc_ref[...] = jnp.zeros(acc_ref.shape, F32)

        def group(jj, inner):
            run(jj * FLASH_UNROLL, FLASH_UNROLL, query)
            return inner

        lax.fori_loop(0, qi // FLASH_UNROLL, group, 0)
        for rem in range(FLASH_UNROLL):
            @pl.when(qi % FLASH_UNROLL == rem)
            def _():
                run(qi - rem, rem, query)
                vts, k_first = values(qi), keys(0)
                next_query = queries(jnp.minimum(qi + 1, nq - 1))
                for u in range(len(units)):
                    update(u, vts, rem % 2, mask, query)
                    scores(u, k_first, 0, next_query, first=True)
                o_ref[0, 0, pl.ds(pl.multiple_of(qi * tq, tq), tq), :] = finalize(acc_ref).astype(o_ref.dtype)
        return carry

    lax.fori_loop(0, nq, query_block, 0)


def _diff_kernel(qt_ref, k_ref, vt_ref, lam_ref, gain_ref, o_ref, s0_ref, s1_ref, st0_ref, st1_ref,
                 mrun_ref, acc_ref, *, tq, lam_init):
    lf = lam_ref[...]
    lam = (jnp.exp(jnp.sum(lf[0:1] * lf[1:2], axis=-1, keepdims=True))
           - jnp.exp(jnp.sum(lf[2:3] * lf[3:4], axis=-1, keepdims=True)) + lam_init)
    gain = gain_ref[...]

    def finalize(acc):
        chain = lambda c: jnp.concatenate([acc[c * Q_SPLIT + n] for n in range(Q_SPLIT)], axis=1)
        maps = [a[:LANES] / a[LANES:LANES + 1] for a in (chain(0), chain(1))]
        o_t = maps[0] - lam * maps[1]
        return _rms(o_t.T, gain) * (1.0 - lam_init)

    _flash_body(qt_ref, k_ref, vt_ref, None, None, o_ref, finalize, (s0_ref, s1_ref), (st0_ref, st1_ref),
                mrun_ref, acc_ref, tq=tq, chunk_causal=True, split_values=False)


def _forget_attn_kernel(qt_ref, k_ref, vt_ref, fq_ref, kbias_ref, o_ref, s0_ref, s1_ref, st0_ref, st1_ref,
                        mrun_ref, acc_ref, *, tq):
    def finalize(acc):
        chain = lambda c: jnp.concatenate([acc[c * Q_SPLIT + n] for n in range(Q_SPLIT)], axis=1)
        heads = [a[:HEAD_DIM] / a[HEAD_DIM:HEAD_DIM + 1] for a in (chain(0), chain(1))]
        return jnp.concatenate(heads, axis=0).T

    _flash_body(qt_ref, k_ref, vt_ref, fq_ref, kbias_ref, o_ref, finalize, (s0_ref, s1_ref), (st0_ref, st1_ref),
                mrun_ref, acc_ref, tq=tq, chunk_causal=False, split_values=True)


def _flash_scratch(tq, acc_rows):
    n_units, width = 2 * Q_SPLIT, tq // Q_SPLIT
    return [pltpu.VMEM((n_units, tq, width), F32), pltpu.VMEM((n_units, tq, width), F32),
            pltpu.VMEM((n_units, 3, width), F32), pltpu.VMEM((n_units, 3, width), F32),
            pltpu.VMEM((n_units, 1, width), F32), pltpu.VMEM((n_units, acc_rows, width), F32)]


def _flash_specs(qt_block, k_block, vt_block, S):
    return [pl.BlockSpec((1, LANES, S), lambda b, p: (qt_block + p, 0, b)),
            pl.BlockSpec((1, 1, S, LANES), lambda b, p: (k_block + p, b, 0, 0)),
            pl.BlockSpec((1, LANES, S), lambda b, p: (vt_block + p, 0, b))]


def _diff_attention(proj, ft, lambdas, sub_gain, lam_init, B, S):
    tq = min(TQ_FLASH, S)
    return pl.pallas_call(
        functools.partial(_diff_kernel, tq=tq, lam_init=lam_init),
        out_shape=jax.ShapeDtypeStruct((GROUP_BLOCKS, B, S, LANES), BF16),
        grid=(B, GROUP_BLOCKS),
        in_specs=_flash_specs(2 * GROUP_BLOCKS, GROUP_BLOCKS, 0, S) + [
            _const_spec(lambdas.shape), _const_spec((1, LANES))],
        out_specs=pl.BlockSpec((1, 1, S, LANES), lambda b, p: (p, b, 0, 0)),
        scratch_shapes=_flash_scratch(tq, LANES + SUM_ROWS),
        compiler_params=_params("parallel", "parallel"), name="diff_attention",
    )(ft, proj, ft, lambdas.astype(F32), sub_gain.astype(F32).reshape(1, LANES))


def _forget_attention(proj, ft, f_row, k_bias, B, S):
    tq = min(TQ_FLASH, S)
    return pl.pallas_call(
        functools.partial(_forget_attn_kernel, tq=tq),
        out_shape=jax.ShapeDtypeStruct((GROUP_BLOCKS, B, S, LANES), BF16),
        grid=(B, GROUP_BLOCKS),
        in_specs=_flash_specs(3 * GROUP_BLOCKS, 2 * GROUP_BLOCKS, GROUP_BLOCKS, S) + [
            pl.BlockSpec((1, 1, 2, S), lambda b, p: (b, p, 0, 0)),
            pl.BlockSpec((1, 1, S, LANES), lambda b, p: (b, p, 0, 0))],
        out_specs=pl.BlockSpec((1, 1, S, LANES), lambda b, p: (p, b, 0, 0)),
        scratch_shapes=_flash_scratch(tq, HEAD_DIM + SUM_ROWS),
        compiler_params=_params("parallel", "parallel"), name="forget_attention",
    )(ft, proj, ft, f_row, k_bias)


def _merge_kernel(x_ref, h_ref, ya_ref, yb_ref, yc_ref, wg_ref, bg_ref, wbr_ref, wo_ref, o_ref):
    h = h_ref[...]
    merged = None
    for b, y_ref in enumerate((ya_ref, yb_ref, yc_ref)):
        y = jnp.concatenate([y_ref[i] for i in range(y_ref.shape[0])], axis=1)
        t = jnp.dot(y, wbr_ref[b], preferred_element_type=F32)
        z = jnp.dot(h, wg_ref[:, b * D_MODEL:(b + 1) * D_MODEL], preferred_element_type=F32)
        g = jax.nn.sigmoid(z + bg_ref[:, b * D_MODEL:(b + 1) * D_MODEL])
        merged = g * t if merged is None else merged + g * t
    o_ref[...] = x_ref[...] + jnp.dot(merged.astype(BF16), wo_ref[...], preferred_element_type=F32)


def _merge(x2, h, ya, yb, yc, w_gate, b_gate, w_br, w_out):
    T = x2.shape[0]
    tm = min(TM_DENSE, T)
    nblk = ya.shape[0]
    row = pl.BlockSpec((tm, D_MODEL), lambda i: (i, 0))
    yspec = pl.BlockSpec((nblk, tm, LANES), lambda i: (0, i, 0))
    return pl.pallas_call(
        _merge_kernel,
        out_shape=jax.ShapeDtypeStruct((T, D_MODEL), F32),
        grid=(T // tm,),
        in_specs=[row, row, yspec, yspec, yspec, _const_spec(w_gate.shape), _const_spec(b_gate.shape),
                  _const_spec(w_br.shape), _const_spec(w_out.shape)],
        out_specs=row,
        compiler_params=_params("parallel"), name="merge_out",
    )(x2, h, ya, yb, yc, w_gate, b_gate, w_br, w_out)


FF_CHUNK = 1024


def _mlp_kernel(x_ref, gn_ref, w1_ref, w2_ref, gnext_ref, *out_refs):
    x = x_ref[...]
    h2 = _rms(x, gn_ref[...]).astype(BF16)
    acc = x
    for c in range(D_FF // FF_CHUNK):
        a = jnp.dot(h2, w1_ref[:, c * FF_CHUNK:(c + 1) * FF_CHUNK], preferred_element_type=F32)
        a = jnp.square(jnp.maximum(a, 0.0))
        acc = acc + jnp.dot(a.astype(BF16), w2_ref[c * FF_CHUNK:(c + 1) * FF_CHUNK, :],
                            preferred_element_type=F32)
    normed = _rms(acc, gnext_ref[...])
    if len(out_refs) == 2:
        out_refs[0][...] = acc
        out_refs[1][...] = normed.astype(out_refs[1].dtype)
    else:
        out_refs[0][...] = normed.astype(out_refs[0].dtype)


def _mlp(x2, g_mlp, w1, w2, g_next, last):
    T = x2.shape[0]
    tm = min(TM_DENSE, T)
    row = pl.BlockSpec((tm, D_MODEL), lambda i: (i, 0))
    if last:
        out_shape, out_specs = jax.ShapeDtypeStruct((T, D_MODEL), F32), row
    else:
        out_shape = (jax.ShapeDtypeStruct((T, D_MODEL), F32), jax.ShapeDtypeStruct((T, D_MODEL), BF16))
        out_specs = (row, row)
    return pl.pallas_call(
        _mlp_kernel, out_shape=out_shape, grid=(T // tm,),
        in_specs=[row, _const_spec((1, D_MODEL)), _const_spec(w1.shape), _const_spec(w2.shape),
                  _const_spec((1, D_MODEL))],
        out_specs=out_specs,
        compiler_params=_params("parallel"), name="mlp",
    )(x2, g_mlp.reshape(1, D_MODEL), w1, w2, g_next.reshape(1, D_MODEL))


def kernel(x, positions, norm_mix, w_in, rel_bias, lambdas, sub_gain, b_forget, w_br_a, w_br_b, w_br_c,
           w_gate, b_gate, w_out, norm_mlp, w_ff1, w_ff2, final_norm):
    B, S, D = x.shape
    T = B * S
    depth = w_in.shape[0]
    n_main = 9 * BRANCH_WIDTH
    q_scale = HEAD_DIM ** -0.5 * LOG2E
    col_scale = jnp.ones((n_main,), F32)
    for grp in (0, 3, 6):
        col_scale = col_scale.at[grp * BRANCH_WIDTH:(grp + 1) * BRANCH_WIDTH].set(q_scale)

    cosf, sinf = _rope_tables(positions)
    cos_t, sin_t = cosf.T, sinf.T
    x2 = x.reshape(T, D)
    h = _rmsnorm(x2, norm_mix[0])
    out = None
    for l in range(depth):
        lam_init = 0.8 - 0.6 * math.exp(-0.3 * l)
        w_all = w_in[l, :, :n_main] * col_scale[None, :]
        group = lambda g: w_all[:, g * BRANCH_WIDTH:(g + 1) * BRANCH_WIDTH]
        w_main = jnp.concatenate([group(g) for g in ROW_GROUPS], axis=1).astype(BF16)
        w_ft = jnp.concatenate([group(g) for g in FEAT_GROUPS], axis=1).T.astype(BF16)
        w_fl = jnp.zeros((D, LANES), F32).at[:, :C_HEADS].set(w_in[l, :, n_main:]).astype(BF16)
        b_fl = jnp.zeros((1, LANES), F32).at[0, :C_HEADS].set(b_forget[l].astype(F32))

        proj, ft = _inproj(h, w_main, w_ft, cosf, sinf, cos_t, sin_t)
        proj = proj.reshape(len(ROW_GROUPS) * GROUP_BLOCKS, B, S, LANES)
        f_tok, k_bias = _forget_cumsum(h.reshape(B, S, D), w_fl, b_fl)
        f_row = f_tok[:, :, :C_HEADS].transpose(0, 2, 1).reshape(B, N_PAIRS, 2, S)

        ya = _band_attention(proj, ft, _band_bias(rel_bias[l]), B, S)
        yb = _diff_attention(proj, ft, lambdas[l], sub_gain[l], lam_init, B, S)
        yc = _forget_attention(proj, ft, f_row, k_bias, B, S)

        w_br = jnp.stack([w_br_a[l], w_br_b[l], w_br_c[l]]).astype(BF16)
        x2 = _merge(x2, h, ya.reshape(GROUP_BLOCKS, T, LANES), yb.reshape(GROUP_BLOCKS, T, LANES),
                    yc.reshape(GROUP_BLOCKS, T, LANES), w_gate[l].astype(BF16),
                    b_gate[l].astype(F32).reshape(1, -1), w_br, w_out[l].astype(BF16))
        last = l == depth - 1
        g_next = final_norm if last else norm_mix[l + 1]
        res = _mlp(x2, norm_mlp[l], w_ff1[l].astype(BF16), w_ff2[l].astype(BF16), g_next, last)
        if last:
            out = res
        else:
            x2, h = res
    return out.reshape(B, S, D)
```

```python
import functools
import math

import jax
import jax.numpy as jnp
import numpy as np
from jax import lax
from jax.experimental import pallas as pl
from jax.experimental.pallas import tpu as pltpu

D_MODEL = 1024
CHUNK = 64
LEFT_CHUNKS = 8
HEAD_DIM = 64
A_HEADS = 8
B_HEADS = 4
C_HEADS = 8
BRANCH_WIDTH = 512
N_BRANCH = 3
D_FF = 4 * D_MODEL
REL_CLIP = 128
ROPE_THETA = 500000.0
ROPE_DIM = HEAD_DIM // 4
ROPE_HALF = ROPE_DIM // 2
EPS = 1e-6

LANES = 128
LOG2E = 1.4426950408889634
NEG_BIG = -1e30

TM_DENSE = 512
TQ_FLASH = 512
VMEM_LIMIT = 56 * 1024 * 1024

BF16 = jnp.bfloat16
F32 = jnp.float32


def _params(*sem):
    return pltpu.CompilerParams(dimension_semantics=sem, vmem_limit_bytes=VMEM_LIMIT)


def _const_spec(shape):
    nd = len(shape)
    return pl.BlockSpec(shape, lambda *_: (0,) * nd, pipeline_mode=pl.Buffered(1))


def _rms(x, g):
    return x * lax.rsqrt(jnp.mean(x * x, axis=-1, keepdims=True) + EPS) * g


def _rope_table_kernel(ang_ref, cos_ref, sin_ref):
    ang = ang_ref[...]
    lane = lax.broadcasted_iota(jnp.int32, ang.shape, 1) % HEAD_DIM
    sign = jnp.where(lane < ROPE_HALF, -1.0, 1.0)
    cos_ref[...] = jnp.cos(ang)
    sin_ref[...] = jnp.sin(ang) * sign


def _rope_tables(positions):
    T = positions.size
    inv_freq = jnp.power(ROPE_THETA, -jnp.arange(0, ROPE_DIM, 2, dtype=F32) / ROPE_DIM)
    lane = jnp.arange(LANES) % HEAD_DIM
    freq_lane = jnp.where(lane < ROPE_DIM, inv_freq[lane % ROPE_HALF], 0.0)
    ang = positions.reshape(T, 1).astype(F32) * freq_lane[None, :]
    tm = min(1024, T)
    spec = pl.BlockSpec((tm, LANES), lambda i: (i, 0))
    return pl.pallas_call(
        _rope_table_kernel,
        out_shape=(jax.ShapeDtypeStruct((T, LANES), F32),) * 2,
        grid=(T // tm,), in_specs=[spec], out_specs=(spec, spec),
        compiler_params=_params("parallel"), name="rope_tables",
    )(ang)


def _rmsnorm_kernel(x_ref, g_ref, h_ref):
    h_ref[...] = _rms(x_ref[...], g_ref[...]).astype(h_ref.dtype)


def _rmsnorm(x2, g):
    T = x2.shape[0]
    tm = min(1024, T)
    return pl.pallas_call(
        _rmsnorm_kernel,
        out_shape=jax.ShapeDtypeStruct((T, D_MODEL), BF16),
        grid=(T // tm,),
        in_specs=[pl.BlockSpec((tm, D_MODEL), lambda i: (i, 0)), _const_spec((1, D_MODEL))],
        out_specs=pl.BlockSpec((tm, D_MODEL), lambda i: (i, 0)),
        compiler_params=_params("parallel"), name="rmsnorm",
    )(x2, g.reshape(1, D_MODEL))


ROW_GROUPS = (1, 4, 7)
FEAT_GROUPS = (5, 8, 3, 6, 0, 2)
ROW_ROPE_GROUP = 1
FEAT_ROPE_GROUP = 2
GROUP_BLOCKS = BRANCH_WIDTH // LANES


def _inproj_kernel(h_ref, w_ref, wft_ref, cos_ref, sin_ref, cos_t_ref, sin_t_ref, o_ref, ft_ref):
    h = h_ref[...]
    cosf = cos_ref[...]
    sinf = sin_ref[...]
    lane = lax.broadcasted_iota(jnp.int32, cosf.shape, 1) % HEAD_DIM
    first_half = lane < ROPE_HALF
    for c in range(len(ROW_GROUPS)):
        res = jnp.dot(h, w_ref[:, c * BRANCH_WIDTH:(c + 1) * BRANCH_WIDTH],
                      preferred_element_type=F32)
        for s in range(GROUP_BLOCKS):
            blk = res[:, s * LANES:(s + 1) * LANES]
            if c == ROW_ROPE_GROUP:
                partner = jnp.where(first_half,
                                    pltpu.roll(blk, LANES - ROPE_HALF, 1),
                                    pltpu.roll(blk, ROPE_HALF, 1))
                blk = blk * cosf + partner * sinf
            o_ref[c * GROUP_BLOCKS + s] = blk.astype(o_ref.dtype)
    for c in range(len(FEAT_GROUPS)):
        res_t = lax.dot_general(wft_ref[c * BRANCH_WIDTH:(c + 1) * BRANCH_WIDTH, :], h,
                                (((1,), (1,)), ((), ())), preferred_element_type=F32)
        for s in range(GROUP_BLOCKS):
            blk = res_t[s * LANES:(s + 1) * LANES, :]
            if c == FEAT_ROPE_GROUP:
                pieces = []
                for base in range(0, LANES, HEAD_DIM):
                    x1 = blk[base:base + ROPE_HALF]
                    x2 = blk[base + ROPE_HALF:base + ROPE_DIM]
                    rot = slice(base, base + ROPE_DIM)
                    pieces.append(jnp.concatenate([x1, x2], axis=0) * cos_t_ref[rot, :]
                                  + jnp.concatenate([x2, x1], axis=0) * sin_t_ref[rot, :])
                    pieces.append(blk[base + ROPE_DIM:base + HEAD_DIM])
                blk = jnp.concatenate(pieces, axis=0)
            ft_ref[c * GROUP_BLOCKS + s] = blk.astype(ft_ref.dtype)


def _inproj(h, w_main, w_ft, cosf, sinf, cos_t, sin_t):
    T = h.shape[0]
    tm = min(TM_DENSE, T)
    n_row = len(ROW_GROUPS) * GROUP_BLOCKS
    n_ft = len(FEAT_GROUPS) * GROUP_BLOCKS
    tok = pl.BlockSpec((tm, LANES), lambda i: (i, 0))
    feat = pl.BlockSpec((LANES, tm), lambda i: (0, i))
    return pl.pallas_call(
        _inproj_kernel,
        out_shape=(jax.ShapeDtypeStruct((n_row, T, LANES), BF16),
                   jax.ShapeDtypeStruct((n_ft, LANES, T), BF16)),
        grid=(T // tm,),
        in_specs=[pl.BlockSpec((tm, D_MODEL), lambda i: (i, 0)),
                  _const_spec(w_main.shape), _const_spec(w_ft.shape), tok, tok, feat, feat],
        out_specs=(pl.BlockSpec((n_row, tm, LANES), lambda i: (0, i, 0)),
                   pl.BlockSpec((n_ft, LANES, tm), lambda i: (0, 0, i))),
        compiler_params=_params("parallel"), name="inproj",
    )(h, w_main, w_ft, cosf, sinf, cos_t, sin_t)


N_PAIRS = C_HEADS // 2
FORGET_ROWS = 2


def _split3(x):
    hi = x.astype(BF16)
    r1 = x - hi.astype(F32)
    mid = r1.astype(BF16)
    lo = (r1 - mid.astype(F32)).astype(BF16)
    return hi, mid, lo


def _forget_constants(tri_ref, place_ref):
    tm = tri_ref.shape[0]
    row = lax.broadcasted_iota(jnp.int32, (tm, tm), 0)
    col = lax.broadcasted_iota(jnp.int32, (tm, tm), 1)
    tri_ref[...] = (col <= row).astype(BF16)
    src = lax.broadcasted_iota(jnp.int32, place_ref.shape, 0)
    dst = lax.broadcasted_iota(jnp.int32, place_ref.shape, 1)
    pair, lane = dst // LANES, dst % LANES
    chain, piece = lane // 3, lane % 3
    place = (lane < 2 * 3) & (src == piece * LANES + 2 * pair + chain)
    place_ref[...] = jnp.where(place, -1.0, 0.0).astype(BF16)


def _forget_kernel(h_ref, w_ref, b_ref, f_ref, kbias_ref, tri_ref, place_ref, carry_ref):
    @pl.when((pl.program_id(0) == 0) & (pl.program_id(1) == 0))
    def _():
        _forget_constants(tri_ref, place_ref)

    @pl.when(pl.program_id(1) == 0)
    def _():
        carry_ref[...] = jnp.zeros_like(carry_ref)

    tm = h_ref.shape[1]
    for r in range(h_ref.shape[0]):
        fl = jnp.dot(h_ref[r], w_ref[...], preferred_element_type=F32) + b_ref[...]
        ls = jnp.minimum(fl, 0.0) - jnp.log1p(jnp.exp(-jnp.abs(fl)))
        cs = sum(jnp.dot(tri_ref[...], part, preferred_element_type=F32) for part in _split3(ls))
        f = cs + carry_ref[r]
        carry_ref[r] = f[tm - 1:tm, :]
        f2 = f * LOG2E
        f_ref[r] = f2
        parts = jnp.concatenate(_split3(f2), axis=1)
        placed = jnp.dot(parts, place_ref[...], preferred_element_type=F32)
        for p in range(N_PAIRS):
            kbias_ref[r, p] = placed[:, p * LANES:(p + 1) * LANES].astype(kbias_ref.dtype)


def _forget_cumsum(h3, w_fl, b_fl):
    B, S, _ = h3.shape
    tm = min(TM_DENSE, S)
    rows = FORGET_ROWS if B % FORGET_ROWS == 0 else 1
    return pl.pallas_call(
        _forget_kernel,
        out_shape=(jax.ShapeDtypeStruct((B, S, LANES), F32),
                   jax.ShapeDtypeStruct((B, N_PAIRS, S, LANES), BF16)),
        grid=(B // rows, S // tm),
        in_specs=[pl.BlockSpec((rows, tm, D_MODEL), lambda b, i: (b, i, 0)),
                  _const_spec(w_fl.shape), _const_spec(b_fl.shape)],
        out_specs=(pl.BlockSpec((rows, tm, LANES), lambda b, i: (b, i, 0)),
                   pl.BlockSpec((rows, N_PAIRS, tm, LANES), lambda b, i: (b, 0, i, 0))),
        scratch_shapes=[pltpu.VMEM((tm, tm), BF16), pltpu.VMEM((3 * LANES, N_PAIRS * LANES), BF16),
                        pltpu.VMEM((rows, 1, LANES), F32)],
        compiler_params=_params("arbitrary", "arbitrary"), name="forget_cumsum",
    )(h3, w_fl, b_fl)


ROW_T, ROW_ALPHA, ROW_MPREV = 0, 1, 2
FLASH_UNROLL = 4
SUM_ROWS = 16
Q_SPLIT = 2
assert FLASH_UNROLL % 2 == 0


def _flash_body(qt_ref, k_ref, vt_ref, fq_ref, kbias_ref, bias_ref, o_ref, finalize, s_refs, st_refs, mrun_ref,
                acc_ref, *, tq, chunk_causal, split_values):
    band = bias_ref is not None

    def key_rows(u, own_block):
        if not band or Q_SPLIT != 2:
            return slice(0, tk)
        first_half = units[u][1] == 0
        if own_block:
            return slice(0, tk // 2) if first_half else slice(0, tk)
        return slice(0, tk) if first_half else slice(tk // 2, tk)
    tk = tq
    nq = qt_ref.shape[2] // tq
    width = tq // Q_SPLIT
    units = [(c, n) for c in range(2) for n in range(Q_SPLIT)]
    feat = lax.broadcasted_iota(jnp.int32, (LANES, 1), 0)
    own = [feat < HEAD_DIM, feat >= HEAD_DIM]

    def queries(qi):
        q_off = pl.multiple_of(qi * tq, tq)
        qt = qt_ref[0, :, pl.ds(q_off, tq)]
        qs = [jnp.where(own[c], qt, jnp.zeros_like(qt)) for c in range(2)]
        if kbias_ref is not None:
            qs = [jnp.concatenate([qs[c], ((feat >= 3 * c) & (feat < 3 * c + 3)).astype(qt.dtype)
                                   * jnp.ones_like(qt)], axis=0) for c in range(2)]
        shifts = [None if fq_ref is None else fq_ref[0, 0, c:c + 1, pl.ds(q_off, tq)] for c in range(2)]
        cols = lambda x, n: None if x is None else x[:, n * width:(n + 1) * width]
        return [(cols(qs[c], n), cols(shifts[c], n)) for c, n in units]

    def stats(shift, col_max, m_prev):
        if shift is not None:
            col_max = col_max + shift
        m_new = jnp.maximum(m_prev, col_max)
        alpha = jnp.exp2(m_prev - m_new)
        t = m_new if shift is None else m_new - shift
        return m_new, alpha, t

    def keys(j):
        off = pl.multiple_of(j * tk, tk)
        k = k_ref[0, 0, pl.ds(off, tk), :]
        if kbias_ref is not None:
            k = jnp.concatenate([k, kbias_ref[0, 0, pl.ds(off, tk), :]], axis=1)
        return k

    def values(j):
        off = pl.multiple_of(j * tk, tk)
        vt = vt_ref[0, :, pl.ds(off, tk)]
        ones = jnp.ones((SUM_ROWS, tk), vt.dtype)
        if split_values:
            return [jnp.concatenate([vt[c * HEAD_DIM:(c + 1) * HEAD_DIM], ones], axis=0) for c in range(2)]
        return [jnp.concatenate([vt, ones], axis=0)] * 2

    def scores(u, k, slot, query, first=False, own_block=True):
        q_u, shift = query[u]
        rows = key_rows(u, own_block)
        s = jnp.dot(k[rows], q_u, preferred_element_type=F32)
        if band:
            c, n = units[u]
            s = s + bias_ref[0, c, int(own_block), rows, n * width:(n + 1) * width]
        s_refs[slot][u, rows] = s
        m_prev = jnp.full((1, width), NEG_BIG, F32) if first else mrun_ref[u]
        m_new, alpha, t = stats(shift, jnp.max(s, axis=0, keepdims=True), m_prev)
        mrun_ref[u] = m_new
        st_refs[slot][u, ROW_T:ROW_T + 1] = t
        st_refs[slot][u, ROW_ALPHA:ROW_ALPHA + 1] = alpha
        st_refs[slot][u, ROW_MPREV:ROW_MPREV + 1] = m_prev

    def update(u, vts, slot, mask, query, own_block=True):
        rows = key_rows(u, own_block)
        s = s_refs[slot][u, rows]
        if mask is None:
            t = st_refs[slot][u, ROW_T:ROW_T + 1]
            alpha = st_refs[slot][u, ROW_ALPHA:ROW_ALPHA + 1]
        else:
            n = units[u][1]
            s = jnp.where(mask[:, n * width:(n + 1) * width], s, NEG_BIG)
            _, alpha, t = stats(query[u][1], jnp.max(s, axis=0, keepdims=True),
                                st_refs[slot][u, ROW_MPREV:ROW_MPREV + 1])
        p = jnp.exp2(s - t).astype(BF16)
        acc_ref[u] = alpha * acc_ref[u] + jnp.dot(vts[units[u][0]][:, rows], p, preferred_element_type=F32)

    key = lax.broadcasted_iota(jnp.int32, (tk, tq), 0)
    qry = lax.broadcasted_iota(jnp.int32, (tk, tq), 1)
    if band:
        mask = None
    elif chunk_causal:
        mask = (key // CHUNK) <= (qry // CHUNK)
    else:
        mask = key <= qry

    def run(j0, n_full, query):
        for b in range(n_full):
            k, vts = keys(j0 + b + 1), values(j0 + b)
            for u in range(len(units)):
                scores(u, k, (b + 1) % 2, query)
                update(u, vts, b % 2, None, query, own_block=not band)

    k0 = keys(0)
    first_query = queries(0)
    for u in range(len(units)):
        scores(u, k0, 0, first_query, first=True)

    def query_block(qi, carry):
        query = queries(qi)
        acc_ref[...] = jnp.zeros(acc_ref.shape, F32)
        if band:
            n_before, n_tails = jnp.minimum(qi, 1), 2
        else:
            def group(jj, inner):
                run(jj * FLASH_UNROLL, FLASH_UNROLL, query)
                return inner

            lax.fori_loop(0, qi // FLASH_UNROLL, group, 0)
            n_before, n_tails = qi % FLASH_UNROLL, FLASH_UNROLL
        for rem in range(n_tails):
            @pl.when(n_before == rem)
            def _():
                run(qi - rem, rem, query)
                nxt = jnp.minimum(qi + 1, nq - 1)
                vts, k_first = values(qi), keys(jnp.maximum(nxt - 1, 0) if band else 0)
                next_query = queries(nxt)
                for u in range(len(units)):
                    update(u, vts, rem % 2, mask, query)
                    scores(u, k_first, 0, next_query, first=True, own_block=False)
                o_ref[0, 0, pl.ds(pl.multiple_of(qi * tq, tq), tq), :] = finalize(acc_ref).astype(o_ref.dtype)
        return carry

    lax.fori_loop(0, nq, query_block, 0)


def _diff_kernel(qt_ref, k_ref, vt_ref, lam_ref, gain_ref, o_ref, s0_ref, s1_ref, st0_ref, st1_ref,
                 mrun_ref, acc_ref, *, tq, lam_init):
    lf = lam_ref[...]
    lam = (jnp.exp(jnp.sum(lf[0:1] * lf[1:2], axis=-1, keepdims=True))
           - jnp.exp(jnp.sum(lf[2:3] * lf[3:4], axis=-1, keepdims=True)) + lam_init)
    gain = gain_ref[...]

    def finalize(acc):
        chain = lambda c: jnp.concatenate([acc[c * Q_SPLIT + n] for n in range(Q_SPLIT)], axis=1)
        maps = [a[:LANES] / a[LANES:LANES + 1] for a in (chain(0), chain(1))]
        o_t = maps[0] - lam * maps[1]
        return _rms(o_t.T, gain) * (1.0 - lam_init)

    _flash_body(qt_ref, k_ref, vt_ref, None, None, None, o_ref, finalize, (s0_ref, s1_ref), (st0_ref, st1_ref),
                mrun_ref, acc_ref, tq=tq, chunk_causal=True, split_values=False)


def _head_pair_output(acc):
    chain = lambda c: jnp.concatenate([acc[c * Q_SPLIT + n] for n in range(Q_SPLIT)], axis=1)
    heads = [a[:HEAD_DIM] / a[HEAD_DIM:HEAD_DIM + 1] for a in (chain(0), chain(1))]
    return jnp.concatenate(heads, axis=0).T


def _forget_attn_kernel(qt_ref, k_ref, vt_ref, fq_ref, kbias_ref, o_ref, s0_ref, s1_ref, st0_ref, st1_ref,
                        mrun_ref, acc_ref, *, tq):
    _flash_body(qt_ref, k_ref, vt_ref, fq_ref, kbias_ref, None, o_ref, _head_pair_output, (s0_ref, s1_ref),
                (st0_ref, st1_ref), mrun_ref, acc_ref, tq=tq, chunk_causal=False, split_values=True)


def _band_kernel(qt_ref, k_ref, vt_ref, bias_ref, o_ref, s0_ref, s1_ref, st0_ref, st1_ref, mrun_ref, acc_ref, *,
                 tq):
    _flash_body(qt_ref, k_ref, vt_ref, None, None, bias_ref, o_ref, _head_pair_output, (s0_ref, s1_ref),
                (st0_ref, st1_ref), mrun_ref, acc_ref, tq=tq, chunk_causal=False, split_values=True)


def _flash_scratch(tq, acc_rows):
    n_units, width = 2 * Q_SPLIT, tq // Q_SPLIT
    return [pltpu.VMEM((n_units, tq, width), F32), pltpu.VMEM((n_units, tq, width), F32),
            pltpu.VMEM((n_units, 3, width), F32), pltpu.VMEM((n_units, 3, width), F32),
            pltpu.VMEM((n_units, 1, width), F32), pltpu.VMEM((n_units, acc_rows, width), F32)]


def _flash_specs(qt_block, k_block, vt_block, S):
    return [pl.BlockSpec((1, LANES, S), lambda b, p: (qt_block + p, 0, b)),
            pl.BlockSpec((1, 1, S, LANES), lambda b, p: (k_block + p, b, 0, 0)),
            pl.BlockSpec((1, LANES, S), lambda b, p: (vt_block + p, 0, b))]


def _diff_attention(proj, ft, lambdas, sub_gain, lam_init, B, S):
    tq = min(TQ_FLASH, S)
    return pl.pallas_call(
        functools.partial(_diff_kernel, tq=tq, lam_init=lam_init),
        out_shape=jax.ShapeDtypeStruct((GROUP_BLOCKS, B, S, LANES), BF16),
        grid=(B, GROUP_BLOCKS),
        in_specs=_flash_specs(2 * GROUP_BLOCKS, GROUP_BLOCKS, 0, S) + [
            _const_spec(lambdas.shape), _const_spec((1, LANES))],
        out_specs=pl.BlockSpec((1, 1, S, LANES), lambda b, p: (p, b, 0, 0)),
        scratch_shapes=_flash_scratch(tq, LANES + SUM_ROWS),
        compiler_params=_params("parallel", "parallel"), name="diff_attention",
    )(ft, proj, ft, lambdas.astype(F32), sub_gain.astype(F32).reshape(1, LANES))


def _forget_attention(proj, ft, f_row, k_bias, B, S):
    tq = min(TQ_FLASH, S)
    return pl.pallas_call(
        functools.partial(_forget_attn_kernel, tq=tq),
        out_shape=jax.ShapeDtypeStruct((GROUP_BLOCKS, B, S, LANES), BF16),
        grid=(B, GROUP_BLOCKS),
        in_specs=_flash_specs(3 * GROUP_BLOCKS, 2 * GROUP_BLOCKS, GROUP_BLOCKS, S) + [
            pl.BlockSpec((1, 1, 2, S), lambda b, p: (b, p, 0, 0)),
            pl.BlockSpec((1, 1, S, LANES), lambda b, p: (b, p, 0, 0))],
        out_specs=pl.BlockSpec((1, 1, S, LANES), lambda b, p: (p, b, 0, 0)),
        scratch_shapes=_flash_scratch(tq, HEAD_DIM + SUM_ROWS),
        compiler_params=_params("parallel", "parallel"), name="forget_attention",
    )(ft, proj, ft, f_row, k_bias)


def _band_attention(proj, ft, bias, B, S):
    tq = min(TQ_FLASH, S)
    return pl.pallas_call(
        functools.partial(_band_kernel, tq=tq),
        out_shape=jax.ShapeDtypeStruct((GROUP_BLOCKS, B, S, LANES), BF16),
        grid=(B, GROUP_BLOCKS),
        in_specs=_flash_specs(4 * GROUP_BLOCKS, 0, 5 * GROUP_BLOCKS, S) + [
            pl.BlockSpec((1,) + bias.shape[1:], lambda b, p: (p, 0, 0, 0, 0))],
        out_specs=pl.BlockSpec((1, 1, S, LANES), lambda b, p: (p, b, 0, 0)),
        scratch_shapes=_flash_scratch(tq, HEAD_DIM + SUM_ROWS),
        compiler_params=_params("parallel", "parallel"), name="band_attention",
    )(ft, proj, ft, bias)


def _band_bias(rel_bias, tq):
    n_keys = 2 * tq
    period = n_keys + tq
    u = np.arange(period)
    diag = np.where(u < n_keys, u, u - period)
    rel = np.clip(diag - tq, -REL_CLIP, REL_CLIP) + REL_CLIP
    profile = rel_bias.astype(F32)[:, rel] * LOG2E
    heads = profile.shape[0]
    rows = min(tq, 256)
    pieces = []
    for first in range(0, tq, rows):
        shifted = jnp.roll(profile, first, axis=1)
        piece = jnp.tile(shifted, (1, rows))[:, :rows * (period - 1)]
        pieces.append(piece.reshape(heads, rows, period - 1)[:, :, :n_keys])
    table = jnp.concatenate(pieces, axis=1)
    q_chunk = tq // CHUNK + np.arange(tq)[:, None] // CHUNK
    k_chunk = np.arange(n_keys)[None, :] // CHUNK
    valid = (k_chunk <= q_chunk) & (k_chunk >= q_chunk - LEFT_CHUNKS)
    table = jnp.where(jnp.asarray(valid)[None], table, NEG_BIG).transpose(0, 2, 1)
    return table.reshape(heads // 2, 2, 2, tq, tq)


def _merge_kernel(x_ref, h_ref, ya_ref, yb_ref, yc_ref, wg_ref, bg_ref, wbr_ref, wo_ref, o_ref):
    h = h_ref[...]
    merged = None
    for b, y_ref in enumerate((ya_ref, yb_ref, yc_ref)):
        y = jnp.concatenate([y_ref[i] for i in range(y_ref.shape[0])], axis=1)
        t = jnp.dot(y, wbr_ref[b], preferred_element_type=F32)
        z = jnp.dot(h, wg_ref[:, b * D_MODEL:(b + 1) * D_MODEL], preferred_element_type=F32)
        g = jax.nn.sigmoid(z + bg_ref[:, b * D_MODEL:(b + 1) * D_MODEL])
        merged = g * t if merged is None else merged + g * t
    o_ref[...] = x_ref[...] + jnp.dot(merged.astype(BF16), wo_ref[...], preferred_element_type=F32)


def _merge(x2, h, ya, yb, yc, w_gate, b_gate, w_br, w_out):
    T = x2.shape[0]
    tm = min(TM_DENSE, T)
    nblk = ya.shape[0]
    row = pl.BlockSpec((tm, D_MODEL), lambda i: (i, 0))
    yspec = pl.BlockSpec((nblk, tm, LANES), lambda i: (0, i, 0))
    return pl.pallas_call(
        _merge_kernel,
        out_shape=jax.ShapeDtypeStruct((T, D_MODEL), F32),
        grid=(T // tm,),
        in_specs=[row, row, yspec, yspec, yspec, _const_spec(w_gate.shape), _const_spec(b_gate.shape),
                  _const_spec(w_br.shape), _const_spec(w_out.shape)],
        out_specs=row,
        compiler_params=_params("parallel"), name="merge_out",
    )(x2, h, ya, yb, yc, w_gate, b_gate, w_br, w_out)


FF_CHUNK = 1024


def _mlp_kernel(x_ref, gn_ref, w1_ref, w2_ref, gnext_ref, *out_refs):
    x = x_ref[...]
    h2 = _rms(x, gn_ref[...]).astype(BF16)
    acc = x
    for c in range(D_FF // FF_CHUNK):
        a = jnp.dot(h2, w1_ref[:, c * FF_CHUNK:(c + 1) * FF_CHUNK], preferred_element_type=F32)
        a = jnp.square(jnp.maximum(a, 0.0))
        acc = acc + jnp.dot(a.astype(BF16), w2_ref[c * FF_CHUNK:(c + 1) * FF_CHUNK, :],
                            preferred_element_type=F32)
    normed = _rms(acc, gnext_ref[...])
    if len(out_refs) == 2:
        out_refs[0][...] = acc
        out_refs[1][...] = normed.astype(out_refs[1].dtype)
    else:
        out_refs[0][...] = normed.astype(out_refs[0].dtype)


def _mlp(x2, g_mlp, w1, w2, g_next, last):
    T = x2.shape[0]
    tm = min(TM_DENSE, T)
    row = pl.BlockSpec((tm, D_MODEL), lambda i: (i, 0))
    if last:
        out_shape, out_specs = jax.ShapeDtypeStruct((T, D_MODEL), F32), row
    else:
        out_shape = (jax.ShapeDtypeStruct((T, D_MODEL), F32), jax.ShapeDtypeStruct((T, D_MODEL), BF16))
        out_specs = (row, row)
    return pl.pallas_call(
        _mlp_kernel, out_shape=out_shape, grid=(T // tm,),
        in_specs=[row, _const_spec((1, D_MODEL)), _const_spec(w1.shape), _const_spec(w2.shape),
                  _const_spec((1, D_MODEL))],
        out_specs=out_specs,
        compiler_params=_params("parallel"), name="mlp",
    )(x2, g_mlp.reshape(1, D_MODEL), w1, w2, g_next.reshape(1, D_MODEL))


def kernel(x, positions, norm_mix, w_in, rel_bias, lambdas, sub_gain, b_forget, w_br_a, w_br_b, w_br_c,
           w_gate, b_gate, w_out, norm_mlp, w_ff1, w_ff2, final_norm):
    B, S, D = x.shape
    T = B * S
    depth = w_in.shape[0]
    n_main = 9 * BRANCH_WIDTH
    q_scale = HEAD_DIM ** -0.5 * LOG2E
    col_scale = jnp.ones((n_main,), F32)
    for grp in (0, 3, 6):
        col_scale = col_scale.at[grp * BRANCH_WIDTH:(grp + 1) * BRANCH_WIDTH].set(q_scale)

    cosf, sinf = _rope_tables(positions)
    cos_t, sin_t = cosf.T, sinf.T
    x2 = x.reshape(T, D)
    h = _rmsnorm(x2, norm_mix[0])
    out = None
    for l in range(depth):
        lam_init = 0.8 - 0.6 * math.exp(-0.3 * l)
        w_all = w_in[l, :, :n_main] * col_scale[None, :]
        group = lambda g: w_all[:, g * BRANCH_WIDTH:(g + 1) * BRANCH_WIDTH]
        w_main = jnp.concatenate([group(g) for g in ROW_GROUPS], axis=1).astype(BF16)
        w_ft = jnp.concatenate([group(g) for g in FEAT_GROUPS], axis=1).T.astype(BF16)
        w_fl = jnp.zeros((D, LANES), F32).at[:, :C_HEADS].set(w_in[l, :, n_main:]).astype(BF16)
        b_fl = jnp.zeros((1, LANES), F32).at[0, :C_HEADS].set(b_forget[l].astype(F32))

        proj, ft = _inproj(h, w_main, w_ft, cosf, sinf, cos_t, sin_t)
        proj = proj.reshape(len(ROW_GROUPS) * GROUP_BLOCKS, B, S, LANES)
        f_tok, k_bias = _forget_cumsum(h.reshape(B, S, D), w_fl, b_fl)
        f_row = f_tok[:, :, :C_HEADS].transpose(0, 2, 1).reshape(B, N_PAIRS, 2, S)

        ya = _band_attention(proj, ft, _band_bias(rel_bias[l], min(TQ_FLASH, S)), B, S)
        yb = _diff_attention(proj, ft, lambdas[l], sub_gain[l], lam_init, B, S)
        yc = _forget_attention(proj, ft, f_row, k_bias, B, S)

        w_br = jnp.stack([w_br_a[l], w_br_b[l], w_br_c[l]]).astype(BF16)
        x2 = _merge(x2, h, ya.reshape(GROUP_BLOCKS, T, LANES), yb.reshape(GROUP_BLOCKS, T, LANES),
                    yc.reshape(GROUP_BLOCKS, T, LANES), w_gate[l].astype(BF16),
                    b_gate[l].astype(F32).reshape(1, -1), w_br, w_out[l].astype(BF16))
        last = l == depth - 1
        g_next = final_norm if last else norm_mix[l + 1]
        res = _mlp(x2, norm_mlp[l], w_ff1[l].astype(BF16), w_ff2[l].astype(BF16), g_next, last)
        if last:
            out = res
        else:
            x2, h = res
    return out.reshape(B, S, D)
```

```python
import functools
import math

import jax
import jax.numpy as jnp
import numpy as np
from jax import lax
from jax.experimental import pallas as pl
from jax.experimental.pallas import tpu as pltpu

D_MODEL = 1024
CHUNK = 64
LEFT_CHUNKS = 8
HEAD_DIM = 64
A_HEADS = 8
B_HEADS = 4
C_HEADS = 8
BRANCH_WIDTH = 512
N_BRANCH = 3
D_FF = 4 * D_MODEL
REL_CLIP = 128
ROPE_THETA = 500000.0
ROPE_DIM = HEAD_DIM // 4
ROPE_HALF = ROPE_DIM // 2
EPS = 1e-6

LANES = 128
LOG2E = 1.4426950408889634
NEG_BIG = -1e30

TM_DENSE = 512
TQ_FLASH = 512
VMEM_LIMIT = 56 * 1024 * 1024

BF16 = jnp.bfloat16
F32 = jnp.float32


def _params(*sem):
    return pltpu.CompilerParams(dimension_semantics=sem, vmem_limit_bytes=VMEM_LIMIT)


def _const_spec(shape):
    nd = len(shape)
    return pl.BlockSpec(shape, lambda *_: (0,) * nd, pipeline_mode=pl.Buffered(1))


def _rms(x, g):
    return x * lax.rsqrt(jnp.mean(x * x, axis=-1, keepdims=True) + EPS) * g


def _rope_table_kernel(ang_ref, cos_ref, sin_ref):
    ang = ang_ref[...]
    row = lax.broadcasted_iota(jnp.int32, ang.shape, 0)
    sign = jnp.where(row < ROPE_HALF, -1.0, 1.0)
    cos_ref[...] = jnp.cos(ang)
    sin_ref[...] = jnp.sin(ang) * sign


def _rope_tables(positions):
    T = positions.size
    inv_freq = jnp.power(ROPE_THETA, -jnp.arange(0, ROPE_DIM, 2, dtype=F32) / ROPE_DIM)
    ang = jnp.tile(inv_freq, 2)[:, None] * positions.reshape(1, T).astype(F32)
    tt = min(8192, T)
    spec = pl.BlockSpec((ROPE_DIM, tt), lambda i: (0, i))
    cos_t, sin_t = pl.pallas_call(
        _rope_table_kernel,
        out_shape=(jax.ShapeDtypeStruct((ROPE_DIM, T), F32),) * 2,
        grid=(T // tt,), in_specs=[spec], out_specs=(spec, spec),
        compiler_params=_params("parallel"), name="rope_tables",
    )(ang)
    rest = HEAD_DIM - ROPE_DIM
    lanes = lambda t, fill: jnp.concatenate([t.T, jnp.full((T, rest), fill, F32)] * (LANES // HEAD_DIM), axis=1)
    return lanes(cos_t, 1.0), lanes(sin_t, 0.0), cos_t, sin_t


def _rmsnorm_kernel(x_ref, g_ref, h_ref):
    h_ref[...] = _rms(x_ref[...], g_ref[...]).astype(h_ref.dtype)


def _rmsnorm(x2, g):
    T = x2.shape[0]
    tm = min(1024, T)
    return pl.pallas_call(
        _rmsnorm_kernel,
        out_shape=jax.ShapeDtypeStruct((T, D_MODEL), BF16),
        grid=(T // tm,),
        in_specs=[pl.BlockSpec((tm, D_MODEL), lambda i: (i, 0)), _const_spec((1, D_MODEL))],
        out_specs=pl.BlockSpec((tm, D_MODEL), lambda i: (i, 0)),
        compiler_params=_params("parallel"), name="rmsnorm",
    )(x2, g.reshape(1, D_MODEL))


ROW_GROUPS = (1, 4, 7)
FEAT_GROUPS = (5, 8, 3, 6, 0, 2)
ROW_ROPE_GROUP = 1
FEAT_ROPE_GROUP = 2
GROUP_BLOCKS = BRANCH_WIDTH // LANES


def _inproj_kernel(h_ref, w_ref, wft_ref, cos_ref, sin_ref, cos_t_ref, sin_t_ref, o_ref, ft_ref):
    h = h_ref[...]
    cosf = cos_ref[...]
    sinf = sin_ref[...]
    lane = lax.broadcasted_iota(jnp.int32, cosf.shape, 1) % HEAD_DIM
    first_half = lane < ROPE_HALF
    for c in range(len(ROW_GROUPS)):
        res = jnp.dot(h, w_ref[:, c * BRANCH_WIDTH:(c + 1) * BRANCH_WIDTH],
                      preferred_element_type=F32)
        for s in range(GROUP_BLOCKS):
            blk = res[:, s * LANES:(s + 1) * LANES]
            if c == ROW_ROPE_GROUP:
                partner = jnp.where(first_half,
                                    pltpu.roll(blk, LANES - ROPE_HALF, 1),
                                    pltpu.roll(blk, ROPE_HALF, 1))
                blk = blk * cosf + partner * sinf
            o_ref[c * GROUP_BLOCKS + s] = blk.astype(o_ref.dtype)
    for c in range(len(FEAT_GROUPS)):
        res_t = lax.dot_general(wft_ref[c * BRANCH_WIDTH:(c + 1) * BRANCH_WIDTH, :], h,
                                (((1,), (1,)), ((), ())), preferred_element_type=F32)
        for s in range(GROUP_BLOCKS):
            blk = res_t[s * LANES:(s + 1) * LANES, :]
            if c == FEAT_ROPE_GROUP:
                pieces = []
                for base in range(0, LANES, HEAD_DIM):
                    x1 = blk[base:base + ROPE_HALF]
                    x2 = blk[base + ROPE_HALF:base + ROPE_DIM]
                    pieces.append(jnp.concatenate([x1, x2], axis=0) * cos_t_ref[...]
                                  + jnp.concatenate([x2, x1], axis=0) * sin_t_ref[...])
                    pieces.append(blk[base + ROPE_DIM:base + HEAD_DIM])
                blk = jnp.concatenate(pieces, axis=0)
            ft_ref[c * GROUP_BLOCKS + s] = blk.astype(ft_ref.dtype)


def _inproj(h, w_main, w_ft, cosf, sinf, cos_t, sin_t):
    T = h.shape[0]
    tm = min(TM_DENSE, T)
    n_row = len(ROW_GROUPS) * GROUP_BLOCKS
    n_ft = len(FEAT_GROUPS) * GROUP_BLOCKS
    tok = pl.BlockSpec((tm, LANES), lambda i: (i, 0))
    feat = pl.BlockSpec((ROPE_DIM, tm), lambda i: (0, i))
    return pl.pallas_call(
        _inproj_kernel,
        out_shape=(jax.ShapeDtypeStruct((n_row, T, LANES), BF16),
                   jax.ShapeDtypeStruct((n_ft, LANES, T), BF16)),
        grid=(T // tm,),
        in_specs=[pl.BlockSpec((tm, D_MODEL), lambda i: (i, 0)),
                  _const_spec(w_main.shape), _const_spec(w_ft.shape), tok, tok, feat, feat],
        out_specs=(pl.BlockSpec((n_row, tm, LANES), lambda i: (0, i, 0)),
                   pl.BlockSpec((n_ft, LANES, tm), lambda i: (0, 0, i))),
        compiler_params=_params("parallel"), name="inproj",
    )(h, w_main, w_ft, cosf, sinf, cos_t, sin_t)


N_PAIRS = C_HEADS // 2
FORGET_ROWS = 2


def _split3(x):
    hi = x.astype(BF16)
    r1 = x - hi.astype(F32)
    mid = r1.astype(BF16)
    lo = (r1 - mid.astype(F32)).astype(BF16)
    return hi, mid, lo


def _forget_constants(tri_ref, place_ref):
    tm = tri_ref.shape[0]
    row = lax.broadcasted_iota(jnp.int32, (tm, tm), 0)
    col = lax.broadcasted_iota(jnp.int32, (tm, tm), 1)
    tri_ref[...] = (col <= row).astype(BF16)
    src = lax.broadcasted_iota(jnp.int32, place_ref.shape, 0)
    dst = lax.broadcasted_iota(jnp.int32, place_ref.shape, 1)
    pair, lane = dst // LANES, dst % LANES
    chain, piece = lane // 3, lane % 3
    place = (lane < 2 * 3) & (src == piece * LANES + 2 * pair + chain)
    place_ref[...] = jnp.where(place, -1.0, 0.0).astype(BF16)


def _forget_kernel(h_ref, w_ref, b_ref, f_ref, kbias_ref, tri_ref, place_ref, carry_ref):
    @pl.when((pl.program_id(0) == 0) & (pl.program_id(1) == 0))
    def _():
        _forget_constants(tri_ref, place_ref)

    @pl.when(pl.program_id(1) == 0)
    def _():
        carry_ref[...] = jnp.zeros_like(carry_ref)

    tm = h_ref.shape[1]
    for r in range(h_ref.shape[0]):
        fl = jnp.dot(h_ref[r], w_ref[...], preferred_element_type=F32) + b_ref[...]
        ls = jnp.minimum(fl, 0.0) - jnp.log1p(jnp.exp(-jnp.abs(fl)))
        cs = sum(jnp.dot(tri_ref[...], part, preferred_element_type=F32) for part in _split3(ls))
        f = cs + carry_ref[r]
        carry_ref[r] = f[tm - 1:tm, :]
        f2 = f * LOG2E
        f_ref[r] = f2
        parts = jnp.concatenate(_split3(f2), axis=1)
        placed = jnp.dot(parts, place_ref[...], preferred_element_type=F32)
        for p in range(N_PAIRS):
            kbias_ref[r, p] = placed[:, p * LANES:(p + 1) * LANES].astype(kbias_ref.dtype)


def _forget_cumsum(h3, w_fl, b_fl):
    B, S, _ = h3.shape
    tm = min(TM_DENSE, S)
    rows = FORGET_ROWS if B % FORGET_ROWS == 0 else 1
    return pl.pallas_call(
        _forget_kernel,
        out_shape=(jax.ShapeDtypeStruct((B, S, LANES), F32),
                   jax.ShapeDtypeStruct((B, N_PAIRS, S, LANES), BF16)),
        grid=(B // rows, S // tm),
        in_specs=[pl.BlockSpec((rows, tm, D_MODEL), lambda b, i: (b, i, 0)),
                  _const_spec(w_fl.shape), _const_spec(b_fl.shape)],
        out_specs=(pl.BlockSpec((rows, tm, LANES), lambda b, i: (b, i, 0)),
                   pl.BlockSpec((rows, N_PAIRS, tm, LANES), lambda b, i: (b, 0, i, 0))),
        scratch_shapes=[pltpu.VMEM((tm, tm), BF16), pltpu.VMEM((3 * LANES, N_PAIRS * LANES), BF16),
                        pltpu.VMEM((rows, 1, LANES), F32)],
        compiler_params=_params("arbitrary", "arbitrary"), name="forget_cumsum",
    )(h3, w_fl, b_fl)


ROW_T, ROW_ALPHA, ROW_MPREV = 0, 1, 2
FLASH_UNROLL = 6
SUM_ROWS = 16
Q_SPLIT = 2
assert FLASH_UNROLL % 2 == 0


def _flash_body(qt_ref, k_ref, vt_ref, fq_ref, kbias_ref, bias_ref, o_ref, finalize, s_refs, st_refs, mrun_ref,
                acc_ref, *, tq, chunk_causal, split_values):
    band = bias_ref is not None

    def key_rows(u, own_block):
        if not band or Q_SPLIT != 2:
            return slice(0, tk)
        first_half = units[u][1] == 0
        if own_block:
            return slice(0, tk // 2) if first_half else slice(0, tk)
        return slice(0, tk) if first_half else slice(tk // 2, tk)
    tk = tq
    nq = qt_ref.shape[2] // tq
    width = tq // Q_SPLIT
    units = [(c, n) for c in range(2) for n in range(Q_SPLIT)]
    feat = lax.broadcasted_iota(jnp.int32, (LANES, 1), 0)
    own = [feat < HEAD_DIM, feat >= HEAD_DIM]

    def queries(qi):
        q_off = pl.multiple_of(qi * tq, tq)
        qt = qt_ref[0, :, pl.ds(q_off, tq)]
        qs = [jnp.where(own[c], qt, jnp.zeros_like(qt)) for c in range(2)]
        if kbias_ref is not None:
            qs = [jnp.concatenate([qs[c], ((feat >= 3 * c) & (feat < 3 * c + 3)).astype(qt.dtype)
                                   * jnp.ones_like(qt)], axis=0) for c in range(2)]
        shifts = [None if fq_ref is None else fq_ref[0, 0, c:c + 1, pl.ds(q_off, tq)] for c in range(2)]
        cols = lambda x, n: None if x is None else x[:, n * width:(n + 1) * width]
        return [(cols(qs[c], n), cols(shifts[c], n)) for c, n in units]

    def stats(shift, col_max, m_prev):
        if shift is not None:
            col_max = col_max + shift
        m_new = jnp.maximum(m_prev, col_max)
        alpha = jnp.exp2(m_prev - m_new)
        t = m_new if shift is None else m_new - shift
        return m_new, alpha, t

    def keys(j):
        off = pl.multiple_of(j * tk, tk)
        k = k_ref[0, 0, pl.ds(off, tk), :]
        if kbias_ref is not None:
            k = jnp.concatenate([k, kbias_ref[0, 0, pl.ds(off, tk), :]], axis=1)
        return k

    def values(j):
        off = pl.multiple_of(j * tk, tk)
        vt = vt_ref[0, :, pl.ds(off, tk)]
        ones = jnp.ones((SUM_ROWS, tk), vt.dtype)
        if split_values:
            return [jnp.concatenate([vt[c * HEAD_DIM:(c + 1) * HEAD_DIM], ones], axis=0) for c in range(2)]
        return [jnp.concatenate([vt, ones], axis=0)] * 2

    def scores(u, k, slot, query, first=False, own_block=True):
        q_u, shift = query[u]
        rows = key_rows(u, own_block)
        s = jnp.dot(k[rows], q_u, preferred_element_type=F32)
        if band:
            c, n = units[u]
            s = s + bias_ref[0, c, int(own_block), rows, n * width:(n + 1) * width]
        s_refs[slot][u, rows] = s
        m_prev = jnp.full((1, width), NEG_BIG, F32) if first else mrun_ref[u]
        m_new, alpha, t = stats(shift, jnp.max(s, axis=0, keepdims=True), m_prev)
        mrun_ref[u] = m_new
        st_refs[slot][u, ROW_T:ROW_T + 1] = t
        st_refs[slot][u, ROW_ALPHA:ROW_ALPHA + 1] = alpha
        st_refs[slot][u, ROW_MPREV:ROW_MPREV + 1] = m_prev

    def update(u, vts, slot, mask, query, own_block=True):
        rows = key_rows(u, own_block)
        s = s_refs[slot][u, rows]
        if mask is None:
            t = st_refs[slot][u, ROW_T:ROW_T + 1]
            alpha = st_refs[slot][u, ROW_ALPHA:ROW_ALPHA + 1]
        else:
            n = units[u][1]
            s = jnp.where(mask[:, n * width:(n + 1) * width], s, NEG_BIG)
            _, alpha, t = stats(query[u][1], jnp.max(s, axis=0, keepdims=True),
                                st_refs[slot][u, ROW_MPREV:ROW_MPREV + 1])
        p = jnp.exp2(s - t).astype(BF16)
        acc_ref[u] = alpha * acc_ref[u] + jnp.dot(vts[units[u][0]][:, rows], p, preferred_element_type=F32)

    key = lax.broadcasted_iota(jnp.int32, (tk, tq), 0)
    qry = lax.broadcasted_iota(jnp.int32, (tk, tq), 1)
    if band:
        mask = None
    elif chunk_causal:
        mask = (key // CHUNK) <= (qry // CHUNK)
    else:
        mask = key <= qry

    def run(j0, n_full, query):
        for b in range(n_full):
            k, vts = keys(j0 + b + 1), values(j0 + b)
            for u in range(len(units)):
                scores(u, k, (b + 1) % 2, query)
                update(u, vts, b % 2, None, query, own_block=not band)

    k0 = keys(0)
    first_query = queries(0)
    for u in range(len(units)):
        scores(u, k0, 0, first_query, first=True)

    def query_block(qi, carry):
        query = queries(qi)
        acc_ref[...] = jnp.zeros(acc_ref.shape, F32)
        if band:
            n_before, n_tails = jnp.minimum(qi, 1), 2
        else:
            def group(jj, inner):
                run(jj * FLASH_UNROLL, FLASH_UNROLL, query)
                return inner

            lax.fori_loop(0, qi // FLASH_UNROLL, group, 0)
            n_before, n_tails = qi % FLASH_UNROLL, FLASH_UNROLL
        for rem in range(n_tails):
            @pl.when(n_before == rem)
            def _():
                run(qi - rem, rem, query)
                nxt = jnp.minimum(qi + 1, nq - 1)
                vts, k_first = values(qi), keys(jnp.maximum(nxt - 1, 0) if band else 0)
                next_query = queries(nxt)
                for u in range(len(units)):
                    update(u, vts, rem % 2, mask, query)
                    scores(u, k_first, 0, next_query, first=True, own_block=False)
                o_ref[0, 0, pl.ds(pl.multiple_of(qi * tq, tq), tq), :] = finalize(acc_ref).astype(o_ref.dtype)
        return carry

    lax.fori_loop(0, nq, query_block, 0)


def _diff_kernel(qt_ref, k_ref, vt_ref, lam_ref, gain_ref, o_ref, s0_ref, s1_ref, st0_ref, st1_ref,
                 mrun_ref, acc_ref, *, tq, lam_init):
    lf = lam_ref[...]
    lam = (jnp.exp(jnp.sum(lf[0:1] * lf[1:2], axis=-1, keepdims=True))
           - jnp.exp(jnp.sum(lf[2:3] * lf[3:4], axis=-1, keepdims=True)) + lam_init)
    gain = gain_ref[...]

    def finalize(acc):
        chain = lambda c: jnp.concatenate([acc[c * Q_SPLIT + n] for n in range(Q_SPLIT)], axis=1)
        maps = [a[:LANES] / a[LANES:LANES + 1] for a in (chain(0), chain(1))]
        o_t = maps[0] - lam * maps[1]
        return _rms(o_t.T, gain) * (1.0 - lam_init)

    _flash_body(qt_ref, k_ref, vt_ref, None, None, None, o_ref, finalize, (s0_ref, s1_ref), (st0_ref, st1_ref),
                mrun_ref, acc_ref, tq=tq, chunk_causal=True, split_values=False)


def _head_pair_output(acc):
    chain = lambda c: jnp.concatenate([acc[c * Q_SPLIT + n] for n in range(Q_SPLIT)], axis=1)
    heads = [a[:HEAD_DIM] / a[HEAD_DIM:HEAD_DIM + 1] for a in (chain(0), chain(1))]
    return jnp.concatenate(heads, axis=0).T


def _forget_attn_kernel(qt_ref, k_ref, vt_ref, fq_ref, kbias_ref, o_ref, s0_ref, s1_ref, st0_ref, st1_ref,
                        mrun_ref, acc_ref, *, tq):
    _flash_body(qt_ref, k_ref, vt_ref, fq_ref, kbias_ref, None, o_ref, _head_pair_output, (s0_ref, s1_ref),
                (st0_ref, st1_ref), mrun_ref, acc_ref, tq=tq, chunk_causal=False, split_values=True)


def _band_kernel(qt_ref, k_ref, vt_ref, bias_ref, o_ref, s0_ref, s1_ref, st0_ref, st1_ref, mrun_ref, acc_ref, *,
                 tq):
    _flash_body(qt_ref, k_ref, vt_ref, None, None, bias_ref, o_ref, _head_pair_output, (s0_ref, s1_ref),
                (st0_ref, st1_ref), mrun_ref, acc_ref, tq=tq, chunk_causal=False, split_values=True)


def _flash_scratch(tq, acc_rows):
    n_units, width = 2 * Q_SPLIT, tq // Q_SPLIT
    return [pltpu.VMEM((n_units, tq, width), F32), pltpu.VMEM((n_units, tq, width), F32),
            pltpu.VMEM((n_units, 3, width), F32), pltpu.VMEM((n_units, 3, width), F32),
            pltpu.VMEM((n_units, 1, width), F32), pltpu.VMEM((n_units, acc_rows, width), F32)]


def _flash_specs(qt_block, k_block, vt_block, S):
    return [pl.BlockSpec((1, LANES, S), lambda b, p: (qt_block + p, 0, b)),
            pl.BlockSpec((1, 1, S, LANES), lambda b, p: (k_block + p, b, 0, 0)),
            pl.BlockSpec((1, LANES, S), lambda b, p: (vt_block + p, 0, b))]


def _diff_attention(proj, ft, lambdas, sub_gain, lam_init, B, S):
    tq = min(TQ_FLASH, S)
    return pl.pallas_call(
        functools.partial(_diff_kernel, tq=tq, lam_init=lam_init),
        out_shape=jax.ShapeDtypeStruct((GROUP_BLOCKS, B, S, LANES), BF16),
        grid=(B, GROUP_BLOCKS),
        in_specs=_flash_specs(2 * GROUP_BLOCKS, GROUP_BLOCKS, 0, S) + [
            _const_spec(lambdas.shape), _const_spec((1, LANES))],
        out_specs=pl.BlockSpec((1, 1, S, LANES), lambda b, p: (p, b, 0, 0)),
        scratch_shapes=_flash_scratch(tq, LANES + SUM_ROWS),
        compiler_params=_params("parallel", "parallel"), name="diff_attention",
    )(ft, proj, ft, lambdas.astype(F32), sub_gain.astype(F32).reshape(1, LANES))


def _forget_attention(proj, ft, f_row, k_bias, B, S):
    tq = min(TQ_FLASH, S)
    return pl.pallas_call(
        functools.partial(_forget_attn_kernel, tq=tq),
        out_shape=jax.ShapeDtypeStruct((GROUP_BLOCKS, B, S, LANES), BF16),
        grid=(B, GROUP_BLOCKS),
        in_specs=_flash_specs(3 * GROUP_BLOCKS, 2 * GROUP_BLOCKS, GROUP_BLOCKS, S) + [
            pl.BlockSpec((1, 1, 2, S), lambda b, p: (b, p, 0, 0)),
            pl.BlockSpec((1, 1, S, LANES), lambda b, p: (b, p, 0, 0))],
        out_specs=pl.BlockSpec((1, 1, S, LANES), lambda b, p: (p, b, 0, 0)),
        scratch_shapes=_flash_scratch(tq, HEAD_DIM + SUM_ROWS),
        compiler_params=_params("parallel", "parallel"), name="forget_attention",
    )(ft, proj, ft, f_row, k_bias)


def _band_attention(proj, ft, bias, B, S):
    tq = min(TQ_FLASH, S)
    return pl.pallas_call(
        functools.partial(_band_kernel, tq=tq),
        out_shape=jax.ShapeDtypeStruct((GROUP_BLOCKS, B, S, LANES), BF16),
        grid=(B, GROUP_BLOCKS),
        in_specs=_flash_specs(4 * GROUP_BLOCKS, 0, 5 * GROUP_BLOCKS, S) + [
            pl.BlockSpec((1,) + bias.shape[1:], lambda b, p: (p, 0, 0, 0, 0))],
        out_specs=pl.BlockSpec((1, 1, S, LANES), lambda b, p: (p, b, 0, 0)),
        scratch_shapes=_flash_scratch(tq, HEAD_DIM + SUM_ROWS),
        compiler_params=_params("parallel", "parallel"), name="band_attention",
    )(ft, proj, ft, bias)


def _band_bias(rel_bias, tq):
    n_keys = 2 * tq
    period = n_keys + tq
    u = np.arange(period)
    diag = np.where(u < n_keys, u, u - period)
    rel = np.clip(diag - tq, -REL_CLIP, REL_CLIP) + REL_CLIP
    profile = rel_bias.astype(F32)[:, rel] * LOG2E
    heads = profile.shape[0]
    rows = min(tq, 256)
    pieces = []
    for first in range(0, tq, rows):
        shifted = jnp.roll(profile, first, axis=1)
        piece = jnp.tile(shifted, (1, rows))[:, :rows * (period - 1)]
        pieces.append(piece.reshape(heads, rows, period - 1)[:, :, :n_keys])
    table = jnp.concatenate(pieces, axis=1)
    q_chunk = tq // CHUNK + np.arange(tq)[:, None] // CHUNK
    k_chunk = np.arange(n_keys)[None, :] // CHUNK
    valid = (k_chunk <= q_chunk) & (k_chunk >= q_chunk - LEFT_CHUNKS)
    table = jnp.where(jnp.asarray(valid)[None], table, NEG_BIG).transpose(0, 2, 1)
    return table.reshape(heads // 2, 2, 2, tq, tq)


def _merge_kernel(x_ref, h_ref, ya_ref, yb_ref, yc_ref, wg_ref, bg_ref, wbr_ref, wo_ref, o_ref):
    h = h_ref[...]
    merged = None
    for b, y_ref in enumerate((ya_ref, yb_ref, yc_ref)):
        y = jnp.concatenate([y_ref[i] for i in range(y_ref.shape[0])], axis=1)
        t = jnp.dot(y, wbr_ref[b], preferred_element_type=F32)
        z = jnp.dot(h, wg_ref[:, b * D_MODEL:(b + 1) * D_MODEL], preferred_element_type=F32)
        g = jax.nn.sigmoid(z + bg_ref[:, b * D_MODEL:(b + 1) * D_MODEL])
        merged = g * t if merged is None else merged + g * t
    o_ref[...] = x_ref[...] + jnp.dot(merged.astype(BF16), wo_ref[...], preferred_element_type=F32)


def _merge(x2, h, ya, yb, yc, w_gate, b_gate, w_br, w_out):
    T = x2.shape[0]
    tm = min(TM_DENSE, T)
    nblk = ya.shape[0]
    row = pl.BlockSpec((tm, D_MODEL), lambda i: (i, 0))
    yspec = pl.BlockSpec((nblk, tm, LANES), lambda i: (0, i, 0))
    return pl.pallas_call(
        _merge_kernel,
        out_shape=jax.ShapeDtypeStruct((T, D_MODEL), F32),
        grid=(T // tm,),
        in_specs=[row, row, yspec, yspec, yspec, _const_spec(w_gate.shape), _const_spec(b_gate.shape),
                  _const_spec(w_br.shape), _const_spec(w_out.shape)],
        out_specs=row,
        compiler_params=_params("parallel"), name="merge_out",
    )(x2, h, ya, yb, yc, w_gate, b_gate, w_br, w_out)


FF_CHUNK = 1024


def _mlp_kernel(x_ref, gn_ref, w1_ref, w2_ref, gnext_ref, *out_refs):
    x = x_ref[...]
    h2 = _rms(x, gn_ref[...]).astype(BF16)
    acc = x
    for c in range(D_FF // FF_CHUNK):
        a = jnp.dot(h2, w1_ref[:, c * FF_CHUNK:(c + 1) * FF_CHUNK], preferred_element_type=F32)
        a = jnp.square(jnp.maximum(a, 0.0))
        acc = acc + jnp.dot(a.astype(BF16), w2_ref[c * FF_CHUNK:(c + 1) * FF_CHUNK, :],
                            preferred_element_type=F32)
    normed = _rms(acc, gnext_ref[...])
    if len(out_refs) == 2:
        out_refs[0][...] = acc
        out_refs[1][...] = normed.astype(out_refs[1].dtype)
    else:
        out_refs[0][...] = normed.astype(out_refs[0].dtype)


def _mlp(x2, g_mlp, w1, w2, g_next, last):
    T = x2.shape[0]
    tm = min(TM_DENSE, T)
    row = pl.BlockSpec((tm, D_MODEL), lambda i: (i, 0))
    if last:
        out_shape, out_specs = jax.ShapeDtypeStruct((T, D_MODEL), F32), row
    else:
        out_shape = (jax.ShapeDtypeStruct((T, D_MODEL), F32), jax.ShapeDtypeStruct((T, D_MODEL), BF16))
        out_specs = (row, row)
    return pl.pallas_call(
        _mlp_kernel, out_shape=out_shape, grid=(T // tm,),
        in_specs=[row, _const_spec((1, D_MODEL)), _const_spec(w1.shape), _const_spec(w2.shape),
                  _const_spec((1, D_MODEL))],
        out_specs=out_specs,
        compiler_params=_params("parallel"), name="mlp",
    )(x2, g_mlp.reshape(1, D_MODEL), w1, w2, g_next.reshape(1, D_MODEL))


def kernel(x, positions, norm_mix, w_in, rel_bias, lambdas, sub_gain, b_forget, w_br_a, w_br_b, w_br_c,
           w_gate, b_gate, w_out, norm_mlp, w_ff1, w_ff2, final_norm):
    B, S, D = x.shape
    T = B * S
    depth = w_in.shape[0]
    n_main = 9 * BRANCH_WIDTH
    q_scale = HEAD_DIM ** -0.5 * LOG2E
    col_scale = jnp.ones((n_main,), F32)
    for grp in (0, 3, 6):
        col_scale = col_scale.at[grp * BRANCH_WIDTH:(grp + 1) * BRANCH_WIDTH].set(q_scale)

    cosf, sinf, cos_t, sin_t = _rope_tables(positions)
    x2 = x.reshape(T, D)
    h = _rmsnorm(x2, norm_mix[0])
    out = None
    for l in range(depth):
        lam_init = 0.8 - 0.6 * math.exp(-0.3 * l)
        w_all = w_in[l, :, :n_main] * col_scale[None, :]
        group = lambda g: w_all[:, g * BRANCH_WIDTH:(g + 1) * BRANCH_WIDTH]
        w_main = jnp.concatenate([group(g) for g in ROW_GROUPS], axis=1).astype(BF16)
        w_ft = jnp.concatenate([group(g) for g in FEAT_GROUPS], axis=1).T.astype(BF16)
        w_fl = jnp.zeros((D, LANES), F32).at[:, :C_HEADS].set(w_in[l, :, n_main:]).astype(BF16)
        b_fl = jnp.zeros((1, LANES), F32).at[0, :C_HEADS].set(b_forget[l].astype(F32))

        proj, ft = _inproj(h, w_main, w_ft, cosf, sinf, cos_t, sin_t)
        proj = proj.reshape(len(ROW_GROUPS) * GROUP_BLOCKS, B, S, LANES)
        f_tok, k_bias = _forget_cumsum(h.reshape(B, S, D), w_fl, b_fl)
        f_row = f_tok[:, :, :C_HEADS].transpose(0, 2, 1).reshape(B, N_PAIRS, 2, S)

        ya = _band_attention(proj, ft, _band_bias(rel_bias[l], min(TQ_FLASH, S)), B, S)
        yb = _diff_attention(proj, ft, lambdas[l], sub_gain[l], lam_init, B, S)
        yc = _forget_attention(proj, ft, f_row, k_bias, B, S)

        w_br = jnp.stack([w_br_a[l], w_br_b[l], w_br_c[l]]).astype(BF16)
        x2 = _merge(x2, h, ya.reshape(GROUP_BLOCKS, T, LANES), yb.reshape(GROUP_BLOCKS, T, LANES),
                    yc.reshape(GROUP_BLOCKS, T, LANES), w_gate[l].astype(BF16),
                    b_gate[l].astype(F32).reshape(1, -1), w_br, w_out[l].astype(BF16))
        last = l == depth - 1
        g_next = final_norm if last else norm_mix[l + 1]
        res = _mlp(x2, norm_mlp[l], w_ff1[l].astype(BF16), w_ff2[l].astype(BF16), g_next, last)
        if last:
            out = res
        else:
            x2, h = res
    return out.reshape(B, S, D)
```

```python
import functools
import math

import jax
import jax.numpy as jnp
import numpy as np
from jax import lax
from jax.experimental import pallas as pl
from jax.experimental.pallas import tpu as pltpu

D_MODEL = 1024
CHUNK = 64
LEFT_CHUNKS = 8
HEAD_DIM = 64
A_HEADS = 8
B_HEADS = 4
C_HEADS = 8
BRANCH_WIDTH = 512
N_BRANCH = 3
D_FF = 4 * D_MODEL
REL_CLIP = 128
ROPE_THETA = 500000.0
ROPE_DIM = HEAD_DIM // 4
ROPE_HALF = ROPE_DIM // 2
EPS = 1e-6

LANES = 128
LOG2E = 1.4426950408889634
NEG_BIG = -1e30

TM_DENSE = 512
TQ_FLASH = 512
VMEM_LIMIT = 56 * 1024 * 1024

BF16 = jnp.bfloat16
F32 = jnp.float32


def _params(*sem):
    return pltpu.CompilerParams(dimension_semantics=sem, vmem_limit_bytes=VMEM_LIMIT)


def _const_spec(shape):
    nd = len(shape)
    return pl.BlockSpec(shape, lambda *_: (0,) * nd, pipeline_mode=pl.Buffered(1))


def _rms(x, g):
    return x * lax.rsqrt(jnp.mean(x * x, axis=-1, keepdims=True) + EPS) * g


def _rope_table_kernel(ang_ref, cos_ref, sin_ref):
    ang = ang_ref[...]
    row = lax.broadcasted_iota(jnp.int32, ang.shape, 0)
    sign = jnp.where(row < ROPE_HALF, -1.0, 1.0)
    cos_ref[...] = jnp.cos(ang)
    sin_ref[...] = jnp.sin(ang) * sign


def _rope_tables(positions):
    T = positions.size
    inv_freq = jnp.power(ROPE_THETA, -jnp.arange(0, ROPE_DIM, 2, dtype=F32) / ROPE_DIM)
    ang = jnp.tile(inv_freq, 2)[:, None] * positions.reshape(1, T).astype(F32)
    tt = min(8192, T)
    spec = pl.BlockSpec((ROPE_DIM, tt), lambda i: (0, i))
    return pl.pallas_call(
        _rope_table_kernel,
        out_shape=(jax.ShapeDtypeStruct((ROPE_DIM, T), F32),) * 2,
        grid=(T // tt,), in_specs=[spec], out_specs=(spec, spec),
        compiler_params=_params("parallel"), name="rope_tables",
    )(ang)


def _rmsnorm_kernel(x_ref, g_ref, h_ref):
    h_ref[...] = _rms(x_ref[...], g_ref[...]).astype(h_ref.dtype)


def _rmsnorm(x2, g):
    T = x2.shape[0]
    tm = min(1024, T)
    return pl.pallas_call(
        _rmsnorm_kernel,
        out_shape=jax.ShapeDtypeStruct((T, D_MODEL), BF16),
        grid=(T // tm,),
        in_specs=[pl.BlockSpec((tm, D_MODEL), lambda i: (i, 0)), _const_spec((1, D_MODEL))],
        out_specs=pl.BlockSpec((tm, D_MODEL), lambda i: (i, 0)),
        compiler_params=_params("parallel"), name="rmsnorm",
    )(x2, g.reshape(1, D_MODEL))


ROW_GROUPS = (1, 7)
FEAT_GROUPS = (5, 8, 3, 6, 0, 2, 4)
FEAT_ROPE_GROUPS = (2, 6)
ROW_SLOTS = {0: 0, 1: 2}
KB_FEAT_GROUP, KB_ROW_SLOT = 6, 1
GROUP_BLOCKS = BRANCH_WIDTH // LANES


def _inproj_kernel(h_ref, w_ref, wft_ref, cos_t_ref, sin_t_ref, o_ref, ft_ref):
    h = h_ref[...]
    for c in range(len(ROW_GROUPS)):
        res = jnp.dot(h, w_ref[:, c * BRANCH_WIDTH:(c + 1) * BRANCH_WIDTH],
                      preferred_element_type=F32)
        for s in range(GROUP_BLOCKS):
            o_ref[ROW_SLOTS[c] * GROUP_BLOCKS + s] = res[:, s * LANES:(s + 1) * LANES].astype(o_ref.dtype)
    for c in range(len(FEAT_GROUPS)):
        res_t = lax.dot_general(wft_ref[c * BRANCH_WIDTH:(c + 1) * BRANCH_WIDTH, :], h,
                                (((1,), (1,)), ((), ())), preferred_element_type=F32)
        for s in range(GROUP_BLOCKS):
            blk = res_t[s * LANES:(s + 1) * LANES, :]
            if c in FEAT_ROPE_GROUPS:
                pieces = []
                for base in range(0, LANES, HEAD_DIM):
                    x1 = blk[base:base + ROPE_HALF]
                    x2 = blk[base + ROPE_HALF:base + ROPE_DIM]
                    pieces.append(jnp.concatenate([x1, x2], axis=0) * cos_t_ref[...]
                                  + jnp.concatenate([x2, x1], axis=0) * sin_t_ref[...])
                    pieces.append(blk[base + ROPE_DIM:base + HEAD_DIM])
                blk = jnp.concatenate(pieces, axis=0)
            if c == KB_FEAT_GROUP:
                o_ref[KB_ROW_SLOT * GROUP_BLOCKS + s] = blk.T.astype(o_ref.dtype)
            else:
                ft_ref[c * GROUP_BLOCKS + s] = blk.astype(ft_ref.dtype)


def _inproj(h, w_main, w_ft, cos_t, sin_t):
    T = h.shape[0]
    tm = min(TM_DENSE, T)
    n_row = (len(ROW_GROUPS) + 1) * GROUP_BLOCKS
    n_ft = (len(FEAT_GROUPS) - 1) * GROUP_BLOCKS
    feat = pl.BlockSpec((ROPE_DIM, tm), lambda i: (0, i))
    return pl.pallas_call(
        _inproj_kernel,
        out_shape=(jax.ShapeDtypeStruct((n_row, T, LANES), BF16),
                   jax.ShapeDtypeStruct((n_ft, LANES, T), BF16)),
        grid=(T // tm,),
        in_specs=[pl.BlockSpec((tm, D_MODEL), lambda i: (i, 0)),
                  _const_spec(w_main.shape), _const_spec(w_ft.shape), feat, feat],
        out_specs=(pl.BlockSpec((n_row, tm, LANES), lambda i: (0, i, 0)),
                   pl.BlockSpec((n_ft, LANES, tm), lambda i: (0, 0, i))),
        compiler_params=_params("parallel"), name="inproj",
    )(h, w_main, w_ft, cos_t, sin_t)


N_PAIRS = C_HEADS // 2
FORGET_ROWS = 2


def _split3(x):
    hi = x.astype(BF16)
    r1 = x - hi.astype(F32)
    mid = r1.astype(BF16)
    lo = (r1 - mid.astype(F32)).astype(BF16)
    return hi, mid, lo


def _forget_constants(tri_ref, place_ref):
    tm = tri_ref.shape[0]
    row = lax.broadcasted_iota(jnp.int32, (tm, tm), 0)
    col = lax.broadcasted_iota(jnp.int32, (tm, tm), 1)
    tri_ref[...] = (col <= row).astype(BF16)
    src = lax.broadcasted_iota(jnp.int32, place_ref.shape, 0)
    dst = lax.broadcasted_iota(jnp.int32, place_ref.shape, 1)
    pair, lane = dst // LANES, dst % LANES
    chain, piece = lane // 3, lane % 3
    place = (lane < 2 * 3) & (src == piece * LANES + 2 * pair + chain)
    place_ref[...] = jnp.where(place, -1.0, 0.0).astype(BF16)


def _forget_kernel(h_ref, w_ref, b_ref, f_ref, kbias_ref, tri_ref, place_ref, carry_ref):
    @pl.when((pl.program_id(0) == 0) & (pl.program_id(1) == 0))
    def _():
        _forget_constants(tri_ref, place_ref)

    @pl.when(pl.program_id(1) == 0)
    def _():
        carry_ref[...] = jnp.zeros_like(carry_ref)

    tm = h_ref.shape[1]
    for r in range(h_ref.shape[0]):
        fl = jnp.dot(h_ref[r], w_ref[...], preferred_element_type=F32) + b_ref[...]
        ls = jnp.minimum(fl, 0.0) - jnp.log1p(jnp.exp(-jnp.abs(fl)))
        cs = sum(jnp.dot(tri_ref[...], part, preferred_element_type=F32) for part in _split3(ls))
        f = cs + carry_ref[r]
        carry_ref[r] = f[tm - 1:tm, :]
        f2 = f * LOG2E
        f_ref[r] = f2
        parts = jnp.concatenate(_split3(f2), axis=1)
        placed = jnp.dot(parts, place_ref[...], preferred_element_type=F32)
        for p in range(N_PAIRS):
            kbias_ref[r, p] = placed[:, p * LANES:(p + 1) * LANES].astype(kbias_ref.dtype)


def _forget_cumsum(h3, w_fl, b_fl):
    B, S, _ = h3.shape
    tm = min(TM_DENSE, S)
    rows = FORGET_ROWS if B % FORGET_ROWS == 0 else 1
    return pl.pallas_call(
        _forget_kernel,
        out_shape=(jax.ShapeDtypeStruct((B, S, LANES), F32),
                   jax.ShapeDtypeStruct((B, N_PAIRS, S, LANES), BF16)),
        grid=(B // rows, S // tm),
        in_specs=[pl.BlockSpec((rows, tm, D_MODEL), lambda b, i: (b, i, 0)),
                  _const_spec(w_fl.shape), _const_spec(b_fl.shape)],
        out_specs=(pl.BlockSpec((rows, tm, LANES), lambda b, i: (b, i, 0)),
                   pl.BlockSpec((rows, N_PAIRS, tm, LANES), lambda b, i: (b, 0, i, 0))),
        scratch_shapes=[pltpu.VMEM((tm, tm), BF16), pltpu.VMEM((3 * LANES, N_PAIRS * LANES), BF16),
                        pltpu.VMEM((rows, 1, LANES), F32)],
        compiler_params=_params("arbitrary", "arbitrary"), name="forget_cumsum",
    )(h3, w_fl, b_fl)


ROW_T, ROW_ALPHA, ROW_MPREV = 0, 1, 2
FLASH_UNROLL = 6
SUM_ROWS = 16
Q_SPLIT = 2
assert FLASH_UNROLL % 2 == 0


def _flash_body(qt_ref, k_ref, vt_ref, fq_ref, kbias_ref, bias_ref, o_ref, finalize, s_refs, st_refs, mrun_ref,
                acc_ref, *, tq, chunk_causal, split_values):
    band = bias_ref is not None

    def key_rows(u, own_block):
        if not band or Q_SPLIT != 2:
            return slice(0, tk)
        first_half = units[u][1] == 0
        if own_block:
            return slice(0, tk // 2) if first_half else slice(0, tk)
        return slice(0, tk) if first_half else slice(tk // 2, tk)
    tk = tq
    nq = qt_ref.shape[2] // tq
    width = tq // Q_SPLIT
    units = [(c, n) for c in range(2) for n in range(Q_SPLIT)]
    feat = lax.broadcasted_iota(jnp.int32, (LANES, 1), 0)
    own = [feat < HEAD_DIM, feat >= HEAD_DIM]

    def queries(qi):
        q_off = pl.multiple_of(qi * tq, tq)
        qt = qt_ref[0, :, pl.ds(q_off, tq)]
        qs = [jnp.where(own[c], qt, jnp.zeros_like(qt)) for c in range(2)]
        if kbias_ref is not None:
            qs = [jnp.concatenate([qs[c], ((feat >= 3 * c) & (feat < 3 * c + 3)).astype(qt.dtype)
                                   * jnp.ones_like(qt)], axis=0) for c in range(2)]
        shifts = [None if fq_ref is None else fq_ref[0, 0, c:c + 1, pl.ds(q_off, tq)] for c in range(2)]
        cols = lambda x, n: None if x is None else x[:, n * width:(n + 1) * width]
        return [(cols(qs[c], n), cols(shifts[c], n)) for c, n in units]

    def stats(shift, col_max, m_prev):
        if shift is not None:
            col_max = col_max + shift
        m_new = jnp.maximum(m_prev, col_max)
        alpha = jnp.exp2(m_prev - m_new)
        t = m_new if shift is None else m_new - shift
        return m_new, alpha, t

    def keys(j):
        off = pl.multiple_of(j * tk, tk)
        k = k_ref[0, 0, pl.ds(off, tk), :]
        if kbias_ref is not None:
            k = jnp.concatenate([k, kbias_ref[0, 0, pl.ds(off, tk), :]], axis=1)
        return k

    def values(j):
        off = pl.multiple_of(j * tk, tk)
        vt = vt_ref[0, :, pl.ds(off, tk)]
        ones = jnp.ones((SUM_ROWS, tk), vt.dtype)
        if split_values:
            return [jnp.concatenate([vt[c * HEAD_DIM:(c + 1) * HEAD_DIM], ones], axis=0) for c in range(2)]
        return [jnp.concatenate([vt, ones], axis=0)] * 2

    def scores(u, k, slot, query, first=False, own_block=True):
        q_u, shift = query[u]
        rows = key_rows(u, own_block)
        s = jnp.dot(k[rows], q_u, preferred_element_type=F32)
        if band:
            c, n = units[u]
            s = s + bias_ref[0, c, int(own_block), rows, n * width:(n + 1) * width]
        s_refs[slot][u, rows] = s
        m_prev = jnp.full((1, width), NEG_BIG, F32) if first else mrun_ref[u]
        m_new, alpha, t = stats(shift, jnp.max(s, axis=0, keepdims=True), m_prev)
        mrun_ref[u] = m_new
        st_refs[slot][u, ROW_T:ROW_T + 1] = t
        st_refs[slot][u, ROW_ALPHA:ROW_ALPHA + 1] = alpha
        st_refs[slot][u, ROW_MPREV:ROW_MPREV + 1] = m_prev

    def update(u, vts, slot, mask, query, own_block=True):
        rows = key_rows(u, own_block)
        s = s_refs[slot][u, rows]
        if mask is None:
            t = st_refs[slot][u, ROW_T:ROW_T + 1]
            alpha = st_refs[slot][u, ROW_ALPHA:ROW_ALPHA + 1]
        else:
            n = units[u][1]
            s = jnp.where(mask[:, n * width:(n + 1) * width], s, NEG_BIG)
            _, alpha, t = stats(query[u][1], jnp.max(s, axis=0, keepdims=True),
                                st_refs[slot][u, ROW_MPREV:ROW_MPREV + 1])
        p = jnp.exp2(s - t).astype(BF16)
        acc_ref[u] = alpha * acc_ref[u] + jnp.dot(vts[units[u][0]][:, rows], p, preferred_element_type=F32)

    key = lax.broadcasted_iota(jnp.int32, (tk, tq), 0)
    qry = lax.broadcasted_iota(jnp.int32, (tk, tq), 1)
    if band:
        mask = None
    elif chunk_causal:
        mask = (key // CHUNK) <= (qry // CHUNK)
    else:
        mask = key <= qry

    def run(j0, n_full, query):
        for b in range(n_full):
            k, vts = keys(j0 + b + 1), values(j0 + b)
            for u in range(len(units)):
                scores(u, k, (b + 1) % 2, query)
                update(u, vts, b % 2, None, query, own_block=not band)

    k0 = keys(0)
    first_query = queries(0)
    for u in range(len(units)):
        scores(u, k0, 0, first_query, first=True)

    def query_block(qi, carry):
        query = queries(qi)
        acc_ref[...] = jnp.zeros(acc_ref.shape, F32)
        if band:
            n_before, n_tails = jnp.minimum(qi, 1), 2
        else:
            def group(jj, inner):
                run(jj * FLASH_UNROLL, FLASH_UNROLL, query)
                return inner

            lax.fori_loop(0, qi // FLASH_UNROLL, group, 0)
            n_before, n_tails = qi % FLASH_UNROLL, FLASH_UNROLL
        for rem in range(n_tails):
            @pl.when(n_before == rem)
            def _():
                run(qi - rem, rem, query)
                nxt = jnp.minimum(qi + 1, nq - 1)
                vts, k_first = values(qi), keys(jnp.maximum(nxt - 1, 0) if band else 0)
                next_query = queries(nxt)
                for u in range(len(units)):
                    update(u, vts, rem % 2, mask, query)
                    scores(u, k_first, 0, next_query, first=True, own_block=False)
                o_ref[0, 0, pl.ds(pl.multiple_of(qi * tq, tq), tq), :] = finalize(acc_ref).astype(o_ref.dtype)
        return carry

    lax.fori_loop(0, nq, query_block, 0)


def _diff_kernel(qt_ref, k_ref, vt_ref, lam_ref, gain_ref, o_ref, s0_ref, s1_ref, st0_ref, st1_ref,
                 mrun_ref, acc_ref, *, tq, lam_init):
    lf = lam_ref[...]
    lam = (jnp.exp(jnp.sum(lf[0:1] * lf[1:2], axis=-1, keepdims=True))
           - jnp.exp(jnp.sum(lf[2:3] * lf[3:4], axis=-1, keepdims=True)) + lam_init)
    gain = gain_ref[...]

    def finalize(acc):
        chain = lambda c: jnp.concatenate([acc[c * Q_SPLIT + n] for n in range(Q_SPLIT)], axis=1)
        maps = [a[:LANES] / a[LANES:LANES + 1] for a in (chain(0), chain(1))]
        o_t = maps[0] - lam * maps[1]
        return _rms(o_t.T, gain) * (1.0 - lam_init)

    _flash_body(qt_ref, k_ref, vt_ref, None, None, None, o_ref, finalize, (s0_ref, s1_ref), (st0_ref, st1_ref),
                mrun_ref, acc_ref, tq=tq, chunk_causal=True, split_values=False)


def _head_pair_output(acc):
    chain = lambda c: jnp.concatenate([acc[c * Q_SPLIT + n] for n in range(Q_SPLIT)], axis=1)
    heads = [a[:HEAD_DIM] / a[HEAD_DIM:HEAD_DIM + 1] for a in (chain(0), chain(1))]
    return jnp.concatenate(heads, axis=0).T


def _forget_attn_kernel(qt_ref, k_ref, vt_ref, fq_ref, kbias_ref, o_ref, s0_ref, s1_ref, st0_ref, st1_ref,
                        mrun_ref, acc_ref, *, tq):
    _flash_body(qt_ref, k_ref, vt_ref, fq_ref, kbias_ref, None, o_ref, _head_pair_output, (s0_ref, s1_ref),
                (st0_ref, st1_ref), mrun_ref, acc_ref, tq=tq, chunk_causal=False, split_values=True)


def _band_kernel(qt_ref, k_ref, vt_ref, bias_ref, o_ref, s0_ref, s1_ref, st0_ref, st1_ref, mrun_ref, acc_ref, *,
                 tq):
    _flash_body(qt_ref, k_ref, vt_ref, None, None, bias_ref, o_ref, _head_pair_output, (s0_ref, s1_ref),
                (st0_ref, st1_ref), mrun_ref, acc_ref, tq=tq, chunk_causal=False, split_values=True)


def _flash_scratch(tq, acc_rows):
    n_units, width = 2 * Q_SPLIT, tq // Q_SPLIT
    return [pltpu.VMEM((n_units, tq, width), F32), pltpu.VMEM((n_units, tq, width), F32),
            pltpu.VMEM((n_units, 3, width), F32), pltpu.VMEM((n_units, 3, width), F32),
            pltpu.VMEM((n_units, 1, width), F32), pltpu.VMEM((n_units, acc_rows, width), F32)]


def _flash_specs(qt_block, k_block, vt_block, S):
    return [pl.BlockSpec((1, LANES, S), lambda b, p: (qt_block + p, 0, b)),
            pl.BlockSpec((1, 1, S, LANES), lambda b, p: (k_block + p, b, 0, 0)),
            pl.BlockSpec((1, LANES, S), lambda b, p: (vt_block + p, 0, b))]


def _diff_attention(proj, ft, lambdas, sub_gain, lam_init, B, S):
    tq = min(TQ_FLASH, S)
    return pl.pallas_call(
        functools.partial(_diff_kernel, tq=tq, lam_init=lam_init),
        out_shape=jax.ShapeDtypeStruct((GROUP_BLOCKS, B, S, LANES), BF16),
        grid=(B, GROUP_BLOCKS),
        in_specs=_flash_specs(2 * GROUP_BLOCKS, GROUP_BLOCKS, 0, S) + [
            _const_spec(lambdas.shape), _const_spec((1, LANES))],
        out_specs=pl.BlockSpec((1, 1, S, LANES), lambda b, p: (p, b, 0, 0)),
        scratch_shapes=_flash_scratch(tq, LANES + SUM_ROWS),
        compiler_params=_params("parallel", "parallel"), name="diff_attention",
    )(ft, proj, ft, lambdas.astype(F32), sub_gain.astype(F32).reshape(1, LANES))


def _forget_attention(proj, ft, f_row, k_bias, B, S):
    tq = min(TQ_FLASH, S)
    return pl.pallas_call(
        functools.partial(_forget_attn_kernel, tq=tq),
        out_shape=jax.ShapeDtypeStruct((GROUP_BLOCKS, B, S, LANES), BF16),
        grid=(B, GROUP_BLOCKS),
        in_specs=_flash_specs(3 * GROUP_BLOCKS, 2 * GROUP_BLOCKS, GROUP_BLOCKS, S) + [
            pl.BlockSpec((1, 1, 2, S), lambda b, p: (b, p, 0, 0)),
            pl.BlockSpec((1, 1, S, LANES), lambda b, p: (b, p, 0, 0))],
        out_specs=pl.BlockSpec((1, 1, S, LANES), lambda b, p: (p, b, 0, 0)),
        scratch_shapes=_flash_scratch(tq, HEAD_DIM + SUM_ROWS),
        compiler_params=_params("parallel", "parallel"), name="forget_attention",
    )(ft, proj, ft, f_row, k_bias)


def _band_attention(proj, ft, bias, B, S):
    tq = min(TQ_FLASH, S)
    return pl.pallas_call(
        functools.partial(_band_kernel, tq=tq),
        out_shape=jax.ShapeDtypeStruct((GROUP_BLOCKS, B, S, LANES), BF16),
        grid=(B, GROUP_BLOCKS),
        in_specs=_flash_specs(4 * GROUP_BLOCKS, 0, 5 * GROUP_BLOCKS, S) + [
            pl.BlockSpec((1,) + bias.shape[1:], lambda b, p: (p, 0, 0, 0, 0))],
        out_specs=pl.BlockSpec((1, 1, S, LANES), lambda b, p: (p, b, 0, 0)),
        scratch_shapes=_flash_scratch(tq, HEAD_DIM + SUM_ROWS),
        compiler_params=_params("parallel", "parallel"), name="band_attention",
    )(ft, proj, ft, bias)


def _band_bias(rel_bias, tq):
    n_keys = 2 * tq
    period = n_keys + tq
    u = np.arange(period)
    diag = np.where(u < n_keys, u, u - period)
    rel = np.clip(diag - tq, -REL_CLIP, REL_CLIP) + REL_CLIP
    profile = rel_bias.astype(F32)[:, rel] * LOG2E
    heads = profile.shape[0]
    rows = min(tq, 256)
    pieces = []
    for first in range(0, tq, rows):
        shifted = jnp.roll(profile, first, axis=1)
        piece = jnp.tile(shifted, (1, rows))[:, :rows * (period - 1)]
        pieces.append(piece.reshape(heads, rows, period - 1)[:, :, :n_keys])
    table = jnp.concatenate(pieces, axis=1)
    q_chunk = tq // CHUNK + np.arange(tq)[:, None] // CHUNK
    k_chunk = np.arange(n_keys)[None, :] // CHUNK
    valid = (k_chunk <= q_chunk) & (k_chunk >= q_chunk - LEFT_CHUNKS)
    table = jnp.where(jnp.asarray(valid)[None], table, NEG_BIG).transpose(0, 2, 1)
    return table.reshape(heads // 2, 2, 2, tq, tq)


def _merge_kernel(x_ref, h_ref, ya_ref, yb_ref, yc_ref, wg_ref, bg_ref, wbr_ref, wo_ref, o_ref):
    h = h_ref[...]
    merged = None
    for b, y_ref in enumerate((ya_ref, yb_ref, yc_ref)):
        y = jnp.concatenate([y_ref[i] for i in range(y_ref.shape[0])], axis=1)
        t = jnp.dot(y, wbr_ref[b], preferred_element_type=F32)
        z = jnp.dot(h, wg_ref[:, b * D_MODEL:(b + 1) * D_MODEL], preferred_element_type=F32)
        g = jax.nn.sigmoid(z + bg_ref[:, b * D_MODEL:(b + 1) * D_MODEL])
        merged = g * t if merged is None else merged + g * t
    o_ref[...] = x_ref[...] + jnp.dot(merged.astype(BF16), wo_ref[...], preferred_element_type=F32)


def _merge(x2, h, ya, yb, yc, w_gate, b_gate, w_br, w_out):
    T = x2.shape[0]
    tm = min(TM_DENSE, T)
    nblk = ya.shape[0]
    row = pl.BlockSpec((tm, D_MODEL), lambda i: (i, 0))
    yspec = pl.BlockSpec((nblk, tm, LANES), lambda i: (0, i, 0))
    return pl.pallas_call(
        _merge_kernel,
        out_shape=jax.ShapeDtypeStruct((T, D_MODEL), F32),
        grid=(T // tm,),
        in_specs=[row, row, yspec, yspec, yspec, _const_spec(w_gate.shape), _const_spec(b_gate.shape),
                  _const_spec(w_br.shape), _const_spec(w_out.shape)],
        out_specs=row,
        compiler_params=_params("parallel"), name="merge_out",
    )(x2, h, ya, yb, yc, w_gate, b_gate, w_br, w_out)


FF_CHUNK = 1024


def _mlp_kernel(x_ref, gn_ref, w1_ref, w2_ref, gnext_ref, *out_refs):
    x = x_ref[...]
    h2 = _rms(x, gn_ref[...]).astype(BF16)
    acc = x
    for c in range(D_FF // FF_CHUNK):
        a = jnp.dot(h2, w1_ref[:, c * FF_CHUNK:(c + 1) * FF_CHUNK], preferred_element_type=F32)
        a = jnp.square(jnp.maximum(a, 0.0))
        acc = acc + jnp.dot(a.astype(BF16), w2_ref[c * FF_CHUNK:(c + 1) * FF_CHUNK, :],
                            preferred_element_type=F32)
    normed = _rms(acc, gnext_ref[...])
    if len(out_refs) == 2:
        out_refs[0][...] = acc
        out_refs[1][...] = normed.astype(out_refs[1].dtype)
    else:
        out_refs[0][...] = normed.astype(out_refs[0].dtype)


def _mlp(x2, g_mlp, w1, w2, g_next, last):
    T = x2.shape[0]
    tm = min(TM_DENSE, T)
    row = pl.BlockSpec((tm, D_MODEL), lambda i: (i, 0))
    if last:
        out_shape, out_specs = jax.ShapeDtypeStruct((T, D_MODEL), F32), row
    else:
        out_shape = (jax.ShapeDtypeStruct((T, D_MODEL), F32), jax.ShapeDtypeStruct((T, D_MODEL), BF16))
        out_specs = (row, row)
    return pl.pallas_call(
        _mlp_kernel, out_shape=out_shape, grid=(T // tm,),
        in_specs=[row, _const_spec((1, D_MODEL)), _const_spec(w1.shape), _const_spec(w2.shape),
                  _const_spec((1, D_MODEL))],
        out_specs=out_specs,
        compiler_params=_params("parallel"), name="mlp",
    )(x2, g_mlp.reshape(1, D_MODEL), w1, w2, g_next.reshape(1, D_MODEL))


def kernel(x, positions, norm_mix, w_in, rel_bias, lambdas, sub_gain, b_forget, w_br_a, w_br_b, w_br_c,
           w_gate, b_gate, w_out, norm_mlp, w_ff1, w_ff2, final_norm):
    B, S, D = x.shape
    T = B * S
    depth = w_in.shape[0]
    n_main = 9 * BRANCH_WIDTH
    q_scale = HEAD_DIM ** -0.5 * LOG2E
    col_scale = jnp.ones((n_main,), F32)
    for grp in (0, 3, 6):
        col_scale = col_scale.at[grp * BRANCH_WIDTH:(grp + 1) * BRANCH_WIDTH].set(q_scale)

    cos_t, sin_t = _rope_tables(positions)
    x2 = x.reshape(T, D)
    h = _rmsnorm(x2, norm_mix[0])
    out = None
    for l in range(depth):
        lam_init = 0.8 - 0.6 * math.exp(-0.3 * l)
        w_all = w_in[l, :, :n_main] * col_scale[None, :]
        group = lambda g: w_all[:, g * BRANCH_WIDTH:(g + 1) * BRANCH_WIDTH]
        w_main = jnp.concatenate([group(g) for g in ROW_GROUPS], axis=1).astype(BF16)
        w_ft = jnp.concatenate([group(g) for g in FEAT_GROUPS], axis=1).T.astype(BF16)
        w_fl = jnp.zeros((D, LANES), F32).at[:, :C_HEADS].set(w_in[l, :, n_main:]).astype(BF16)
        b_fl = jnp.zeros((1, LANES), F32).at[0, :C_HEADS].set(b_forget[l].astype(F32))

        proj, ft = _inproj(h, w_main, w_ft, cos_t, sin_t)
        proj = proj.reshape(-1, B, S, LANES)
        f_tok, k_bias = _forget_cumsum(h.reshape(B, S, D), w_fl, b_fl)
        f_row = f_tok[:, :, :C_HEADS].transpose(0, 2, 1).reshape(B, N_PAIRS, 2, S)

        ya = _band_attention(proj, ft, _band_bias(rel_bias[l], min(TQ_FLASH, S)), B, S)
        yb = _diff_attention(proj, ft, lambdas[l], sub_gain[l], lam_init, B, S)
        yc = _forget_attention(proj, ft, f_row, k_bias, B, S)

        w_br = jnp.stack([w_br_a[l], w_br_b[l], w_br_c[l]]).astype(BF16)
        x2 = _merge(x2, h, ya.reshape(GROUP_BLOCKS, T, LANES), yb.reshape(GROUP_BLOCKS, T, LANES),
                    yc.reshape(GROUP_BLOCKS, T, LANES), w_gate[l].astype(BF16),
                    b_gate[l].astype(F32).reshape(1, -1), w_br, w_out[l].astype(BF16))
        last = l == depth - 1
        g_next = final_norm if last else norm_mix[l + 1]
        res = _mlp(x2, norm_mlp[l], w_ff1[l].astype(BF16), w_ff2[l].astype(BF16), g_next, last)
        if last:
            out = res
        else:
            x2, h = res
    return out.reshape(B, S, D)
```

```python
import functools
import math

import jax
import jax.numpy as jnp
import numpy as np
from jax import lax
from jax.experimental import pallas as pl
from jax.experimental.pallas import tpu as pltpu

D_MODEL = 1024
CHUNK = 64
LEFT_CHUNKS = 8
HEAD_DIM = 64
A_HEADS = 8
B_HEADS = 4
C_HEADS = 8
BRANCH_WIDTH = 512
N_BRANCH = 3
D_FF = 4 * D_MODEL
REL_CLIP = 128
ROPE_THETA = 500000.0
ROPE_DIM = HEAD_DIM // 4
ROPE_HALF = ROPE_DIM // 2
EPS = 1e-6

LANES = 128
LOG2E = 1.4426950408889634
NEG_BIG = -1e30

TM_DENSE = 512
TQ_FLASH = 512
VMEM_LIMIT = 56 * 1024 * 1024

BF16 = jnp.bfloat16
F32 = jnp.float32


def _params(*sem):
    return pltpu.CompilerParams(dimension_semantics=sem, vmem_limit_bytes=VMEM_LIMIT)


def _const_spec(shape):
    nd = len(shape)
    return pl.BlockSpec(shape, lambda *_: (0,) * nd, pipeline_mode=pl.Buffered(1))


def _rms(x, g):
    return x * lax.rsqrt(jnp.mean(x * x, axis=-1, keepdims=True) + EPS) * g


def _rope_table_kernel(ang_ref, cos_ref, sin_ref):
    ang = ang_ref[...]
    row = lax.broadcasted_iota(jnp.int32, ang.shape, 0)
    sign = jnp.where(row < ROPE_HALF, -1.0, 1.0)
    cos_ref[...] = jnp.cos(ang)
    sin_ref[...] = jnp.sin(ang) * sign


def _rope_tables(positions):
    T = positions.size
    inv_freq = jnp.power(ROPE_THETA, -jnp.arange(0, ROPE_DIM, 2, dtype=F32) / ROPE_DIM)
    ang = jnp.tile(inv_freq, 2)[:, None] * positions.reshape(1, T).astype(F32)
    tt = min(8192, T)
    spec = pl.BlockSpec((ROPE_DIM, tt), lambda i: (0, i))
    return pl.pallas_call(
        _rope_table_kernel,
        out_shape=(jax.ShapeDtypeStruct((ROPE_DIM, T), F32),) * 2,
        grid=(T // tt,), in_specs=[spec], out_specs=(spec, spec),
        compiler_params=_params("parallel"), name="rope_tables",
    )(ang)


def _rmsnorm_kernel(x_ref, g_ref, h_ref):
    h_ref[...] = _rms(x_ref[...], g_ref[...]).astype(h_ref.dtype)


def _rmsnorm(x2, g):
    T = x2.shape[0]
    tm = min(1024, T)
    return pl.pallas_call(
        _rmsnorm_kernel,
        out_shape=jax.ShapeDtypeStruct((T, D_MODEL), BF16),
        grid=(T // tm,),
        in_specs=[pl.BlockSpec((tm, D_MODEL), lambda i: (i, 0)), _const_spec((1, D_MODEL))],
        out_specs=pl.BlockSpec((tm, D_MODEL), lambda i: (i, 0)),
        compiler_params=_params("parallel"), name="rmsnorm",
    )(x2, g.reshape(1, D_MODEL))


ROW_GROUPS = (1, 7)
FEAT_GROUPS = (5, 8, 3, 6, 0, 2, 4)
FEAT_ROPE_GROUPS = (2, 6)
ROW_SLOTS = {0: 0, 1: 2}
KB_FEAT_GROUP, KB_ROW_SLOT = 6, 1
GROUP_BLOCKS = BRANCH_WIDTH // LANES


def _inproj_kernel(h_ref, w_ref, wft_ref, cos_t_ref, sin_t_ref, o_ref, ft_ref):
    h = h_ref[...]
    for c in range(len(ROW_GROUPS)):
        res = jnp.dot(h, w_ref[:, c * BRANCH_WIDTH:(c + 1) * BRANCH_WIDTH],
                      preferred_element_type=F32)
        for s in range(GROUP_BLOCKS):
            o_ref[ROW_SLOTS[c] * GROUP_BLOCKS + s] = res[:, s * LANES:(s + 1) * LANES].astype(o_ref.dtype)
    for c in range(len(FEAT_GROUPS)):
        res_t = lax.dot_general(wft_ref[c * BRANCH_WIDTH:(c + 1) * BRANCH_WIDTH, :], h,
                                (((1,), (1,)), ((), ())), preferred_element_type=F32)
        for s in range(GROUP_BLOCKS):
            blk = res_t[s * LANES:(s + 1) * LANES, :]
            if c in FEAT_ROPE_GROUPS:
                pieces = []
                for base in range(0, LANES, HEAD_DIM):
                    x1 = blk[base:base + ROPE_HALF]
                    x2 = blk[base + ROPE_HALF:base + ROPE_DIM]
                    pieces.append(jnp.concatenate([x1, x2], axis=0) * cos_t_ref[...]
                                  + jnp.concatenate([x2, x1], axis=0) * sin_t_ref[...])
                    pieces.append(blk[base + ROPE_DIM:base + HEAD_DIM])
                blk = jnp.concatenate(pieces, axis=0)
            if c == KB_FEAT_GROUP:
                o_ref[KB_ROW_SLOT * GROUP_BLOCKS + s] = blk.T.astype(o_ref.dtype)
            else:
                ft_ref[c * GROUP_BLOCKS + s] = blk.astype(ft_ref.dtype)


def _inproj(h, w_main, w_ft, cos_t, sin_t):
    T = h.shape[0]
    tm = min(TM_DENSE, T)
    n_row = (len(ROW_GROUPS) + 1) * GROUP_BLOCKS
    n_ft = (len(FEAT_GROUPS) - 1) * GROUP_BLOCKS
    feat = pl.BlockSpec((ROPE_DIM, tm), lambda i: (0, i))
    return pl.pallas_call(
        _inproj_kernel,
        out_shape=(jax.ShapeDtypeStruct((n_row, T, LANES), BF16),
                   jax.ShapeDtypeStruct((n_ft, LANES, T), BF16)),
        grid=(T // tm,),
        in_specs=[pl.BlockSpec((tm, D_MODEL), lambda i: (i, 0)),
                  _const_spec(w_main.shape), _const_spec(w_ft.shape), feat, feat],
        out_specs=(pl.BlockSpec((n_row, tm, LANES), lambda i: (0, i, 0)),
                   pl.BlockSpec((n_ft, LANES, tm), lambda i: (0, 0, i))),
        compiler_params=_params("parallel"), name="inproj",
    )(h, w_main, w_ft, cos_t, sin_t)


N_PAIRS = C_HEADS // 2
FORGET_ROWS = 2


def _split3(x):
    hi = x.astype(BF16)
    r1 = x - hi.astype(F32)
    mid = r1.astype(BF16)
    lo = (r1 - mid.astype(F32)).astype(BF16)
    return hi, mid, lo


def _forget_constants(tri_ref, place_ref):
    tm = tri_ref.shape[0]
    row = lax.broadcasted_iota(jnp.int32, (tm, tm), 0)
    col = lax.broadcasted_iota(jnp.int32, (tm, tm), 1)
    tri_ref[...] = (col <= row).astype(BF16)
    src = lax.broadcasted_iota(jnp.int32, place_ref.shape, 0)
    dst = lax.broadcasted_iota(jnp.int32, place_ref.shape, 1)
    pair, lane = dst // LANES, dst % LANES
    chain, piece = lane // 3, lane % 3
    place = (lane < 2 * 3) & (src == piece * LANES + 2 * pair + chain)
    place_ref[...] = jnp.where(place, -1.0, 0.0).astype(BF16)


def _forget_kernel(h_ref, w_ref, b_ref, f_ref, kbias_ref, tri_ref, place_ref, carry_ref):
    @pl.when((pl.program_id(0) == 0) & (pl.program_id(1) == 0))
    def _():
        _forget_constants(tri_ref, place_ref)

    @pl.when(pl.program_id(1) == 0)
    def _():
        carry_ref[...] = jnp.zeros_like(carry_ref)

    tm = h_ref.shape[1]
    for r in range(h_ref.shape[0]):
        fl = jnp.dot(h_ref[r], w_ref[...], preferred_element_type=F32) + b_ref[...]
        ls = jnp.minimum(fl, 0.0) - jnp.log1p(jnp.exp(-jnp.abs(fl)))
        cs = sum(jnp.dot(tri_ref[...], part, preferred_element_type=F32) for part in _split3(ls))
        f = cs + carry_ref[r]
        carry_ref[r] = f[tm - 1:tm, :]
        f2 = f * LOG2E
        f_ref[r] = f2
        parts = jnp.concatenate(_split3(f2), axis=1)
        placed = jnp.dot(parts, place_ref[...], preferred_element_type=F32)
        for p in range(N_PAIRS):
            kbias_ref[r, p] = placed[:, p * LANES:(p + 1) * LANES].astype(kbias_ref.dtype)


def _forget_cumsum(h3, w_fl, b_fl):
    B, S, _ = h3.shape
    tm = min(TM_DENSE, S)
    rows = FORGET_ROWS if B % FORGET_ROWS == 0 else 1
    return pl.pallas_call(
        _forget_kernel,
        out_shape=(jax.ShapeDtypeStruct((B, S, LANES), F32),
                   jax.ShapeDtypeStruct((B, N_PAIRS, S, LANES), BF16)),
        grid=(B // rows, S // tm),
        in_specs=[pl.BlockSpec((rows, tm, D_MODEL), lambda b, i: (b, i, 0)),
                  _const_spec(w_fl.shape), _const_spec(b_fl.shape)],
        out_specs=(pl.BlockSpec((rows, tm, LANES), lambda b, i: (b, i, 0)),
                   pl.BlockSpec((rows, N_PAIRS, tm, LANES), lambda b, i: (b, 0, i, 0))),
        scratch_shapes=[pltpu.VMEM((tm, tm), BF16), pltpu.VMEM((3 * LANES, N_PAIRS * LANES), BF16),
                        pltpu.VMEM((rows, 1, LANES), F32)],
        compiler_params=_params("arbitrary", "arbitrary"), name="forget_cumsum",
    )(h3, w_fl, b_fl)


ROW_T, ROW_ALPHA, ROW_MPREV = 0, 1, 2
FLASH_UNROLL = 6
SUM_ROWS = 16
Q_SPLIT = 2
BAND_ALL_MASKED = 2
assert FLASH_UNROLL % 2 == 0


def _flash_body(qt_ref, k_ref, vt_ref, fq_ref, kbias_ref, bias_ref, o_ref, finalize, s_refs, st_refs, mrun_ref,
                acc_ref, *, tq, chunk_causal, split_values):
    band = bias_ref is not None

    def key_rows(u, own_block):
        if not band or Q_SPLIT != 2:
            return slice(0, tk)
        first_half = units[u][1] == 0
        if own_block:
            return slice(0, tk // 2) if first_half else slice(0, tk)
        return slice(0, tk) if first_half else slice(tk // 2, tk)
    tk = tq
    nq = qt_ref.shape[2] // tq
    width = tq // Q_SPLIT
    units = [(c, n) for c in range(2) for n in range(Q_SPLIT)]
    feat = lax.broadcasted_iota(jnp.int32, (LANES, 1), 0)
    own = [feat < HEAD_DIM, feat >= HEAD_DIM]

    def queries(qi):
        q_off = pl.multiple_of(qi * tq, tq)
        qt = qt_ref[0, :, pl.ds(q_off, tq)]
        qs = [jnp.where(own[c], qt, jnp.zeros_like(qt)) for c in range(2)]
        if kbias_ref is not None:
            qs = [jnp.concatenate([qs[c], ((feat >= 3 * c) & (feat < 3 * c + 3)).astype(qt.dtype)
                                   * jnp.ones_like(qt)], axis=0) for c in range(2)]
        shifts = [None if fq_ref is None else fq_ref[0, 0, c:c + 1, pl.ds(q_off, tq)] for c in range(2)]
        cols = lambda x, n: None if x is None else x[:, n * width:(n + 1) * width]
        return [(cols(qs[c], n), cols(shifts[c], n)) for c, n in units]

    def stats(shift, col_max, m_prev):
        if shift is not None:
            col_max = col_max + shift
        m_new = jnp.maximum(m_prev, col_max)
        alpha = jnp.exp2(m_prev - m_new)
        t = m_new if shift is None else m_new - shift
        return m_new, alpha, t

    def keys(j):
        off = pl.multiple_of(j * tk, tk)
        k = k_ref[0, 0, pl.ds(off, tk), :]
        if kbias_ref is not None:
            k = jnp.concatenate([k, kbias_ref[0, 0, pl.ds(off, tk), :]], axis=1)
        return k

    def values(j):
        off = pl.multiple_of(j * tk, tk)
        vt = vt_ref[0, :, pl.ds(off, tk)]
        ones = jnp.ones((SUM_ROWS, tk), vt.dtype)
        if split_values:
            return [jnp.concatenate([vt[c * HEAD_DIM:(c + 1) * HEAD_DIM], ones], axis=0) for c in range(2)]
        return [jnp.concatenate([vt, ones], axis=0)] * 2

    def scores(u, k, slot, query, first=False, own_block=True, table=None):
        q_u, shift = query[u]
        rows = key_rows(u, own_block)
        s = jnp.dot(k[rows], q_u, preferred_element_type=F32)
        if band:
            c, n = units[u]
            s = s + bias_ref[0, c, int(own_block) if table is None else table, rows, n * width:(n + 1) * width]
        s_refs[slot][u, rows] = s
        m_prev = jnp.full((1, width), NEG_BIG, F32) if first else mrun_ref[u]
        m_new, alpha, t = stats(shift, jnp.max(s, axis=0, keepdims=True), m_prev)
        mrun_ref[u] = m_new
        st_refs[slot][u, ROW_T:ROW_T + 1] = t
        st_refs[slot][u, ROW_ALPHA:ROW_ALPHA + 1] = alpha
        st_refs[slot][u, ROW_MPREV:ROW_MPREV + 1] = m_prev

    def update(u, vts, slot, mask, query, own_block=True):
        rows = key_rows(u, own_block)
        s = s_refs[slot][u, rows]
        if mask is None:
            t = st_refs[slot][u, ROW_T:ROW_T + 1]
            alpha = st_refs[slot][u, ROW_ALPHA:ROW_ALPHA + 1]
        else:
            n = units[u][1]
            s = jnp.where(mask[:, n * width:(n + 1) * width], s, NEG_BIG)
            _, alpha, t = stats(query[u][1], jnp.max(s, axis=0, keepdims=True),
                                st_refs[slot][u, ROW_MPREV:ROW_MPREV + 1])
        p = jnp.exp2(s - t).astype(BF16)
        acc_ref[u] = alpha * acc_ref[u] + jnp.dot(vts[units[u][0]][:, rows], p, preferred_element_type=F32)

    key = lax.broadcasted_iota(jnp.int32, (tk, tq), 0)
    qry = lax.broadcasted_iota(jnp.int32, (tk, tq), 1)
    if band:
        mask = None
    elif chunk_causal:
        mask = (key // CHUNK) <= (qry // CHUNK)
    else:
        mask = key <= qry

    def run(j0, n_full, query):
        for b in range(n_full):
            k, vts = keys(j0 + b + 1), values(jnp.maximum(j0 + b, 0))
            for u in range(len(units)):
                scores(u, k, (b + 1) % 2, query)
                update(u, vts, b % 2, None, query, own_block=not band)

    k0 = keys(0)
    first_query = queries(0)
    for u in range(len(units)):
        if band:
            scores(u, k0, 0, first_query, first=True, own_block=False, table=BAND_ALL_MASKED)
        else:
            scores(u, k0, 0, first_query, first=True)

    def query_block(qi, carry):
        query = queries(qi)
        acc_ref[...] = jnp.zeros(acc_ref.shape, F32)

        def tail(rem):
            run(qi - rem, rem, query)
            nxt = jnp.minimum(qi + 1, nq - 1)
            vts, k_first = values(qi), keys(jnp.maximum(nxt - 1, 0) if band else 0)
            next_query = queries(nxt)
            for u in range(len(units)):
                update(u, vts, rem % 2, mask, query)
                scores(u, k_first, 0, next_query, first=True, own_block=False)
            o_ref[0, 0, pl.ds(pl.multiple_of(qi * tq, tq), tq), :] = finalize(acc_ref).astype(o_ref.dtype)

        if band:
            tail(1)
        else:
            def group(jj, inner):
                run(jj * FLASH_UNROLL, FLASH_UNROLL, query)
                return inner

            lax.fori_loop(0, qi // FLASH_UNROLL, group, 0)
            for rem in range(FLASH_UNROLL):
                pl.when(qi % FLASH_UNROLL == rem)(functools.partial(tail, rem))
        return carry

    lax.fori_loop(0, nq, query_block, 0)


def _diff_kernel(qt_ref, k_ref, vt_ref, lam_ref, gain_ref, o_ref, s0_ref, s1_ref, st0_ref, st1_ref,
                 mrun_ref, acc_ref, *, tq, lam_init):
    lf = lam_ref[...]
    lam = (jnp.exp(jnp.sum(lf[0:1] * lf[1:2], axis=-1, keepdims=True))
           - jnp.exp(jnp.sum(lf[2:3] * lf[3:4], axis=-1, keepdims=True)) + lam_init)
    gain = gain_ref[...]

    def finalize(acc):
        chain = lambda c: jnp.concatenate([acc[c * Q_SPLIT + n] for n in range(Q_SPLIT)], axis=1)
        maps = [a[:LANES] / a[LANES:LANES + 1] for a in (chain(0), chain(1))]
        o_t = maps[0] - lam * maps[1]
        return _rms(o_t.T, gain) * (1.0 - lam_init)

    _flash_body(qt_ref, k_ref, vt_ref, None, None, None, o_ref, finalize, (s0_ref, s1_ref), (st0_ref, st1_ref),
                mrun_ref, acc_ref, tq=tq, chunk_causal=True, split_values=False)


def _head_pair_output(acc):
    chain = lambda c: jnp.concatenate([acc[c * Q_SPLIT + n] for n in range(Q_SPLIT)], axis=1)
    heads = [a[:HEAD_DIM] / a[HEAD_DIM:HEAD_DIM + 1] for a in (chain(0), chain(1))]
    return jnp.concatenate(heads, axis=0).T


def _forget_attn_kernel(qt_ref, k_ref, vt_ref, fq_ref, kbias_ref, o_ref, s0_ref, s1_ref, st0_ref, st1_ref,
                        mrun_ref, acc_ref, *, tq):
    _flash_body(qt_ref, k_ref, vt_ref, fq_ref, kbias_ref, None, o_ref, _head_pair_output, (s0_ref, s1_ref),
                (st0_ref, st1_ref), mrun_ref, acc_ref, tq=tq, chunk_causal=False, split_values=True)


def _band_kernel(qt_ref, k_ref, vt_ref, bias_ref, o_ref, s0_ref, s1_ref, st0_ref, st1_ref, mrun_ref, acc_ref, *,
                 tq):
    _flash_body(qt_ref, k_ref, vt_ref, None, None, bias_ref, o_ref, _head_pair_output, (s0_ref, s1_ref),
                (st0_ref, st1_ref), mrun_ref, acc_ref, tq=tq, chunk_causal=False, split_values=True)


def _flash_scratch(tq, acc_rows):
    n_units, width = 2 * Q_SPLIT, tq // Q_SPLIT
    return [pltpu.VMEM((n_units, tq, width), F32), pltpu.VMEM((n_units, tq, width), F32),
            pltpu.VMEM((n_units, 3, width), F32), pltpu.VMEM((n_units, 3, width), F32),
            pltpu.VMEM((n_units, 1, width), F32), pltpu.VMEM((n_units, acc_rows, width), F32)]


def _flash_specs(qt_block, k_block, vt_block, S):
    return [pl.BlockSpec((1, LANES, S), lambda b, p: (qt_block + p, 0, b)),
            pl.BlockSpec((1, 1, S, LANES), lambda b, p: (k_block + p, b, 0, 0)),
            pl.BlockSpec((1, LANES, S), lambda b, p: (vt_block + p, 0, b))]


def _diff_attention(proj, ft, lambdas, sub_gain, lam_init, B, S):
    tq = min(TQ_FLASH, S)
    return pl.pallas_call(
        functools.partial(_diff_kernel, tq=tq, lam_init=lam_init),
        out_shape=jax.ShapeDtypeStruct((GROUP_BLOCKS, B, S, LANES), BF16),
        grid=(B, GROUP_BLOCKS),
        in_specs=_flash_specs(2 * GROUP_BLOCKS, GROUP_BLOCKS, 0, S) + [
            _const_spec(lambdas.shape), _const_spec((1, LANES))],
        out_specs=pl.BlockSpec((1, 1, S, LANES), lambda b, p: (p, b, 0, 0)),
        scratch_shapes=_flash_scratch(tq, LANES + SUM_ROWS),
        compiler_params=_params("parallel", "parallel"), name="diff_attention",
    )(ft, proj, ft, lambdas.astype(F32), sub_gain.astype(F32).reshape(1, LANES))


def _forget_attention(proj, ft, f_row, k_bias, B, S):
    tq = min(TQ_FLASH, S)
    return pl.pallas_call(
        functools.partial(_forget_attn_kernel, tq=tq),
        out_shape=jax.ShapeDtypeStruct((GROUP_BLOCKS, B, S, LANES), BF16),
        grid=(B, GROUP_BLOCKS),
        in_specs=_flash_specs(3 * GROUP_BLOCKS, 2 * GROUP_BLOCKS, GROUP_BLOCKS, S) + [
            pl.BlockSpec((1, 1, 2, S), lambda b, p: (b, p, 0, 0)),
            pl.BlockSpec((1, 1, S, LANES), lambda b, p: (b, p, 0, 0))],
        out_specs=pl.BlockSpec((1, 1, S, LANES), lambda b, p: (p, b, 0, 0)),
        scratch_shapes=_flash_scratch(tq, HEAD_DIM + SUM_ROWS),
        compiler_params=_params("parallel", "parallel"), name="forget_attention",
    )(ft, proj, ft, f_row, k_bias)


def _band_attention(proj, ft, bias, B, S):
    tq = min(TQ_FLASH, S)
    return pl.pallas_call(
        functools.partial(_band_kernel, tq=tq),
        out_shape=jax.ShapeDtypeStruct((GROUP_BLOCKS, B, S, LANES), BF16),
        grid=(B, GROUP_BLOCKS),
        in_specs=_flash_specs(4 * GROUP_BLOCKS, 0, 5 * GROUP_BLOCKS, S) + [
            pl.BlockSpec((1,) + bias.shape[1:], lambda b, p: (p, 0, 0, 0, 0))],
        out_specs=pl.BlockSpec((1, 1, S, LANES), lambda b, p: (p, b, 0, 0)),
        scratch_shapes=_flash_scratch(tq, HEAD_DIM + SUM_ROWS),
        compiler_params=_params("parallel", "parallel"), name="band_attention",
    )(ft, proj, ft, bias)


def _band_bias(rel_bias, tq):
    n_keys = 2 * tq
    period = n_keys + tq
    u = np.arange(period)
    diag = np.where(u < n_keys, u, u - period)
    rel = np.clip(diag - tq, -REL_CLIP, REL_CLIP) + REL_CLIP
    profile = rel_bias.astype(F32)[:, rel] * LOG2E
    heads = profile.shape[0]
    rows = min(tq, 256)
    pieces = []
    for first in range(0, tq, rows):
        shifted = jnp.roll(profile, first, axis=1)
        piece = jnp.tile(shifted, (1, rows))[:, :rows * (period - 1)]
        pieces.append(piece.reshape(heads, rows, period - 1)[:, :, :n_keys])
    table = jnp.concatenate(pieces, axis=1)
    q_chunk = tq // CHUNK + np.arange(tq)[:, None] // CHUNK
    k_chunk = np.arange(n_keys)[None, :] // CHUNK
    valid = (k_chunk <= q_chunk) & (k_chunk >= q_chunk - LEFT_CHUNKS)
    table = jnp.where(jnp.asarray(valid)[None], table, NEG_BIG).transpose(0, 2, 1)
    table = table.reshape(heads // 2, 2, 2, tq, tq)
    return jnp.concatenate([table, jnp.full_like(table[:, :, :1], NEG_BIG)], axis=2)


def _merge_kernel(x_ref, h_ref, ya_ref, yb_ref, yc_ref, wg_ref, bg_ref, wbr_ref, wo_ref, o_ref):
    h = h_ref[...]
    merged = None
    for b, y_ref in enumerate((ya_ref, yb_ref, yc_ref)):
        y = jnp.concatenate([y_ref[i] for i in range(y_ref.shape[0])], axis=1)
        t = jnp.dot(y, wbr_ref[b], preferred_element_type=F32)
        z = jnp.dot(h, wg_ref[:, b * D_MODEL:(b + 1) * D_MODEL], preferred_element_type=F32)
        g = jax.nn.sigmoid(z + bg_ref[:, b * D_MODEL:(b + 1) * D_MODEL])
        merged = g * t if merged is None else merged + g * t
    o_ref[...] = x_ref[...] + jnp.dot(merged.astype(BF16), wo_ref[...], preferred_element_type=F32)


def _merge(x2, h, ya, yb, yc, w_gate, b_gate, w_br, w_out):
    T = x2.shape[0]
    tm = min(TM_DENSE, T)
    nblk = ya.shape[0]
    row = pl.BlockSpec((tm, D_MODEL), lambda i: (i, 0))
    yspec = pl.BlockSpec((nblk, tm, LANES), lambda i: (0, i, 0))
    return pl.pallas_call(
        _merge_kernel,
        out_shape=jax.ShapeDtypeStruct((T, D_MODEL), F32),
        grid=(T // tm,),
        in_specs=[row, row, yspec, yspec, yspec, _const_spec(w_gate.shape), _const_spec(b_gate.shape),
                  _const_spec(w_br.shape), _const_spec(w_out.shape)],
        out_specs=row,
        compiler_params=_params("parallel"), name="merge_out",
    )(x2, h, ya, yb, yc, w_gate, b_gate, w_br, w_out)


FF_CHUNK = 1024


def _mlp_kernel(x_ref, gn_ref, w1_ref, w2_ref, gnext_ref, *out_refs):
    x = x_ref[...]
    h2 = _rms(x, gn_ref[...]).astype(BF16)
    acc = x
    for c in range(D_FF // FF_CHUNK):
        a = jnp.dot(h2, w1_ref[:, c * FF_CHUNK:(c + 1) * FF_CHUNK], preferred_element_type=F32)
        a = jnp.square(jnp.maximum(a, 0.0))
        acc = acc + jnp.dot(a.astype(BF16), w2_ref[c * FF_CHUNK:(c + 1) * FF_CHUNK, :],
                            preferred_element_type=F32)
    normed = _rms(acc, gnext_ref[...])
    if len(out_refs) == 2:
        out_refs[0][...] = acc
        out_refs[1][...] = normed.astype(out_refs[1].dtype)
    else:
        out_refs[0][...] = normed.astype(out_refs[0].dtype)


def _mlp(x2, g_mlp, w1, w2, g_next, last):
    T = x2.shape[0]
    tm = min(TM_DENSE, T)
    row = pl.BlockSpec((tm, D_MODEL), lambda i: (i, 0))
    if last:
        out_shape, out_specs = jax.ShapeDtypeStruct((T, D_MODEL), F32), row
    else:
        out_shape = (jax.ShapeDtypeStruct((T, D_MODEL), F32), jax.ShapeDtypeStruct((T, D_MODEL), BF16))
        out_specs = (row, row)
    return pl.pallas_call(
        _mlp_kernel, out_shape=out_shape, grid=(T // tm,),
        in_specs=[row, _const_spec((1, D_MODEL)), _const_spec(w1.shape), _const_spec(w2.shape),
                  _const_spec((1, D_MODEL))],
        out_specs=out_specs,
        compiler_params=_params("parallel"), name="mlp",
    )(x2, g_mlp.reshape(1, D_MODEL), w1, w2, g_next.reshape(1, D_MODEL))


def kernel(x, positions, norm_mix, w_in, rel_bias, lambdas, sub_gain, b_forget, w_br_a, w_br_b, w_br_c,
           w_gate, b_gate, w_out, norm_mlp, w_ff1, w_ff2, final_norm):
    B, S, D = x.shape
    T = B * S
    depth = w_in.shape[0]
    n_main = 9 * BRANCH_WIDTH
    q_scale = HEAD_DIM ** -0.5 * LOG2E
    col_scale = jnp.ones((n_main,), F32)
    for grp in (0, 3, 6):
        col_scale = col_scale.at[grp * BRANCH_WIDTH:(grp + 1) * BRANCH_WIDTH].set(q_scale)

    cos_t, sin_t = _rope_tables(positions)
    x2 = x.reshape(T, D)
    h = _rmsnorm(x2, norm_mix[0])
    out = None
    for l in range(depth):
        lam_init = 0.8 - 0.6 * math.exp(-0.3 * l)
        w_all = w_in[l, :, :n_main] * col_scale[None, :]
        group = lambda g: w_all[:, g * BRANCH_WIDTH:(g + 1) * BRANCH_WIDTH]
        w_main = jnp.concatenate([group(g) for g in ROW_GROUPS], axis=1).astype(BF16)
        w_ft = jnp.concatenate([group(g) for g in FEAT_GROUPS], axis=1).T.astype(BF16)
        w_fl = jnp.zeros((D, LANES), F32).at[:, :C_HEADS].set(w_in[l, :, n_main:]).astype(BF16)
        b_fl = jnp.zeros((1, LANES), F32).at[0, :C_HEADS].set(b_forget[l].astype(F32))

        proj, ft = _inproj(h, w_main, w_ft, cos_t, sin_t)
        proj = proj.reshape(-1, B, S, LANES)
        f_tok, k_bias = _forget_cumsum(h.reshape(B, S, D), w_fl, b_fl)
        f_row = f_tok[:, :, :C_HEADS].transpose(0, 2, 1).reshape(B, N_PAIRS, 2, S)

        ya = _band_attention(proj, ft, _band_bias(rel_bias[l], min(TQ_FLASH, S)), B, S)
        yb = _diff_attention(proj, ft, lambdas[l], sub_gain[l], lam_init, B, S)
        yc = _forget_attention(proj, ft, f_row, k_bias, B, S)

        w_br = jnp.stack([w_br_a[l], w_br_b[l], w_br_c[l]]).astype(BF16)
        x2 = _merge(x2, h, ya.reshape(GROUP_BLOCKS, T, LANES), yb.reshape(GROUP_BLOCKS, T, LANES),
                    yc.reshape(GROUP_BLOCKS, T, LANES), w_gate[l].astype(BF16),
                    b_gate[l].astype(F32).reshape(1, -1), w_br, w_out[l].astype(BF16))
        last = l == depth - 1
        g_next = final_norm if last else norm_mix[l + 1]
        res = _mlp(x2, norm_mlp[l], w_ff1[l].astype(BF16), w_ff2[l].astype(BF16), g_next, last)
        if last:
            out = res
        else:
            x2, h = res
    return out.reshape(B, S, D)
```

```python
import functools
import math

import jax
import jax.numpy as jnp
import numpy as np
from jax import lax
from jax.experimental import pallas as pl
from jax.experimental.pallas import tpu as pltpu

D_MODEL = 1024
CHUNK = 64
LEFT_CHUNKS = 8
HEAD_DIM = 64
A_HEADS = 8
B_HEADS = 4
C_HEADS = 8
BRANCH_WIDTH = 512
N_BRANCH = 3
D_FF = 4 * D_MODEL
REL_CLIP = 128
ROPE_THETA = 500000.0
ROPE_DIM = HEAD_DIM // 4
ROPE_HALF = ROPE_DIM // 2
EPS = 1e-6

LANES = 128
LOG2E = 1.4426950408889634
NEG_BIG = -1e30

TM_DENSE = 1024
TM_FORGET = 512
TQ_FLASH = 512
VMEM_LIMIT = 56 * 1024 * 1024

BF16 = jnp.bfloat16
F32 = jnp.float32


def _params(*sem):
    return pltpu.CompilerParams(dimension_semantics=sem, vmem_limit_bytes=VMEM_LIMIT)


def _const_spec(shape):
    nd = len(shape)
    return pl.BlockSpec(shape, lambda *_: (0,) * nd, pipeline_mode=pl.Buffered(1))


def _rms(x, g):
    return x * lax.rsqrt(jnp.mean(x * x, axis=-1, keepdims=True) + EPS) * g


def _rope_table_kernel(ang_ref, cos_ref, sin_ref):
    ang = ang_ref[...]
    row = lax.broadcasted_iota(jnp.int32, ang.shape, 0)
    sign = jnp.where(row < ROPE_HALF, -1.0, 1.0)
    cos_ref[...] = jnp.cos(ang)
    sin_ref[...] = jnp.sin(ang) * sign


def _rope_tables(positions):
    T = positions.size
    inv_freq = jnp.power(ROPE_THETA, -jnp.arange(0, ROPE_DIM, 2, dtype=F32) / ROPE_DIM)
    ang = jnp.tile(inv_freq, 2)[:, None] * positions.reshape(1, T).astype(F32)
    tt = min(8192, T)
    spec = pl.BlockSpec((ROPE_DIM, tt), lambda i: (0, i))
    return pl.pallas_call(
        _rope_table_kernel,
        out_shape=(jax.ShapeDtypeStruct((ROPE_DIM, T), F32),) * 2,
        grid=(T // tt,), in_specs=[spec], out_specs=(spec, spec),
        compiler_params=_params("parallel"), name="rope_tables",
    )(ang)


def _rmsnorm_kernel(x_ref, g_ref, h_ref):
    h_ref[...] = _rms(x_ref[...], g_ref[...]).astype(h_ref.dtype)


def _rmsnorm(x2, g):
    T = x2.shape[0]
    tm = min(1024, T)
    return pl.pallas_call(
        _rmsnorm_kernel,
        out_shape=jax.ShapeDtypeStruct((T, D_MODEL), BF16),
        grid=(T // tm,),
        in_specs=[pl.BlockSpec((tm, D_MODEL), lambda i: (i, 0)), _const_spec((1, D_MODEL))],
        out_specs=pl.BlockSpec((tm, D_MODEL), lambda i: (i, 0)),
        compiler_params=_params("parallel"), name="rmsnorm",
    )(x2, g.reshape(1, D_MODEL))


ROW_GROUPS = (1, 7)
FEAT_GROUPS = (5, 8, 3, 6, 0, 2, 4)
FEAT_ROPE_GROUPS = (2, 6)
ROW_SLOTS = {0: 0, 1: 2}
KB_FEAT_GROUP, KB_ROW_SLOT = 6, 1
GROUP_BLOCKS = BRANCH_WIDTH // LANES


def _inproj_kernel(h_ref, w_ref, wft_ref, cos_t_ref, sin_t_ref, o_ref, ft_ref):
    h = h_ref[...]
    for c in range(len(ROW_GROUPS)):
        res = jnp.dot(h, w_ref[:, c * BRANCH_WIDTH:(c + 1) * BRANCH_WIDTH],
                      preferred_element_type=F32)
        for s in range(GROUP_BLOCKS):
            o_ref[ROW_SLOTS[c] * GROUP_BLOCKS + s] = res[:, s * LANES:(s + 1) * LANES].astype(o_ref.dtype)
    for c in range(len(FEAT_GROUPS)):
        res_t = lax.dot_general(wft_ref[c * BRANCH_WIDTH:(c + 1) * BRANCH_WIDTH, :], h,
                                (((1,), (1,)), ((), ())), preferred_element_type=F32)
        for s in range(GROUP_BLOCKS):
            blk = res_t[s * LANES:(s + 1) * LANES, :]
            if c in FEAT_ROPE_GROUPS:
                pieces = []
                for base in range(0, LANES, HEAD_DIM):
                    x1 = blk[base:base + ROPE_HALF]
                    x2 = blk[base + ROPE_HALF:base + ROPE_DIM]
                    pieces.append(jnp.concatenate([x1, x2], axis=0) * cos_t_ref[...]
                                  + jnp.concatenate([x2, x1], axis=0) * sin_t_ref[...])
                    pieces.append(blk[base + ROPE_DIM:base + HEAD_DIM])
                blk = jnp.concatenate(pieces, axis=0)
            if c == KB_FEAT_GROUP:
                o_ref[KB_ROW_SLOT * GROUP_BLOCKS + s] = blk.T.astype(o_ref.dtype)
            else:
                ft_ref[c * GROUP_BLOCKS + s] = blk.astype(ft_ref.dtype)


def _inproj(h, w_main, w_ft, cos_t, sin_t):
    T = h.shape[0]
    tm = min(TM_DENSE, T)
    n_row = (len(ROW_GROUPS) + 1) * GROUP_BLOCKS
    n_ft = (len(FEAT_GROUPS) - 1) * GROUP_BLOCKS
    feat = pl.BlockSpec((ROPE_DIM, tm), lambda i: (0, i))
    return pl.pallas_call(
        _inproj_kernel,
        out_shape=(jax.ShapeDtypeStruct((n_row, T, LANES), BF16),
                   jax.ShapeDtypeStruct((n_ft, LANES, T), BF16)),
        grid=(T // tm,),
        in_specs=[pl.BlockSpec((tm, D_MODEL), lambda i: (i, 0)),
                  _const_spec(w_main.shape), _const_spec(w_ft.shape), feat, feat],
        out_specs=(pl.BlockSpec((n_row, tm, LANES), lambda i: (0, i, 0)),
                   pl.BlockSpec((n_ft, LANES, tm), lambda i: (0, 0, i))),
        compiler_params=_params("parallel"), name="inproj",
    )(h, w_main, w_ft, cos_t, sin_t)


N_PAIRS = C_HEADS // 2
FORGET_ROWS = 2


def _split3(x):
    hi = x.astype(BF16)
    r1 = x - hi.astype(F32)
    mid = r1.astype(BF16)
    lo = (r1 - mid.astype(F32)).astype(BF16)
    return hi, mid, lo


def _forget_constants(tri_ref, place_ref):
    tm = tri_ref.shape[0]
    row = lax.broadcasted_iota(jnp.int32, (tm, tm), 0)
    col = lax.broadcasted_iota(jnp.int32, (tm, tm), 1)
    tri_ref[...] = (col <= row).astype(BF16)
    src = lax.broadcasted_iota(jnp.int32, place_ref.shape, 0)
    dst = lax.broadcasted_iota(jnp.int32, place_ref.shape, 1)
    pair, lane = dst // LANES, dst % LANES
    chain, piece = lane // 3, lane % 3
    place = (lane < 2 * 3) & (src == piece * LANES + 2 * pair + chain)
    place_ref[...] = jnp.where(place, -1.0, 0.0).astype(BF16)


def _forget_kernel(h_ref, w_ref, b_ref, f_ref, kbias_ref, tri_ref, place_ref, carry_ref):
    @pl.when((pl.program_id(0) == 0) & (pl.program_id(1) == 0))
    def _():
        _forget_constants(tri_ref, place_ref)

    @pl.when(pl.program_id(1) == 0)
    def _():
        carry_ref[...] = jnp.zeros_like(carry_ref)

    tm = h_ref.shape[1]
    for r in range(h_ref.shape[0]):
        fl = jnp.dot(h_ref[r], w_ref[...], preferred_element_type=F32) + b_ref[...]
        ls = jnp.minimum(fl, 0.0) - jnp.log1p(jnp.exp(-jnp.abs(fl)))
        cs = sum(jnp.dot(tri_ref[...], part, preferred_element_type=F32) for part in _split3(ls))
        f = cs + carry_ref[r]
        carry_ref[r] = f[tm - 1:tm, :]
        f2 = f * LOG2E
        f_ref[r] = f2
        parts = jnp.concatenate(_split3(f2), axis=1)
        placed = jnp.dot(parts, place_ref[...], preferred_element_type=F32)
        for p in range(N_PAIRS):
            kbias_ref[r, p] = placed[:, p * LANES:(p + 1) * LANES].astype(kbias_ref.dtype)


def _forget_cumsum(h3, w_fl, b_fl):
    B, S, _ = h3.shape
    tm = min(TM_FORGET, S)
    rows = FORGET_ROWS if B % FORGET_ROWS == 0 else 1
    return pl.pallas_call(
        _forget_kernel,
        out_shape=(jax.ShapeDtypeStruct((B, S, LANES), F32),
                   jax.ShapeDtypeStruct((B, N_PAIRS, S, LANES), BF16)),
        grid=(B // rows, S // tm),
        in_specs=[pl.BlockSpec((rows, tm, D_MODEL), lambda b, i: (b, i, 0)),
                  _const_spec(w_fl.shape), _const_spec(b_fl.shape)],
        out_specs=(pl.BlockSpec((rows, tm, LANES), lambda b, i: (b, i, 0)),
                   pl.BlockSpec((rows, N_PAIRS, tm, LANES), lambda b, i: (b, 0, i, 0))),
        scratch_shapes=[pltpu.VMEM((tm, tm), BF16), pltpu.VMEM((3 * LANES, N_PAIRS * LANES), BF16),
                        pltpu.VMEM((rows, 1, LANES), F32)],
        compiler_params=_params("arbitrary", "arbitrary"), name="forget_cumsum",
    )(h3, w_fl, b_fl)


ROW_T, ROW_ALPHA, ROW_MPREV = 0, 1, 2
FLASH_UNROLL = 6
SUM_ROWS = 16
Q_SPLIT = 2
BAND_ALL_MASKED = 2
assert FLASH_UNROLL % 2 == 0


def _flash_body(qt_ref, k_ref, vt_ref, fq_ref, kbias_ref, bias_ref, o_ref, finalize, s_refs, st_refs, mrun_ref,
                acc_ref, *, tq, chunk_causal, split_values):
    band = bias_ref is not None

    def key_rows(u, own_block):
        if not band or Q_SPLIT != 2:
            return slice(0, tk)
        first_half = units[u][1] == 0
        if own_block:
            return slice(0, tk // 2) if first_half else slice(0, tk)
        return slice(0, tk) if first_half else slice(tk // 2, tk)
    tk = tq
    nq = qt_ref.shape[2] // tq
    width = tq // Q_SPLIT
    units = [(c, n) for c in range(2) for n in range(Q_SPLIT)]
    feat = lax.broadcasted_iota(jnp.int32, (LANES, 1), 0)
    own = [feat < HEAD_DIM, feat >= HEAD_DIM]

    def queries(qi):
        q_off = pl.multiple_of(qi * tq, tq)
        qt = qt_ref[0, :, pl.ds(q_off, tq)]
        qs = [jnp.where(own[c], qt, jnp.zeros_like(qt)) for c in range(2)]
        if kbias_ref is not None:
            qs = [jnp.concatenate([qs[c], ((feat >= 3 * c) & (feat < 3 * c + 3)).astype(qt.dtype)
                                   * jnp.ones_like(qt)], axis=0) for c in range(2)]
        shifts = [None if fq_ref is None else fq_ref[0, 0, c:c + 1, pl.ds(q_off, tq)] for c in range(2)]
        cols = lambda x, n: None if x is None else x[:, n * width:(n + 1) * width]
        return [(cols(qs[c], n), cols(shifts[c], n)) for c, n in units]

    def stats(shift, col_max, m_prev):
        if shift is not None:
            col_max = col_max + shift
        m_new = jnp.maximum(m_prev, col_max)
        alpha = jnp.exp2(m_prev - m_new)
        t = m_new if shift is None else m_new - shift
        return m_new, alpha, t

    def keys(j):
        off = pl.multiple_of(j * tk, tk)
        k = k_ref[0, 0, pl.ds(off, tk), :]
        if kbias_ref is not None:
            k = jnp.concatenate([k, kbias_ref[0, 0, pl.ds(off, tk), :]], axis=1)
        return k

    def values(j):
        off = pl.multiple_of(j * tk, tk)
        vt = vt_ref[0, :, pl.ds(off, tk)]
        ones = jnp.ones((SUM_ROWS, tk), vt.dtype)
        if split_values:
            return [jnp.concatenate([vt[c * HEAD_DIM:(c + 1) * HEAD_DIM], ones], axis=0) for c in range(2)]
        return [jnp.concatenate([vt, ones], axis=0)] * 2

    def scores(u, k, slot, query, first=False, own_block=True, table=None):
        q_u, shift = query[u]
        rows = key_rows(u, own_block)
        s = jnp.dot(k[rows], q_u, preferred_element_type=F32)
        if band:
            c, n = units[u]
            s = s + bias_ref[0, c, int(own_block) if table is None else table, rows, n * width:(n + 1) * width]
        s_refs[slot][u, rows] = s
        m_prev = jnp.full((1, width), NEG_BIG, F32) if first else mrun_ref[u]
        m_new, alpha, t = stats(shift, jnp.max(s, axis=0, keepdims=True), m_prev)
        mrun_ref[u] = m_new
        st_refs[slot][u, ROW_T:ROW_T + 1] = t
        st_refs[slot][u, ROW_ALPHA:ROW_ALPHA + 1] = alpha
        st_refs[slot][u, ROW_MPREV:ROW_MPREV + 1] = m_prev

    def update(u, vts, slot, mask, query, own_block=True):
        rows = key_rows(u, own_block)
        s = s_refs[slot][u, rows]
        if mask is None:
            t = st_refs[slot][u, ROW_T:ROW_T + 1]
            alpha = st_refs[slot][u, ROW_ALPHA:ROW_ALPHA + 1]
        else:
            n = units[u][1]
            s = jnp.where(mask[:, n * width:(n + 1) * width], s, NEG_BIG)
            _, alpha, t = stats(query[u][1], jnp.max(s, axis=0, keepdims=True),
                                st_refs[slot][u, ROW_MPREV:ROW_MPREV + 1])
        p = jnp.exp2(s - t).astype(BF16)
        acc_ref[u] = alpha * acc_ref[u] + jnp.dot(vts[units[u][0]][:, rows], p, preferred_element_type=F32)

    key = lax.broadcasted_iota(jnp.int32, (tk, tq), 0)
    qry = lax.broadcasted_iota(jnp.int32, (tk, tq), 1)
    if band:
        mask = None
    elif chunk_causal:
        mask = (key // CHUNK) <= (qry // CHUNK)
    else:
        mask = key <= qry

    def run(j0, n_full, query):
        for b in range(n_full):
            k, vts = keys(j0 + b + 1), values(jnp.maximum(j0 + b, 0))
            for u in range(len(units)):
                scores(u, k, (b + 1) % 2, query)
                update(u, vts, b % 2, None, query, own_block=not band)

    k0 = keys(0)
    first_query = queries(0)
    for u in range(len(units)):
        if band:
            scores(u, k0, 0, first_query, first=True, own_block=False, table=BAND_ALL_MASKED)
        else:
            scores(u, k0, 0, first_query, first=True)

    def query_block(qi, carry):
        query = queries(qi)
        acc_ref[...] = jnp.zeros(acc_ref.shape, F32)

        def tail(rem):
            run(qi - rem, rem, query)
            nxt = jnp.minimum(qi + 1, nq - 1)
            vts, k_first = values(qi), keys(jnp.maximum(nxt - 1, 0) if band else 0)
            next_query = queries(nxt)
            for u in range(len(units)):
                update(u, vts, rem % 2, mask, query)
                scores(u, k_first, 0, next_query, first=True, own_block=False)
            o_ref[0, 0, pl.ds(pl.multiple_of(qi * tq, tq), tq), :] = finalize(acc_ref).astype(o_ref.dtype)

        if band:
            tail(1)
        else:
            def group(jj, inner):
                run(jj * FLASH_UNROLL, FLASH_UNROLL, query)
                return inner

            lax.fori_loop(0, qi // FLASH_UNROLL, group, 0)
            for rem in range(FLASH_UNROLL):
                pl.when(qi % FLASH_UNROLL == rem)(functools.partial(tail, rem))
        return carry

    lax.fori_loop(0, nq, query_block, 0)


def _diff_kernel(qt_ref, k_ref, vt_ref, lam_ref, gain_ref, o_ref, s0_ref, s1_ref, st0_ref, st1_ref,
                 mrun_ref, acc_ref, *, tq, lam_init):
    lf = lam_ref[...]
    lam = (jnp.exp(jnp.sum(lf[0:1] * lf[1:2], axis=-1, keepdims=True))
           - jnp.exp(jnp.sum(lf[2:3] * lf[3:4], axis=-1, keepdims=True)) + lam_init)
    gain = gain_ref[...]

    def finalize(acc):
        chain = lambda c: jnp.concatenate([acc[c * Q_SPLIT + n] for n in range(Q_SPLIT)], axis=1)
        maps = [a[:LANES] / a[LANES:LANES + 1] for a in (chain(0), chain(1))]
        o_t = maps[0] - lam * maps[1]
        return _rms(o_t.T, gain) * (1.0 - lam_init)

    _flash_body(qt_ref, k_ref, vt_ref, None, None, None, o_ref, finalize, (s0_ref, s1_ref), (st0_ref, st1_ref),
                mrun_ref, acc_ref, tq=tq, chunk_causal=True, split_values=False)


def _head_pair_output(acc):
    chain = lambda c: jnp.concatenate([acc[c * Q_SPLIT + n] for n in range(Q_SPLIT)], axis=1)
    heads = [a[:HEAD_DIM] / a[HEAD_DIM:HEAD_DIM + 1] for a in (chain(0), chain(1))]
    return jnp.concatenate(heads, axis=0).T


def _forget_attn_kernel(qt_ref, k_ref, vt_ref, fq_ref, kbias_ref, o_ref, s0_ref, s1_ref, st0_ref, st1_ref,
                        mrun_ref, acc_ref, *, tq):
    _flash_body(qt_ref, k_ref, vt_ref, fq_ref, kbias_ref, None, o_ref, _head_pair_output, (s0_ref, s1_ref),
                (st0_ref, st1_ref), mrun_ref, acc_ref, tq=tq, chunk_causal=False, split_values=True)


def _band_kernel(qt_ref, k_ref, vt_ref, bias_ref, o_ref, s0_ref, s1_ref, st0_ref, st1_ref, mrun_ref, acc_ref, *,
                 tq):
    _flash_body(qt_ref, k_ref, vt_ref, None, None, bias_ref, o_ref, _head_pair_output, (s0_ref, s1_ref),
                (st0_ref, st1_ref), mrun_ref, acc_ref, tq=tq, chunk_causal=False, split_values=True)


def _flash_scratch(tq, acc_rows):
    n_units, width = 2 * Q_SPLIT, tq // Q_SPLIT
    return [pltpu.VMEM((n_units, tq, width), F32), pltpu.VMEM((n_units, tq, width), F32),
            pltpu.VMEM((n_units, 3, width), F32), pltpu.VMEM((n_units, 3, width), F32),
            pltpu.VMEM((n_units, 1, width), F32), pltpu.VMEM((n_units, acc_rows, width), F32)]


def _flash_specs(qt_block, k_block, vt_block, S):
    return [pl.BlockSpec((1, LANES, S), lambda b, p: (qt_block + p, 0, b)),
            pl.BlockSpec((1, 1, S, LANES), lambda b, p: (k_block + p, b, 0, 0)),
            pl.BlockSpec((1, LANES, S), lambda b, p: (vt_block + p, 0, b))]


def _diff_attention(proj, ft, lambdas, sub_gain, lam_init, B, S):
    tq = min(TQ_FLASH, S)
    return pl.pallas_call(
        functools.partial(_diff_kernel, tq=tq, lam_init=lam_init),
        out_shape=jax.ShapeDtypeStruct((GROUP_BLOCKS, B, S, LANES), BF16),
        grid=(B, GROUP_BLOCKS),
        in_specs=_flash_specs(2 * GROUP_BLOCKS, GROUP_BLOCKS, 0, S) + [
            _const_spec(lambdas.shape), _const_spec((1, LANES))],
        out_specs=pl.BlockSpec((1, 1, S, LANES), lambda b, p: (p, b, 0, 0)),
        scratch_shapes=_flash_scratch(tq, LANES + SUM_ROWS),
        compiler_params=_params("parallel", "parallel"), name="diff_attention",
    )(ft, proj, ft, lambdas.astype(F32), sub_gain.astype(F32).reshape(1, LANES))


def _forget_attention(proj, ft, f_row, k_bias, B, S):
    tq = min(TQ_FLASH, S)
    return pl.pallas_call(
        functools.partial(_forget_attn_kernel, tq=tq),
        out_shape=jax.ShapeDtypeStruct((GROUP_BLOCKS, B, S, LANES), BF16),
        grid=(B, GROUP_BLOCKS),
        in_specs=_flash_specs(3 * GROUP_BLOCKS, 2 * GROUP_BLOCKS, GROUP_BLOCKS, S) + [
            pl.BlockSpec((1, 1, 2, S), lambda b, p: (b, p, 0, 0)),
            pl.BlockSpec((1, 1, S, LANES), lambda b, p: (b, p, 0, 0))],
        out_specs=pl.BlockSpec((1, 1, S, LANES), lambda b, p: (p, b, 0, 0)),
        scratch_shapes=_flash_scratch(tq, HEAD_DIM + SUM_ROWS),
        compiler_params=_params("parallel", "parallel"), name="forget_attention",
    )(ft, proj, ft, f_row, k_bias)


def _band_attention(proj, ft, bias, B, S):
    tq = min(TQ_FLASH, S)
    return pl.pallas_call(
        functools.partial(_band_kernel, tq=tq),
        out_shape=jax.ShapeDtypeStruct((GROUP_BLOCKS, B, S, LANES), BF16),
        grid=(B, GROUP_BLOCKS),
        in_specs=_flash_specs(4 * GROUP_BLOCKS, 0, 5 * GROUP_BLOCKS, S) + [
            pl.BlockSpec((1,) + bias.shape[1:], lambda b, p: (p, 0, 0, 0, 0))],
        out_specs=pl.BlockSpec((1, 1, S, LANES), lambda b, p: (p, b, 0, 0)),
        scratch_shapes=_flash_scratch(tq, HEAD_DIM + SUM_ROWS),
        compiler_params=_params("parallel", "parallel"), name="band_attention",
    )(ft, proj, ft, bias)


def _band_bias(rel_bias, tq):
    n_keys = 2 * tq
    period = n_keys + tq
    u = np.arange(period)
    diag = np.where(u < n_keys, u, u - period)
    rel = np.clip(diag - tq, -REL_CLIP, REL_CLIP) + REL_CLIP
    profile = rel_bias.astype(F32)[:, rel] * LOG2E
    heads = profile.shape[0]
    rows = min(tq, 256)
    pieces = []
    for first in range(0, tq, rows):
        shifted = jnp.roll(profile, first, axis=1)
        piece = jnp.tile(shifted, (1, rows))[:, :rows * (period - 1)]
        pieces.append(piece.reshape(heads, rows, period - 1)[:, :, :n_keys])
    table = jnp.concatenate(pieces, axis=1)
    q_chunk = tq // CHUNK + np.arange(tq)[:, None] // CHUNK
    k_chunk = np.arange(n_keys)[None, :] // CHUNK
    valid = (k_chunk <= q_chunk) & (k_chunk >= q_chunk - LEFT_CHUNKS)
    table = jnp.where(jnp.asarray(valid)[None], table, NEG_BIG).transpose(0, 2, 1)
    table = table.reshape(heads // 2, 2, 2, tq, tq)
    return jnp.concatenate([table, jnp.full_like(table[:, :, :1], NEG_BIG)], axis=2)


def _merge_kernel(x_ref, h_ref, ya_ref, yb_ref, yc_ref, wg_ref, bg_ref, wbr_ref, wo_ref, o_ref):
    h = h_ref[...]
    merged = None
    for b, y_ref in enumerate((ya_ref, yb_ref, yc_ref)):
        y = jnp.concatenate([y_ref[i] for i in range(y_ref.shape[0])], axis=1)
        t = jnp.dot(y, wbr_ref[b], preferred_element_type=F32)
        z = jnp.dot(h, wg_ref[:, b * D_MODEL:(b + 1) * D_MODEL], preferred_element_type=F32)
        g = jax.nn.sigmoid(z + bg_ref[:, b * D_MODEL:(b + 1) * D_MODEL])
        merged = g * t if merged is None else merged + g * t
    o_ref[...] = x_ref[...] + jnp.dot(merged.astype(BF16), wo_ref[...], preferred_element_type=F32)


def _merge(x2, h, ya, yb, yc, w_gate, b_gate, w_br, w_out):
    T = x2.shape[0]
    tm = min(TM_DENSE, T)
    nblk = ya.shape[0]
    row = pl.BlockSpec((tm, D_MODEL), lambda i: (i, 0))
    yspec = pl.BlockSpec((nblk, tm, LANES), lambda i: (0, i, 0))
    return pl.pallas_call(
        _merge_kernel,
        out_shape=jax.ShapeDtypeStruct((T, D_MODEL), F32),
        grid=(T // tm,),
        in_specs=[row, row, yspec, yspec, yspec, _const_spec(w_gate.shape), _const_spec(b_gate.shape),
                  _const_spec(w_br.shape), _const_spec(w_out.shape)],
        out_specs=row,
        compiler_params=_params("parallel"), name="merge_out",
    )(x2, h, ya, yb, yc, w_gate, b_gate, w_br, w_out)


FF_CHUNK = 1024


def _mlp_kernel(x_ref, gn_ref, w1_ref, w2_ref, gnext_ref, *out_refs):
    x = x_ref[...]
    h2 = _rms(x, gn_ref[...]).astype(BF16)
    acc = x
    for c in range(D_FF // FF_CHUNK):
        a = jnp.dot(h2, w1_ref[:, c * FF_CHUNK:(c + 1) * FF_CHUNK], preferred_element_type=F32)
        a = jnp.square(jnp.maximum(a, 0.0))
        acc = acc + jnp.dot(a.astype(BF16), w2_ref[c * FF_CHUNK:(c + 1) * FF_CHUNK, :],
                            preferred_element_type=F32)
    normed = _rms(acc, gnext_ref[...])
    if len(out_refs) == 2:
        out_refs[0][...] = acc
        out_refs[1][...] = normed.astype(out_refs[1].dtype)
    else:
        out_refs[0][...] = normed.astype(out_refs[0].dtype)


def _mlp(x2, g_mlp, w1, w2, g_next, last):
    T = x2.shape[0]
    tm = min(TM_DENSE, T)
    row = pl.BlockSpec((tm, D_MODEL), lambda i: (i, 0))
    if last:
        out_shape, out_specs = jax.ShapeDtypeStruct((T, D_MODEL), F32), row
    else:
        out_shape = (jax.ShapeDtypeStruct((T, D_MODEL), F32), jax.ShapeDtypeStruct((T, D_MODEL), BF16))
        out_specs = (row, row)
    return pl.pallas_call(
        _mlp_kernel, out_shape=out_shape, grid=(T // tm,),
        in_specs=[row, _const_spec((1, D_MODEL)), _const_spec(w1.shape), _const_spec(w2.shape),
                  _const_spec((1, D_MODEL))],
        out_specs=out_specs,
        compiler_params=_params("parallel"), name="mlp",
    )(x2, g_mlp.reshape(1, D_MODEL), w1, w2, g_next.reshape(1, D_MODEL))


def kernel(x, positions, norm_mix, w_in, rel_bias, lambdas, sub_gain, b_forget, w_br_a, w_br_b, w_br_c,
           w_gate, b_gate, w_out, norm_mlp, w_ff1, w_ff2, final_norm):
    B, S, D = x.shape
    T = B * S
    depth = w_in.shape[0]
    n_main = 9 * BRANCH_WIDTH
    q_scale = HEAD_DIM ** -0.5 * LOG2E
    col_scale = jnp.ones((n_main,), F32)
    for grp in (0, 3, 6):
        col_scale = col_scale.at[grp * BRANCH_WIDTH:(grp + 1) * BRANCH_WIDTH].set(q_scale)

    cos_t, sin_t = _rope_tables(positions)
    x2 = x.reshape(T, D)
    h = _rmsnorm(x2, norm_mix[0])
    out = None
    for l in range(depth):
        lam_init = 0.8 - 0.6 * math.exp(-0.3 * l)
        w_all = w_in[l, :, :n_main] * col_scale[None, :]
        group = lambda g: w_all[:, g * BRANCH_WIDTH:(g + 1) * BRANCH_WIDTH]
        w_main = jnp.concatenate([group(g) for g in ROW_GROUPS], axis=1).astype(BF16)
        w_ft = jnp.concatenate([group(g) for g in FEAT_GROUPS], axis=1).T.astype(BF16)
        w_fl = jnp.zeros((D, LANES), F32).at[:, :C_HEADS].set(w_in[l, :, n_main:]).astype(BF16)
        b_fl = jnp.zeros((1, LANES), F32).at[0, :C_HEADS].set(b_forget[l].astype(F32))

        proj, ft = _inproj(h, w_main, w_ft, cos_t, sin_t)
        proj = proj.reshape(-1, B, S, LANES)
        f_tok, k_bias = _forget_cumsum(h.reshape(B, S, D), w_fl, b_fl)
        f_row = f_tok[:, :, :C_HEADS].transpose(0, 2, 1).reshape(B, N_PAIRS, 2, S)

        ya = _band_attention(proj, ft, _band_bias(rel_bias[l], min(TQ_FLASH, S)), B, S)
        yb = _diff_attention(proj, ft, lambdas[l], sub_gain[l], lam_init, B, S)
        yc = _forget_attention(proj, ft, f_row, k_bias, B, S)

        w_br = jnp.stack([w_br_a[l], w_br_b[l], w_br_c[l]]).astype(BF16)
        x2 = _merge(x2, h, ya.reshape(GROUP_BLOCKS, T, LANES), yb.reshape(GROUP_BLOCKS, T, LANES),
                    yc.reshape(GROUP_BLOCKS, T, LANES), w_gate[l].astype(BF16),
                    b_gate[l].astype(F32).reshape(1, -1), w_br, w_out[l].astype(BF16))
        last = l == depth - 1
        g_next = final_norm if last else norm_mix[l + 1]
        res = _mlp(x2, norm_mlp[l], w_ff1[l].astype(BF16), w_ff2[l].astype(BF16), g_next, last)
        if last:
            out = res
        else:
            x2, h = res
    return out.reshape(B, S, D)
```

```python
import functools
import math

import jax
import jax.numpy as jnp
import numpy as np
from jax import lax
from jax.experimental import pallas as pl
from jax.experimental.pallas import tpu as pltpu

D_MODEL = 1024
CHUNK = 64
LEFT_CHUNKS = 8
HEAD_DIM = 64
A_HEADS = 8
B_HEADS = 4
C_HEADS = 8
BRANCH_WIDTH = 512
N_BRANCH = 3
D_FF = 4 * D_MODEL
REL_CLIP = 128
ROPE_THETA = 500000.0
ROPE_DIM = HEAD_DIM // 4
ROPE_HALF = ROPE_DIM // 2
EPS = 1e-6

LANES = 128
LOG2E = 1.4426950408889634
NEG_BIG = -1e30

TM_DENSE = 1024
TM_FORGET = 512
TQ_FLASH = 512
VMEM_LIMIT = 56 * 1024 * 1024

BF16 = jnp.bfloat16
F32 = jnp.float32


def _params(*sem):
    return pltpu.CompilerParams(dimension_semantics=sem, vmem_limit_bytes=VMEM_LIMIT)


def _const_spec(shape):
    nd = len(shape)
    return pl.BlockSpec(shape, lambda *_: (0,) * nd, pipeline_mode=pl.Buffered(1))


def _rms(x, g):
    return x * lax.rsqrt(jnp.mean(x * x, axis=-1, keepdims=True) + EPS) * g


def _rope_table_kernel(ang_ref, cos_ref, sin_ref):
    ang = ang_ref[...]
    row = lax.broadcasted_iota(jnp.int32, ang.shape, 0)
    sign = jnp.where(row < ROPE_HALF, -1.0, 1.0)
    cos_ref[...] = jnp.cos(ang)
    sin_ref[...] = jnp.sin(ang) * sign


def _rope_tables(positions):
    T = positions.size
    inv_freq = jnp.power(ROPE_THETA, -jnp.arange(0, ROPE_DIM, 2, dtype=F32) / ROPE_DIM)
    ang = jnp.tile(inv_freq, 2)[:, None] * positions.reshape(1, T).astype(F32)
    tt = min(8192, T)
    spec = pl.BlockSpec((ROPE_DIM, tt), lambda i: (0, i))
    return pl.pallas_call(
        _rope_table_kernel,
        out_shape=(jax.ShapeDtypeStruct((ROPE_DIM, T), F32),) * 2,
        grid=(T // tt,), in_specs=[spec], out_specs=(spec, spec),
        compiler_params=_params("parallel"), name="rope_tables",
    )(ang)


ROW_GROUPS = (1, 7)
FEAT_GROUPS = (5, 8, 3, 6, 0, 2, 4)
FEAT_ROPE_GROUPS = (2, 6)
ROW_SLOTS = {0: 0, 1: 2}
KB_FEAT_GROUP, KB_ROW_SLOT = 6, 1
GROUP_BLOCKS = BRANCH_WIDTH // LANES


def _inproj_kernel(*refs, normalize):
    if normalize:
        x_ref, g_ref, w_ref, wft_ref, cos_t_ref, sin_t_ref, o_ref, ft_ref, h_ref = refs
        h = _rms(x_ref[...], g_ref[...]).astype(h_ref.dtype)
        h_ref[...] = h
    else:
        h_ref, w_ref, wft_ref, cos_t_ref, sin_t_ref, o_ref, ft_ref = refs
        h = h_ref[...]
    for c in range(len(ROW_GROUPS)):
        res = jnp.dot(h, w_ref[:, c * BRANCH_WIDTH:(c + 1) * BRANCH_WIDTH],
                      preferred_element_type=F32)
        for s in range(GROUP_BLOCKS):
            o_ref[ROW_SLOTS[c] * GROUP_BLOCKS + s] = res[:, s * LANES:(s + 1) * LANES].astype(o_ref.dtype)
    for c in range(len(FEAT_GROUPS)):
        res_t = lax.dot_general(wft_ref[c * BRANCH_WIDTH:(c + 1) * BRANCH_WIDTH, :], h,
                                (((1,), (1,)), ((), ())), preferred_element_type=F32)
        for s in range(GROUP_BLOCKS):
            blk = res_t[s * LANES:(s + 1) * LANES, :]
            if c in FEAT_ROPE_GROUPS:
                pieces = []
                for base in range(0, LANES, HEAD_DIM):
                    x1 = blk[base:base + ROPE_HALF]
                    x2 = blk[base + ROPE_HALF:base + ROPE_DIM]
                    pieces.append(jnp.concatenate([x1, x2], axis=0) * cos_t_ref[...]
                                  + jnp.concatenate([x2, x1], axis=0) * sin_t_ref[...])
                    pieces.append(blk[base + ROPE_DIM:base + HEAD_DIM])
                blk = jnp.concatenate(pieces, axis=0)
            if c == KB_FEAT_GROUP:
                o_ref[KB_ROW_SLOT * GROUP_BLOCKS + s] = blk.T.astype(o_ref.dtype)
            else:
                ft_ref[c * GROUP_BLOCKS + s] = blk.astype(ft_ref.dtype)


def _inproj(h_or_x, w_main, w_ft, cos_t, sin_t, norm_gain=None):
    T = h_or_x.shape[0]
    tm = min(TM_DENSE, T)
    n_row = (len(ROW_GROUPS) + 1) * GROUP_BLOCKS
    n_ft = (len(FEAT_GROUPS) - 1) * GROUP_BLOCKS
    row = pl.BlockSpec((tm, D_MODEL), lambda i: (i, 0))
    feat = pl.BlockSpec((ROPE_DIM, tm), lambda i: (0, i))
    normalize = norm_gain is not None
    out_shape = [jax.ShapeDtypeStruct((n_row, T, LANES), BF16), jax.ShapeDtypeStruct((n_ft, LANES, T), BF16)]
    out_specs = [pl.BlockSpec((n_row, tm, LANES), lambda i: (0, i, 0)),
                 pl.BlockSpec((n_ft, LANES, tm), lambda i: (0, 0, i))]
    lead_specs, lead_args = [row], [h_or_x]
    if normalize:
        out_shape.append(jax.ShapeDtypeStruct((T, D_MODEL), BF16))
        out_specs.append(row)
        lead_specs.append(_const_spec((1, D_MODEL)))
        lead_args.append(norm_gain.reshape(1, D_MODEL))
    return pl.pallas_call(
        functools.partial(_inproj_kernel, normalize=normalize),
        out_shape=tuple(out_shape), grid=(T // tm,),
        in_specs=lead_specs + [_const_spec(w_main.shape), _const_spec(w_ft.shape), feat, feat],
        out_specs=tuple(out_specs),
        compiler_params=_params("parallel"), name="inproj",
    )(*lead_args, w_main, w_ft, cos_t, sin_t)


N_PAIRS = C_HEADS // 2
FORGET_ROWS = 2


def _split3(x):
    hi = x.astype(BF16)
    r1 = x - hi.astype(F32)
    mid = r1.astype(BF16)
    lo = (r1 - mid.astype(F32)).astype(BF16)
    return hi, mid, lo


def _forget_constants(tri_ref, place_ref):
    tm = tri_ref.shape[0]
    row = lax.broadcasted_iota(jnp.int32, (tm, tm), 0)
    col = lax.broadcasted_iota(jnp.int32, (tm, tm), 1)
    tri_ref[...] = (col <= row).astype(BF16)
    src = lax.broadcasted_iota(jnp.int32, place_ref.shape, 0)
    dst = lax.broadcasted_iota(jnp.int32, place_ref.shape, 1)
    pair, lane = dst // LANES, dst % LANES
    chain, piece = lane // 3, lane % 3
    place = (lane < 2 * 3) & (src == piece * LANES + 2 * pair + chain)
    place_ref[...] = jnp.where(place, -1.0, 0.0).astype(BF16)


def _forget_kernel(h_ref, w_ref, b_ref, f_ref, kbias_ref, tri_ref, place_ref, carry_ref):
    @pl.when((pl.program_id(0) == 0) & (pl.program_id(1) == 0))
    def _():
        _forget_constants(tri_ref, place_ref)

    @pl.when(pl.program_id(1) == 0)
    def _():
        carry_ref[...] = jnp.zeros_like(carry_ref)

    tm = h_ref.shape[1]
    for r in range(h_ref.shape[0]):
        fl = jnp.dot(h_ref[r], w_ref[...], preferred_element_type=F32) + b_ref[...]
        ls = jnp.minimum(fl, 0.0) - jnp.log1p(jnp.exp(-jnp.abs(fl)))
        cs = sum(jnp.dot(tri_ref[...], part, preferred_element_type=F32) for part in _split3(ls))
        f = cs + carry_ref[r]
        carry_ref[r] = f[tm - 1:tm, :]
        f2 = f * LOG2E
        f_ref[r] = f2
        parts = jnp.concatenate(_split3(f2), axis=1)
        placed = jnp.dot(parts, place_ref[...], preferred_element_type=F32)
        for p in range(N_PAIRS):
            kbias_ref[r, p] = placed[:, p * LANES:(p + 1) * LANES].astype(kbias_ref.dtype)


def _forget_cumsum(h3, w_fl, b_fl):
    B, S, _ = h3.shape
    tm = min(TM_FORGET, S)
    rows = FORGET_ROWS if B % FORGET_ROWS == 0 else 1
    return pl.pallas_call(
        _forget_kernel,
        out_shape=(jax.ShapeDtypeStruct((B, S, LANES), F32),
                   jax.ShapeDtypeStruct((B, N_PAIRS, S, LANES), BF16)),
        grid=(B // rows, S // tm),
        in_specs=[pl.BlockSpec((rows, tm, D_MODEL), lambda b, i: (b, i, 0)),
                  _const_spec(w_fl.shape), _const_spec(b_fl.shape)],
        out_specs=(pl.BlockSpec((rows, tm, LANES), lambda b, i: (b, i, 0)),
                   pl.BlockSpec((rows, N_PAIRS, tm, LANES), lambda b, i: (b, 0, i, 0))),
        scratch_shapes=[pltpu.VMEM((tm, tm), BF16), pltpu.VMEM((3 * LANES, N_PAIRS * LANES), BF16),
                        pltpu.VMEM((rows, 1, LANES), F32)],
        compiler_params=_params("arbitrary", "arbitrary"), name="forget_cumsum",
    )(h3, w_fl, b_fl)


ROW_T, ROW_ALPHA, ROW_MPREV = 0, 1, 2
FLASH_UNROLL = 8
SUM_ROWS = 16
Q_SPLIT = 2
BAND_ALL_MASKED = 2
assert FLASH_UNROLL % 2 == 0


def _flash_body(qt_ref, k_ref, vt_ref, fq_ref, kbias_ref, bias_ref, o_ref, finalize, s_refs, st_refs, mrun_ref,
                acc_ref, *, tq, chunk_causal, split_values):
    band = bias_ref is not None

    def key_rows(u, own_block):
        if not band or Q_SPLIT != 2:
            return slice(0, tk)
        first_half = units[u][1] == 0
        if own_block:
            return slice(0, tk // 2) if first_half else slice(0, tk)
        return slice(0, tk) if first_half else slice(tk // 2, tk)
    tk = tq
    nq = qt_ref.shape[2] // tq
    width = tq // Q_SPLIT
    units = [(c, n) for c in range(2) for n in range(Q_SPLIT)]
    feat = lax.broadcasted_iota(jnp.int32, (LANES, 1), 0)
    own = [feat < HEAD_DIM, feat >= HEAD_DIM]

    def queries(qi):
        q_off = pl.multiple_of(qi * tq, tq)
        qt = qt_ref[0, :, pl.ds(q_off, tq)]
        qs = [jnp.where(own[c], qt, jnp.zeros_like(qt)) for c in range(2)]
        if kbias_ref is not None:
            qs = [jnp.concatenate([qs[c], ((feat >= 3 * c) & (feat < 3 * c + 3)).astype(qt.dtype)
                                   * jnp.ones_like(qt)], axis=0) for c in range(2)]
        shifts = [None if fq_ref is None else fq_ref[0, 0, c:c + 1, pl.ds(q_off, tq)] for c in range(2)]
        cols = lambda x, n: None if x is None else x[:, n * width:(n + 1) * width]
        return [(cols(qs[c], n), cols(shifts[c], n)) for c, n in units]

    def stats(shift, col_max, m_prev):
        if shift is not None:
            col_max = col_max + shift
        m_new = jnp.maximum(m_prev, col_max)
        alpha = jnp.exp2(m_prev - m_new)
        t = m_new if shift is None else m_new - shift
        return m_new, alpha, t

    def keys(j):
        off = pl.multiple_of(j * tk, tk)
        k = k_ref[0, 0, pl.ds(off, tk), :]
        if kbias_ref is not None:
            k = jnp.concatenate([k, kbias_ref[0, 0, pl.ds(off, tk), :]], axis=1)
        return k

    def values(j):
        off = pl.multiple_of(j * tk, tk)
        vt = vt_ref[0, :, pl.ds(off, tk)]
        ones = jnp.ones((SUM_ROWS, tk), vt.dtype)
        if split_values:
            return [jnp.concatenate([vt[c * HEAD_DIM:(c + 1) * HEAD_DIM], ones], axis=0) for c in range(2)]
        return [jnp.concatenate([vt, ones], axis=0)] * 2

    def scores(u, k, slot, query, first=False, own_block=True, table=None):
        q_u, shift = query[u]
        rows = key_rows(u, own_block)
        s = jnp.dot(k[rows], q_u, preferred_element_type=F32)
        if band:
            c, n = units[u]
            s = s + bias_ref[0, c, int(own_block) if table is None else table, rows, n * width:(n + 1) * width]
        s_refs[slot][u, rows] = s
        m_prev = jnp.full((1, width), NEG_BIG, F32) if first else mrun_ref[u]
        m_new, alpha, t = stats(shift, jnp.max(s, axis=0, keepdims=True), m_prev)
        mrun_ref[u] = m_new
        st_refs[slot][u, ROW_T:ROW_T + 1] = t
        st_refs[slot][u, ROW_ALPHA:ROW_ALPHA + 1] = alpha
        st_refs[slot][u, ROW_MPREV:ROW_MPREV + 1] = m_prev

    def update(u, vts, slot, mask, query, own_block=True):
        rows = key_rows(u, own_block)
        s = s_refs[slot][u, rows]
        if mask is None:
            t = st_refs[slot][u, ROW_T:ROW_T + 1]
            alpha = st_refs[slot][u, ROW_ALPHA:ROW_ALPHA + 1]
        else:
            n = units[u][1]
            s = jnp.where(mask[:, n * width:(n + 1) * width], s, NEG_BIG)
            _, alpha, t = stats(query[u][1], jnp.max(s, axis=0, keepdims=True),
                                st_refs[slot][u, ROW_MPREV:ROW_MPREV + 1])
        p = jnp.exp2(s - t).astype(BF16)
        acc_ref[u] = alpha * acc_ref[u] + jnp.dot(vts[units[u][0]][:, rows], p, preferred_element_type=F32)

    key = lax.broadcasted_iota(jnp.int32, (tk, tq), 0)
    qry = lax.broadcasted_iota(jnp.int32, (tk, tq), 1)
    if band:
        mask = None
    elif chunk_causal:
        mask = (key // CHUNK) <= (qry // CHUNK)
    else:
        mask = key <= qry

    def run(j0, n_full, query):
        for b in range(n_full):
            k, vts = keys(j0 + b + 1), values(jnp.maximum(j0 + b, 0))
            for u in range(len(units)):
                scores(u, k, (b + 1) % 2, query)
                update(u, vts, b % 2, None, query, own_block=not band)

    k0 = keys(0)
    first_query = queries(0)
    for u in range(len(units)):
        if band:
            scores(u, k0, 0, first_query, first=True, own_block=False, table=BAND_ALL_MASKED)
        else:
            scores(u, k0, 0, first_query, first=True)

    def query_block(qi, carry):
        query = queries(qi)
        acc_ref[...] = jnp.zeros(acc_ref.shape, F32)

        def tail(rem):
            run(qi - rem, rem, query)
            nxt = jnp.minimum(qi + 1, nq - 1)
            vts, k_first = values(qi), keys(jnp.maximum(nxt - 1, 0) if band else 0)
            next_query = queries(nxt)
            for u in range(len(units)):
                update(u, vts, rem % 2, mask, query)
                scores(u, k_first, 0, next_query, first=True, own_block=False)
            o_ref[0, 0, pl.ds(pl.multiple_of(qi * tq, tq), tq), :] = finalize(acc_ref).astype(o_ref.dtype)

        if band:
            tail(1)
        else:
            def group(jj, inner):
                run(jj * FLASH_UNROLL, FLASH_UNROLL, query)
                return inner

            lax.fori_loop(0, qi // FLASH_UNROLL, group, 0)
            for rem in range(FLASH_UNROLL):
                pl.when(qi % FLASH_UNROLL == rem)(functools.partial(tail, rem))
        return carry

    lax.fori_loop(0, nq, query_block, 0)


def _diff_kernel(qt_ref, k_ref, vt_ref, lam_ref, gain_ref, o_ref, s0_ref, s1_ref, st0_ref, st1_ref,
                 mrun_ref, acc_ref, *, tq, lam_init):
    lf = lam_ref[...]
    lam = (jnp.exp(jnp.sum(lf[0:1] * lf[1:2], axis=-1, keepdims=True))
           - jnp.exp(jnp.sum(lf[2:3] * lf[3:4], axis=-1, keepdims=True)) + lam_init)
    gain = gain_ref[...]

    def finalize(acc):
        chain = lambda c: jnp.concatenate([acc[c * Q_SPLIT + n] for n in range(Q_SPLIT)], axis=1)
        maps = [a[:LANES] / a[LANES:LANES + 1] for a in (chain(0), chain(1))]
        o_t = maps[0] - lam * maps[1]
        return _rms(o_t.T, gain) * (1.0 - lam_init)

    _flash_body(qt_ref, k_ref, vt_ref, None, None, None, o_ref, finalize, (s0_ref, s1_ref), (st0_ref, st1_ref),
                mrun_ref, acc_ref, tq=tq, chunk_causal=True, split_values=False)


def _head_pair_output(acc):
    chain = lambda c: jnp.concatenate([acc[c * Q_SPLIT + n] for n in range(Q_SPLIT)], axis=1)
    heads = [a[:HEAD_DIM] / a[HEAD_DIM:HEAD_DIM + 1] for a in (chain(0), chain(1))]
    return jnp.concatenate(heads, axis=0).T


def _forget_attn_kernel(qt_ref, k_ref, vt_ref, fq_ref, kbias_ref, o_ref, s0_ref, s1_ref, st0_ref, st1_ref,
                        mrun_ref, acc_ref, *, tq):
    _flash_body(qt_ref, k_ref, vt_ref, fq_ref, kbias_ref, None, o_ref, _head_pair_output, (s0_ref, s1_ref),
                (st0_ref, st1_ref), mrun_ref, acc_ref, tq=tq, chunk_causal=False, split_values=True)


def _band_kernel(qt_ref, k_ref, vt_ref, bias_ref, o_ref, s0_ref, s1_ref, st0_ref, st1_ref, mrun_ref, acc_ref, *,
                 tq):
    _flash_body(qt_ref, k_ref, vt_ref, None, None, bias_ref, o_ref, _head_pair_output, (s0_ref, s1_ref),
                (st0_ref, st1_ref), mrun_ref, acc_ref, tq=tq, chunk_causal=False, split_values=True)


def _flash_scratch(tq, acc_rows):
    n_units, width = 2 * Q_SPLIT, tq // Q_SPLIT
    return [pltpu.VMEM((n_units, tq, width), F32), pltpu.VMEM((n_units, tq, width), F32),
            pltpu.VMEM((n_units, 3, width), F32), pltpu.VMEM((n_units, 3, width), F32),
            pltpu.VMEM((n_units, 1, width), F32), pltpu.VMEM((n_units, acc_rows, width), F32)]


def _flash_specs(qt_block, k_block, vt_block, S):
    return [pl.BlockSpec((1, LANES, S), lambda b, p: (qt_block + p, 0, b)),
            pl.BlockSpec((1, 1, S, LANES), lambda b, p: (k_block + p, b, 0, 0)),
            pl.BlockSpec((1, LANES, S), lambda b, p: (vt_block + p, 0, b))]


def _diff_attention(proj, ft, lambdas, sub_gain, lam_init, B, S):
    tq = min(TQ_FLASH, S)
    return pl.pallas_call(
        functools.partial(_diff_kernel, tq=tq, lam_init=lam_init),
        out_shape=jax.ShapeDtypeStruct((GROUP_BLOCKS, B, S, LANES), BF16),
        grid=(B, GROUP_BLOCKS),
        in_specs=_flash_specs(2 * GROUP_BLOCKS, GROUP_BLOCKS, 0, S) + [
            _const_spec(lambdas.shape), _const_spec((1, LANES))],
        out_specs=pl.BlockSpec((1, 1, S, LANES), lambda b, p: (p, b, 0, 0)),
        scratch_shapes=_flash_scratch(tq, LANES + SUM_ROWS),
        compiler_params=_params("parallel", "parallel"), name="diff_attention",
    )(ft, proj, ft, lambdas.astype(F32), sub_gain.astype(F32).reshape(1, LANES))


def _forget_attention(proj, ft, f_row, k_bias, B, S):
    tq = min(TQ_FLASH, S)
    return pl.pallas_call(
        functools.partial(_forget_attn_kernel, tq=tq),
        out_shape=jax.ShapeDtypeStruct((GROUP_BLOCKS, B, S, LANES), BF16),
        grid=(B, GROUP_BLOCKS),
        in_specs=_flash_specs(3 * GROUP_BLOCKS, 2 * GROUP_BLOCKS, GROUP_BLOCKS, S) + [
            pl.BlockSpec((1, 1, 2, S), lambda b, p: (b, p, 0, 0)),
            pl.BlockSpec((1, 1, S, LANES), lambda b, p: (b, p, 0, 0))],
        out_specs=pl.BlockSpec((1, 1, S, LANES), lambda b, p: (p, b, 0, 0)),
        scratch_shapes=_flash_scratch(tq, HEAD_DIM + SUM_ROWS),
        compiler_params=_params("parallel", "parallel"), name="forget_attention",
    )(ft, proj, ft, f_row, k_bias)


def _band_attention(proj, ft, bias, B, S):
    tq = min(TQ_FLASH, S)
    return pl.pallas_call(
        functools.partial(_band_kernel, tq=tq),
        out_shape=jax.ShapeDtypeStruct((GROUP_BLOCKS, B, S, LANES), BF16),
        grid=(B, GROUP_BLOCKS),
        in_specs=_flash_specs(4 * GROUP_BLOCKS, 0, 5 * GROUP_BLOCKS, S) + [
            pl.BlockSpec((1,) + bias.shape[1:], lambda b, p: (p, 0, 0, 0, 0))],
        out_specs=pl.BlockSpec((1, 1, S, LANES), lambda b, p: (p, b, 0, 0)),
        scratch_shapes=_flash_scratch(tq, HEAD_DIM + SUM_ROWS),
        compiler_params=_params("parallel", "parallel"), name="band_attention",
    )(ft, proj, ft, bias)


def _band_bias(rel_bias, tq):
    n_keys = 2 * tq
    period = n_keys + tq
    u = np.arange(period)
    diag = np.where(u < n_keys, u, u - period)
    rel = np.clip(diag - tq, -REL_CLIP, REL_CLIP) + REL_CLIP
    profile = rel_bias.astype(F32)[:, rel] * LOG2E
    heads = profile.shape[0]
    rows = min(tq, 256)
    pieces = []
    for first in range(0, tq, rows):
        shifted = jnp.roll(profile, first, axis=1)
        piece = jnp.tile(shifted, (1, rows))[:, :rows * (period - 1)]
        pieces.append(piece.reshape(heads, rows, period - 1)[:, :, :n_keys])
    table = jnp.concatenate(pieces, axis=1)
    q_chunk = tq // CHUNK + np.arange(tq)[:, None] // CHUNK
    k_chunk = np.arange(n_keys)[None, :] // CHUNK
    valid = (k_chunk <= q_chunk) & (k_chunk >= q_chunk - LEFT_CHUNKS)
    table = jnp.where(jnp.asarray(valid)[None], table, NEG_BIG).transpose(0, 2, 1)
    table = table.reshape(heads // 2, 2, 2, tq, tq)
    return jnp.concatenate([table, jnp.full_like(table[:, :, :1], NEG_BIG)], axis=2)


def _merge_kernel(x_ref, h_ref, ya_ref, yb_ref, yc_ref, wg_ref, bg_ref, wbr_ref, wo_ref, o_ref):
    h = h_ref[...]
    merged = None
    for b, y_ref in enumerate((ya_ref, yb_ref, yc_ref)):
        y = jnp.concatenate([y_ref[i] for i in range(y_ref.shape[0])], axis=1)
        t = jnp.dot(y, wbr_ref[b], preferred_element_type=F32)
        z = jnp.dot(h, wg_ref[:, b * D_MODEL:(b + 1) * D_MODEL], preferred_element_type=F32)
        g = jax.nn.sigmoid(z + bg_ref[:, b * D_MODEL:(b + 1) * D_MODEL])
        merged = g * t if merged is None else merged + g * t
    o_ref[...] = x_ref[...] + jnp.dot(merged.astype(BF16), wo_ref[...], preferred_element_type=F32)


def _merge(x2, h, ya, yb, yc, w_gate, b_gate, w_br, w_out):
    T = x2.shape[0]
    tm = min(TM_DENSE, T)
    nblk = ya.shape[0]
    row = pl.BlockSpec((tm, D_MODEL), lambda i: (i, 0))
    yspec = pl.BlockSpec((nblk, tm, LANES), lambda i: (0, i, 0))
    return pl.pallas_call(
        _merge_kernel,
        out_shape=jax.ShapeDtypeStruct((T, D_MODEL), F32),
        grid=(T // tm,),
        in_specs=[row, row, yspec, yspec, yspec, _const_spec(w_gate.shape), _const_spec(b_gate.shape),
                  _const_spec(w_br.shape), _const_spec(w_out.shape)],
        out_specs=row,
        compiler_params=_params("parallel"), name="merge_out",
    )(x2, h, ya, yb, yc, w_gate, b_gate, w_br, w_out)


FF_CHUNK = 1024


def _mlp_kernel(x_ref, gn_ref, w1_ref, w2_ref, gnext_ref, *out_refs):
    x = x_ref[...]
    h2 = _rms(x, gn_ref[...]).astype(BF16)
    acc = x
    for c in range(D_FF // FF_CHUNK):
        a = jnp.dot(h2, w1_ref[:, c * FF_CHUNK:(c + 1) * FF_CHUNK], preferred_element_type=F32)
        a = jnp.square(jnp.maximum(a, 0.0))
        acc = acc + jnp.dot(a.astype(BF16), w2_ref[c * FF_CHUNK:(c + 1) * FF_CHUNK, :],
                            preferred_element_type=F32)
    normed = _rms(acc, gnext_ref[...])
    if len(out_refs) == 2:
        out_refs[0][...] = acc
        out_refs[1][...] = normed.astype(out_refs[1].dtype)
    else:
        out_refs[0][...] = normed.astype(out_refs[0].dtype)


def _mlp(x2, g_mlp, w1, w2, g_next, last):
    T = x2.shape[0]
    tm = min(TM_DENSE, T)
    row = pl.BlockSpec((tm, D_MODEL), lambda i: (i, 0))
    if last:
        out_shape, out_specs = jax.ShapeDtypeStruct((T, D_MODEL), F32), row
    else:
        out_shape = (jax.ShapeDtypeStruct((T, D_MODEL), F32), jax.ShapeDtypeStruct((T, D_MODEL), BF16))
        out_specs = (row, row)
    return pl.pallas_call(
        _mlp_kernel, out_shape=out_shape, grid=(T // tm,),
        in_specs=[row, _const_spec((1, D_MODEL)), _const_spec(w1.shape), _const_spec(w2.shape),
                  _const_spec((1, D_MODEL))],
        out_specs=out_specs,
        compiler_params=_params("parallel"), name="mlp",
    )(x2, g_mlp.reshape(1, D_MODEL), w1, w2, g_next.reshape(1, D_MODEL))


def kernel(x, positions, norm_mix, w_in, rel_bias, lambdas, sub_gain, b_forget, w_br_a, w_br_b, w_br_c,
           w_gate, b_gate, w_out, norm_mlp, w_ff1, w_ff2, final_norm):
    B, S, D = x.shape
    T = B * S
    depth = w_in.shape[0]
    n_main = 9 * BRANCH_WIDTH
    q_scale = HEAD_DIM ** -0.5 * LOG2E
    col_scale = jnp.ones((n_main,), F32)
    for grp in (0, 3, 6):
        col_scale = col_scale.at[grp * BRANCH_WIDTH:(grp + 1) * BRANCH_WIDTH].set(q_scale)

    cos_t, sin_t = _rope_tables(positions)
    x2 = x.reshape(T, D)
    h = None
    out = None
    for l in range(depth):
        lam_init = 0.8 - 0.6 * math.exp(-0.3 * l)
        w_all = w_in[l, :, :n_main] * col_scale[None, :]
        group = lambda g: w_all[:, g * BRANCH_WIDTH:(g + 1) * BRANCH_WIDTH]
        w_main = jnp.concatenate([group(g) for g in ROW_GROUPS], axis=1).astype(BF16)
        w_ft = jnp.concatenate([group(g) for g in FEAT_GROUPS], axis=1).T.astype(BF16)
        w_fl = jnp.zeros((D, LANES), F32).at[:, :C_HEADS].set(w_in[l, :, n_main:]).astype(BF16)
        b_fl = jnp.zeros((1, LANES), F32).at[0, :C_HEADS].set(b_forget[l].astype(F32))

        if l == 0:
            proj, ft, h = _inproj(x2, w_main, w_ft, cos_t, sin_t, norm_gain=norm_mix[0])
        else:
            proj, ft = _inproj(h, w_main, w_ft, cos_t, sin_t)
        proj = proj.reshape(-1, B, S, LANES)
        f_tok, k_bias = _forget_cumsum(h.reshape(B, S, D), w_fl, b_fl)
        f_row = f_tok[:, :, :C_HEADS].transpose(0, 2, 1).reshape(B, N_PAIRS, 2, S)

        ya = _band_attention(proj, ft, _band_bias(rel_bias[l], min(TQ_FLASH, S)), B, S)
        yb = _diff_attention(proj, ft, lambdas[l], sub_gain[l], lam_init, B, S)
        yc = _forget_attention(proj, ft, f_row, k_bias, B, S)

        w_br = jnp.stack([w_br_a[l], w_br_b[l], w_br_c[l]]).astype(BF16)
        x2 = _merge(x2, h, ya.reshape(GROUP_BLOCKS, T, LANES), yb.reshape(GROUP_BLOCKS, T, LANES),
                    yc.reshape(GROUP_BLOCKS, T, LANES), w_gate[l].astype(BF16),
                    b_gate[l].astype(F32).reshape(1, -1), w_br, w_out[l].astype(BF16))
        last = l == depth - 1
        g_next = final_norm if last else norm_mix[l + 1]
        res = _mlp(x2, norm_mlp[l], w_ff1[l].astype(BF16), w_ff2[l].astype(BF16), g_next, last)
        if last:
            out = res
        else:
            x2, h = res
    return out.reshape(B, S, D)
```

```python
import functools
import math

import jax
import jax.numpy as jnp
import numpy as np
from jax import lax
from jax.experimental import pallas as pl
from jax.experimental.pallas import tpu as pltpu

D_MODEL = 1024
CHUNK = 64
LEFT_CHUNKS = 8
HEAD_DIM = 64
C_HEADS = 8
BRANCH_WIDTH = 512
D_FF = 4 * D_MODEL
REL_CLIP = 128
ROPE_THETA = 500000.0
ROPE_DIM = HEAD_DIM // 4
ROPE_HALF = ROPE_DIM // 2
EPS = 1e-6

LANES = 128
LOG2E = 1.4426950408889634
NEG_BIG = -1e30

TM_DENSE = 1024
TM_FORGET = 512
TQ_FLASH = 512
VMEM_LIMIT = 56 * 1024 * 1024

BF16 = jnp.bfloat16
F32 = jnp.float32


def _params(*sem):
    return pltpu.CompilerParams(dimension_semantics=sem, vmem_limit_bytes=VMEM_LIMIT)


def _const_spec(shape):
    nd = len(shape)
    return pl.BlockSpec(shape, lambda *_: (0,) * nd, pipeline_mode=pl.Buffered(1))


def _rms(x, g):
    return x * lax.rsqrt(jnp.mean(x * x, axis=-1, keepdims=True) + EPS) * g


def _rope_table_kernel(ang_ref, cos_ref, sin_ref):
    ang = ang_ref[...]
    row = lax.broadcasted_iota(jnp.int32, ang.shape, 0)
    sign = jnp.where(row < ROPE_HALF, -1.0, 1.0)
    cos_ref[...] = jnp.cos(ang)
    sin_ref[...] = jnp.sin(ang) * sign


def _rope_tables(positions):
    T = positions.size
    inv_freq = jnp.power(ROPE_THETA, -jnp.arange(0, ROPE_DIM, 2, dtype=F32) / ROPE_DIM)
    ang = jnp.tile(inv_freq, 2)[:, None] * positions.reshape(1, T).astype(F32)
    tt = min(8192, T)
    spec = pl.BlockSpec((ROPE_DIM, tt), lambda i: (0, i))
    return pl.pallas_call(
        _rope_table_kernel,
        out_shape=(jax.ShapeDtypeStruct((ROPE_DIM, T), F32),) * 2,
        grid=(T // tt,), in_specs=[spec], out_specs=(spec, spec),
        compiler_params=_params("parallel"), name="rope_tables",
    )(ang)


ROW_GROUPS = (1, 7)
FEAT_GROUPS = (5, 8, 3, 6, 0, 2, 4)
FEAT_ROPE_GROUPS = (2, 6)
ROW_SLOTS = {0: 0, 1: 2}
KB_FEAT_GROUP, KB_ROW_SLOT = 6, 1
GROUP_BLOCKS = BRANCH_WIDTH // LANES


def _inproj_kernel(*refs, normalize):
    if normalize:
        x_ref, g_ref, w_ref, wft_ref, cos_t_ref, sin_t_ref, o_ref, ft_ref, h_ref = refs
        h = _rms(x_ref[...], g_ref[...]).astype(h_ref.dtype)
        h_ref[...] = h
    else:
        h_ref, w_ref, wft_ref, cos_t_ref, sin_t_ref, o_ref, ft_ref = refs
        h = h_ref[...]
    for c in range(len(ROW_GROUPS)):
        res = jnp.dot(h, w_ref[:, c * BRANCH_WIDTH:(c + 1) * BRANCH_WIDTH],
                      preferred_element_type=F32)
        for s in range(GROUP_BLOCKS):
            o_ref[ROW_SLOTS[c] * GROUP_BLOCKS + s] = res[:, s * LANES:(s + 1) * LANES].astype(o_ref.dtype)
    for c in range(len(FEAT_GROUPS)):
        res_t = lax.dot_general(wft_ref[c * BRANCH_WIDTH:(c + 1) * BRANCH_WIDTH, :], h,
                                (((1,), (1,)), ((), ())), preferred_element_type=F32)
        for s in range(GROUP_BLOCKS):
            blk = res_t[s * LANES:(s + 1) * LANES, :]
            if c in FEAT_ROPE_GROUPS:
                pieces = []
                for base in range(0, LANES, HEAD_DIM):
                    x1 = blk[base:base + ROPE_HALF]
                    x2 = blk[base + ROPE_HALF:base + ROPE_DIM]
                    pieces.append(jnp.concatenate([x1, x2], axis=0) * cos_t_ref[...]
                                  + jnp.concatenate([x2, x1], axis=0) * sin_t_ref[...])
                    pieces.append(blk[base + ROPE_DIM:base + HEAD_DIM])
                blk = jnp.concatenate(pieces, axis=0)
            if c == KB_FEAT_GROUP:
                o_ref[KB_ROW_SLOT * GROUP_BLOCKS + s] = blk.T.astype(o_ref.dtype)
            else:
                ft_ref[c * GROUP_BLOCKS + s] = blk.astype(ft_ref.dtype)


def _inproj(h_or_x, w_main, w_ft, cos_t, sin_t, norm_gain=None):
    T = h_or_x.shape[0]
    tm = min(TM_DENSE, T)
    n_row = (len(ROW_GROUPS) + 1) * GROUP_BLOCKS
    n_ft = (len(FEAT_GROUPS) - 1) * GROUP_BLOCKS
    row = pl.BlockSpec((tm, D_MODEL), lambda i: (i, 0))
    feat = pl.BlockSpec((ROPE_DIM, tm), lambda i: (0, i))
    normalize = norm_gain is not None
    out_shape = [jax.ShapeDtypeStruct((n_row, T, LANES), BF16), jax.ShapeDtypeStruct((n_ft, LANES, T), BF16)]
    out_specs = [pl.BlockSpec((n_row, tm, LANES), lambda i: (0, i, 0)),
                 pl.BlockSpec((n_ft, LANES, tm), lambda i: (0, 0, i))]
    lead_specs, lead_args = [row], [h_or_x]
    if normalize:
        out_shape.append(jax.ShapeDtypeStruct((T, D_MODEL), BF16))
        out_specs.append(row)
        lead_specs.append(_const_spec((1, D_MODEL)))
        lead_args.append(norm_gain.reshape(1, D_MODEL))
    return pl.pallas_call(
        functools.partial(_inproj_kernel, normalize=normalize),
        out_shape=tuple(out_shape), grid=(T // tm,),
        in_specs=lead_specs + [_const_spec(w_main.shape), _const_spec(w_ft.shape), feat, feat],
        out_specs=tuple(out_specs),
        compiler_params=_params("parallel"), name="inproj",
    )(*lead_args, w_main, w_ft, cos_t, sin_t)


N_PAIRS = C_HEADS // 2
FORGET_ROWS = 2


def _split3(x):
    hi = x.astype(BF16)
    r1 = x - hi.astype(F32)
    mid = r1.astype(BF16)
    lo = (r1 - mid.astype(F32)).astype(BF16)
    return hi, mid, lo


def _forget_constants(tri_ref, place_ref):
    tm = tri_ref.shape[0]
    row = lax.broadcasted_iota(jnp.int32, (tm, tm), 0)
    col = lax.broadcasted_iota(jnp.int32, (tm, tm), 1)
    tri_ref[...] = (col <= row).astype(BF16)
    src = lax.broadcasted_iota(jnp.int32, place_ref.shape, 0)
    dst = lax.broadcasted_iota(jnp.int32, place_ref.shape, 1)
    pair, lane = dst // LANES, dst % LANES
    chain, piece = lane // 3, lane % 3
    place = (lane < 2 * 3) & (src == piece * LANES + 2 * pair + chain)
    place_ref[...] = jnp.where(place, -1.0, 0.0).astype(BF16)


def _forget_kernel(h_ref, w_ref, b_ref, f_ref, kbias_ref, tri_ref, place_ref, carry_ref):
    @pl.when((pl.program_id(0) == 0) & (pl.program_id(1) == 0))
    def _():
        _forget_constants(tri_ref, place_ref)

    @pl.when(pl.program_id(1) == 0)
    def _():
        carry_ref[...] = jnp.zeros_like(carry_ref)

    tm = h_ref.shape[1]
    for r in range(h_ref.shape[0]):
        fl = jnp.dot(h_ref[r], w_ref[...], preferred_element_type=F32) + b_ref[...]
        ls = jnp.minimum(fl, 0.0) - jnp.log1p(jnp.exp(-jnp.abs(fl)))
        cs = sum(jnp.dot(tri_ref[...], part, preferred_element_type=F32) for part in _split3(ls))
        f = cs + carry_ref[r]
        carry_ref[r] = f[tm - 1:tm, :]
        f2 = f * LOG2E
        f_ref[r] = f2
        parts = jnp.concatenate(_split3(f2), axis=1)
        placed = jnp.dot(parts, place_ref[...], preferred_element_type=F32)
        for p in range(N_PAIRS):
            kbias_ref[r, p] = placed[:, p * LANES:(p + 1) * LANES].astype(kbias_ref.dtype)


def _forget_cumsum(h3, w_fl, b_fl):
    B, S, _ = h3.shape
    tm = min(TM_FORGET, S)
    rows = FORGET_ROWS if B % FORGET_ROWS == 0 else 1
    return pl.pallas_call(
        _forget_kernel,
        out_shape=(jax.ShapeDtypeStruct((B, S, LANES), F32),
                   jax.ShapeDtypeStruct((B, N_PAIRS, S, LANES), BF16)),
        grid=(B // rows, S // tm),
        in_specs=[pl.BlockSpec((rows, tm, D_MODEL), lambda b, i: (b, i, 0)),
                  _const_spec(w_fl.shape), _const_spec(b_fl.shape)],
        out_specs=(pl.BlockSpec((rows, tm, LANES), lambda b, i: (b, i, 0)),
                   pl.BlockSpec((rows, N_PAIRS, tm, LANES), lambda b, i: (b, 0, i, 0))),
        scratch_shapes=[pltpu.VMEM((tm, tm), BF16), pltpu.VMEM((3 * LANES, N_PAIRS * LANES), BF16),
                        pltpu.VMEM((rows, 1, LANES), F32)],
        compiler_params=_params("arbitrary", "arbitrary"), name="forget_cumsum",
    )(h3, w_fl, b_fl)


ROW_T, ROW_ALPHA, ROW_MPREV = 0, 1, 2
FLASH_UNROLL = 8
SUM_ROWS = 16
Q_SPLIT = 2
BAND_ALL_MASKED = 2
assert FLASH_UNROLL % 2 == 0


def _flash_body(qt_ref, k_ref, vt_ref, fq_ref, kbias_ref, bias_ref, o_ref, finalize, s_refs, st_refs, mrun_ref,
                acc_ref, *, tq, chunk_causal, split_values):
    band = bias_ref is not None

    def key_rows(u, own_block):
        if not band or Q_SPLIT != 2:
            return slice(0, tk)
        first_half = units[u][1] == 0
        if own_block:
            return slice(0, tk // 2) if first_half else slice(0, tk)
        return slice(0, tk) if first_half else slice(tk // 2, tk)
    tk = tq
    nq = qt_ref.shape[2] // tq
    width = tq // Q_SPLIT
    units = [(c, n) for c in range(2) for n in range(Q_SPLIT)]
    feat = lax.broadcasted_iota(jnp.int32, (LANES, 1), 0)
    own = [feat < HEAD_DIM, feat >= HEAD_DIM]

    def queries(qi):
        q_off = pl.multiple_of(qi * tq, tq)
        qt = qt_ref[0, :, pl.ds(q_off, tq)]
        qs = [jnp.where(own[c], qt, jnp.zeros_like(qt)) for c in range(2)]
        if kbias_ref is not None:
            qs = [jnp.concatenate([qs[c], ((feat >= 3 * c) & (feat < 3 * c + 3)).astype(qt.dtype)
                                   * jnp.ones_like(qt)], axis=0) for c in range(2)]
        shifts = [None if fq_ref is None else fq_ref[0, 0, c:c + 1, pl.ds(q_off, tq)] for c in range(2)]
        cols = lambda x, n: None if x is None else x[:, n * width:(n + 1) * width]
        return [(cols(qs[c], n), cols(shifts[c], n)) for c, n in units]

    def stats(shift, col_max, m_prev):
        if shift is not None:
            col_max = col_max + shift
        m_new = jnp.maximum(m_prev, col_max)
        alpha = jnp.exp2(m_prev - m_new)
        t = m_new if shift is None else m_new - shift
        return m_new, alpha, t

    def keys(j):
        off = pl.multiple_of(j * tk, tk)
        k = k_ref[0, 0, pl.ds(off, tk), :]
        if kbias_ref is not None:
            k = jnp.concatenate([k, kbias_ref[0, 0, pl.ds(off, tk), :]], axis=1)
        return k

    def values(j):
        off = pl.multiple_of(j * tk, tk)
        vt = vt_ref[0, :, pl.ds(off, tk)]
        ones = jnp.ones((SUM_ROWS, tk), vt.dtype)
        if split_values:
            return [jnp.concatenate([vt[c * HEAD_DIM:(c + 1) * HEAD_DIM], ones], axis=0) for c in range(2)]
        return [jnp.concatenate([vt, ones], axis=0)] * 2

    def scores(u, k, slot, query, first=False, own_block=True, table=None):
        q_u, shift = query[u]
        rows = key_rows(u, own_block)
        s = jnp.dot(k[rows], q_u, preferred_element_type=F32)
        if band:
            c, n = units[u]
            s = s + bias_ref[0, c, int(own_block) if table is None else table, rows, n * width:(n + 1) * width]
        s_refs[slot][u, rows] = s
        m_prev = jnp.full((1, width), NEG_BIG, F32) if first else mrun_ref[u]
        m_new, alpha, t = stats(shift, jnp.max(s, axis=0, keepdims=True), m_prev)
        mrun_ref[u] = m_new
        st_refs[slot][u, ROW_T:ROW_T + 1] = t
        st_refs[slot][u, ROW_ALPHA:ROW_ALPHA + 1] = alpha
        st_refs[slot][u, ROW_MPREV:ROW_MPREV + 1] = m_prev

    def update(u, vts, slot, mask, query, own_block=True):
        rows = key_rows(u, own_block)
        s = s_refs[slot][u, rows]
        if mask is None:
            t = st_refs[slot][u, ROW_T:ROW_T + 1]
            alpha = st_refs[slot][u, ROW_ALPHA:ROW_ALPHA + 1]
        else:
            n = units[u][1]
            s = jnp.where(mask[:, n * width:(n + 1) * width], s, NEG_BIG)
            _, alpha, t = stats(query[u][1], jnp.max(s, axis=0, keepdims=True),
                                st_refs[slot][u, ROW_MPREV:ROW_MPREV + 1])
        p = jnp.exp2(s - t).astype(BF16)
        acc_ref[u] = alpha * acc_ref[u] + jnp.dot(vts[units[u][0]][:, rows], p, preferred_element_type=F32)

    key = lax.broadcasted_iota(jnp.int32, (tk, tq), 0)
    qry = lax.broadcasted_iota(jnp.int32, (tk, tq), 1)
    if band:
        mask = None
    elif chunk_causal:
        mask = (key // CHUNK) <= (qry // CHUNK)
    else:
        mask = key <= qry

    def run(j0, n_full, query):
        for b in range(n_full):
            k, vts = keys(j0 + b + 1), values(jnp.maximum(j0 + b, 0))
            for u in range(len(units)):
                scores(u, k, (b + 1) % 2, query)
                update(u, vts, b % 2, None, query, own_block=not band)

    k0 = keys(0)
    first_query = queries(0)
    for u in range(len(units)):
        if band:
            scores(u, k0, 0, first_query, first=True, own_block=False, table=BAND_ALL_MASKED)
        else:
            scores(u, k0, 0, first_query, first=True)

    def query_block(qi, carry):
        query = queries(qi)
        acc_ref[...] = jnp.zeros(acc_ref.shape, F32)

        def tail(rem):
            run(qi - rem, rem, query)
            nxt = jnp.minimum(qi + 1, nq - 1)
            vts, k_first = values(qi), keys(jnp.maximum(nxt - 1, 0) if band else 0)
            next_query = queries(nxt)
            for u in range(len(units)):
                update(u, vts, rem % 2, mask, query)
                scores(u, k_first, 0, next_query, first=True, own_block=False)
            o_ref[0, 0, pl.ds(pl.multiple_of(qi * tq, tq), tq), :] = finalize(acc_ref).astype(o_ref.dtype)

        if band:
            tail(1)
        else:
            def group(jj, inner):
                run(jj * FLASH_UNROLL, FLASH_UNROLL, query)
                return inner

            lax.fori_loop(0, qi // FLASH_UNROLL, group, 0)
            for rem in range(FLASH_UNROLL):
                pl.when(qi % FLASH_UNROLL == rem)(functools.partial(tail, rem))
        return carry

    lax.fori_loop(0, nq, query_block, 0)


def _diff_kernel(qt_ref, k_ref, vt_ref, lam_ref, gain_ref, o_ref, s0_ref, s1_ref, st0_ref, st1_ref,
                 mrun_ref, acc_ref, *, tq, lam_init):
    lf = lam_ref[...]
    lam = (jnp.exp(jnp.sum(lf[0:1] * lf[1:2], axis=-1, keepdims=True))
           - jnp.exp(jnp.sum(lf[2:3] * lf[3:4], axis=-1, keepdims=True)) + lam_init)
    gain = gain_ref[...]

    def finalize(acc):
        chain = lambda c: jnp.concatenate([acc[c * Q_SPLIT + n] for n in range(Q_SPLIT)], axis=1)
        maps = [a[:LANES] / a[LANES:LANES + 1] for a in (chain(0), chain(1))]
        o_t = maps[0] - lam * maps[1]
        return _rms(o_t.T, gain) * (1.0 - lam_init)

    _flash_body(qt_ref, k_ref, vt_ref, None, None, None, o_ref, finalize, (s0_ref, s1_ref), (st0_ref, st1_ref),
                mrun_ref, acc_ref, tq=tq, chunk_causal=True, split_values=False)


def _head_pair_output(acc):
    chain = lambda c: jnp.concatenate([acc[c * Q_SPLIT + n] for n in range(Q_SPLIT)], axis=1)
    heads = [a[:HEAD_DIM] / a[HEAD_DIM:HEAD_DIM + 1] for a in (chain(0), chain(1))]
    return jnp.concatenate(heads, axis=0).T


def _forget_attn_kernel(qt_ref, k_ref, vt_ref, fq_ref, kbias_ref, o_ref, s0_ref, s1_ref, st0_ref, st1_ref,
                        mrun_ref, acc_ref, *, tq):
    _flash_body(qt_ref, k_ref, vt_ref, fq_ref, kbias_ref, None, o_ref, _head_pair_output, (s0_ref, s1_ref),
                (st0_ref, st1_ref), mrun_ref, acc_ref, tq=tq, chunk_causal=False, split_values=True)


def _band_kernel(qt_ref, k_ref, vt_ref, bias_ref, o_ref, s0_ref, s1_ref, st0_ref, st1_ref, mrun_ref, acc_ref, *,
                 tq):
    _flash_body(qt_ref, k_ref, vt_ref, None, None, bias_ref, o_ref, _head_pair_output, (s0_ref, s1_ref),
                (st0_ref, st1_ref), mrun_ref, acc_ref, tq=tq, chunk_causal=False, split_values=True)


def _flash_scratch(tq, acc_rows):
    n_units, width = 2 * Q_SPLIT, tq // Q_SPLIT
    return [pltpu.VMEM((n_units, tq, width), F32), pltpu.VMEM((n_units, tq, width), F32),
            pltpu.VMEM((n_units, 3, width), F32), pltpu.VMEM((n_units, 3, width), F32),
            pltpu.VMEM((n_units, 1, width), F32), pltpu.VMEM((n_units, acc_rows, width), F32)]


def _flash_specs(qt_block, k_block, vt_block, S):
    return [pl.BlockSpec((1, LANES, S), lambda b, p: (qt_block + p, 0, b)),
            pl.BlockSpec((1, 1, S, LANES), lambda b, p: (k_block + p, b, 0, 0)),
            pl.BlockSpec((1, LANES, S), lambda b, p: (vt_block + p, 0, b))]


def _diff_attention(proj, ft, lambdas, sub_gain, lam_init, B, S):
    tq = min(TQ_FLASH, S)
    return pl.pallas_call(
        functools.partial(_diff_kernel, tq=tq, lam_init=lam_init),
        out_shape=jax.ShapeDtypeStruct((GROUP_BLOCKS, B, S, LANES), BF16),
        grid=(B, GROUP_BLOCKS),
        in_specs=_flash_specs(2 * GROUP_BLOCKS, GROUP_BLOCKS, 0, S) + [
            _const_spec(lambdas.shape), _const_spec((1, LANES))],
        out_specs=pl.BlockSpec((1, 1, S, LANES), lambda b, p: (p, b, 0, 0)),
        scratch_shapes=_flash_scratch(tq, LANES + SUM_ROWS),
        compiler_params=_params("parallel", "parallel"), name="diff_attention",
    )(ft, proj, ft, lambdas.astype(F32), sub_gain.astype(F32).reshape(1, LANES))


def _forget_attention(proj, ft, f_row, k_bias, B, S):
    tq = min(TQ_FLASH, S)
    return pl.pallas_call(
        functools.partial(_forget_attn_kernel, tq=tq),
        out_shape=jax.ShapeDtypeStruct((GROUP_BLOCKS, B, S, LANES), BF16),
        grid=(B, GROUP_BLOCKS),
        in_specs=_flash_specs(3 * GROUP_BLOCKS, 2 * GROUP_BLOCKS, GROUP_BLOCKS, S) + [
            pl.BlockSpec((1, 1, 2, S), lambda b, p: (b, p, 0, 0)),
            pl.BlockSpec((1, 1, S, LANES), lambda b, p: (b, p, 0, 0))],
        out_specs=pl.BlockSpec((1, 1, S, LANES), lambda b, p: (p, b, 0, 0)),
        scratch_shapes=_flash_scratch(tq, HEAD_DIM + SUM_ROWS),
        compiler_params=_params("parallel", "parallel"), name="forget_attention",
    )(ft, proj, ft, f_row, k_bias)


def _band_attention(proj, ft, bias, B, S):
    tq = min(TQ_FLASH, S)
    return pl.pallas_call(
        functools.partial(_band_kernel, tq=tq),
        out_shape=jax.ShapeDtypeStruct((GROUP_BLOCKS, B, S, LANES), BF16),
        grid=(B, GROUP_BLOCKS),
        in_specs=_flash_specs(4 * GROUP_BLOCKS, 0, 5 * GROUP_BLOCKS, S) + [
            pl.BlockSpec((1,) + bias.shape[1:], lambda b, p: (p, 0, 0, 0, 0))],
        out_specs=pl.BlockSpec((1, 1, S, LANES), lambda b, p: (p, b, 0, 0)),
        scratch_shapes=_flash_scratch(tq, HEAD_DIM + SUM_ROWS),
        compiler_params=_params("parallel", "parallel"), name="band_attention",
    )(ft, proj, ft, bias)


def _band_bias(rel_bias, tq):
    n_keys = 2 * tq
    period = n_keys + tq
    u = np.arange(period)
    diag = np.where(u < n_keys, u, u - period)
    rel = np.clip(diag - tq, -REL_CLIP, REL_CLIP) + REL_CLIP
    profile = rel_bias.astype(F32)[:, rel] * LOG2E
    heads = profile.shape[0]
    rows = min(tq, 256)
    pieces = []
    for first in range(0, tq, rows):
        shifted = jnp.roll(profile, first, axis=1)
        piece = jnp.tile(shifted, (1, rows))[:, :rows * (period - 1)]
        pieces.append(piece.reshape(heads, rows, period - 1)[:, :, :n_keys])
    table = jnp.concatenate(pieces, axis=1)
    q_chunk = tq // CHUNK + np.arange(tq)[:, None] // CHUNK
    k_chunk = np.arange(n_keys)[None, :] // CHUNK
    valid = (k_chunk <= q_chunk) & (k_chunk >= q_chunk - LEFT_CHUNKS)
    table = jnp.where(jnp.asarray(valid)[None], table, NEG_BIG).transpose(0, 2, 1)
    table = table.reshape(heads // 2, 2, 2, tq, tq)
    return jnp.concatenate([table, jnp.full_like(table[:, :, :1], NEG_BIG)], axis=2)


def _merge_kernel(x_ref, h_ref, ya_ref, yb_ref, yc_ref, wg_ref, bg_ref, wbr_ref, wo_ref, o_ref):
    h = h_ref[...]
    merged = None
    for b, y_ref in enumerate((ya_ref, yb_ref, yc_ref)):
        y = jnp.concatenate([y_ref[i] for i in range(y_ref.shape[0])], axis=1)
        t = jnp.dot(y, wbr_ref[b], preferred_element_type=F32)
        z = jnp.dot(h, wg_ref[:, b * D_MODEL:(b + 1) * D_MODEL], preferred_element_type=F32)
        g = jax.nn.sigmoid(z + bg_ref[:, b * D_MODEL:(b + 1) * D_MODEL])
        merged = g * t if merged is None else merged + g * t
    o_ref[...] = x_ref[...] + jnp.dot(merged.astype(BF16), wo_ref[...], preferred_element_type=F32)


def _merge(x2, h, ya, yb, yc, w_gate, b_gate, w_br, w_out):
    T = x2.shape[0]
    tm = min(TM_DENSE, T)
    nblk = ya.shape[0]
    row = pl.BlockSpec((tm, D_MODEL), lambda i: (i, 0))
    yspec = pl.BlockSpec((nblk, tm, LANES), lambda i: (0, i, 0))
    return pl.pallas_call(
        _merge_kernel,
        out_shape=jax.ShapeDtypeStruct((T, D_MODEL), F32),
        grid=(T // tm,),
        in_specs=[row, row, yspec, yspec, yspec, _const_spec(w_gate.shape), _const_spec(b_gate.shape),
                  _const_spec(w_br.shape), _const_spec(w_out.shape)],
        out_specs=row,
        compiler_params=_params("parallel"), name="merge_out",
    )(x2, h, ya, yb, yc, w_gate, b_gate, w_br, w_out)


FF_CHUNK = 1024


def _mlp_kernel(x_ref, gn_ref, w1_ref, w2_ref, gnext_ref, *out_refs):
    x = x_ref[...]
    h2 = _rms(x, gn_ref[...]).astype(BF16)
    acc = x
    for c in range(D_FF // FF_CHUNK):
        a = jnp.dot(h2, w1_ref[:, c * FF_CHUNK:(c + 1) * FF_CHUNK], preferred_element_type=F32)
        a = jnp.square(jnp.maximum(a, 0.0))
        acc = acc + jnp.dot(a.astype(BF16), w2_ref[c * FF_CHUNK:(c + 1) * FF_CHUNK, :],
                            preferred_element_type=F32)
    normed = _rms(acc, gnext_ref[...])
    if len(out_refs) == 2:
        out_refs[0][...] = acc
        out_refs[1][...] = normed.astype(out_refs[1].dtype)
    else:
        out_refs[0][...] = normed.astype(out_refs[0].dtype)


def _mlp(x2, g_mlp, w1, w2, g_next, last):
    T = x2.shape[0]
    tm = min(TM_DENSE, T)
    row = pl.BlockSpec((tm, D_MODEL), lambda i: (i, 0))
    if last:
        out_shape, out_specs = jax.ShapeDtypeStruct((T, D_MODEL), F32), row
    else:
        out_shape = (jax.ShapeDtypeStruct((T, D_MODEL), F32), jax.ShapeDtypeStruct((T, D_MODEL), BF16))
        out_specs = (row, row)
    return pl.pallas_call(
        _mlp_kernel, out_shape=out_shape, grid=(T // tm,),
        in_specs=[row, _const_spec((1, D_MODEL)), _const_spec(w1.shape), _const_spec(w2.shape),
                  _const_spec((1, D_MODEL))],
        out_specs=out_specs,
        compiler_params=_params("parallel"), name="mlp",
    )(x2, g_mlp.reshape(1, D_MODEL), w1, w2, g_next.reshape(1, D_MODEL))


def kernel(x, positions, norm_mix, w_in, rel_bias, lambdas, sub_gain, b_forget, w_br_a, w_br_b, w_br_c,
           w_gate, b_gate, w_out, norm_mlp, w_ff1, w_ff2, final_norm):
    B, S, D = x.shape
    T = B * S
    depth = w_in.shape[0]
    n_main = 9 * BRANCH_WIDTH
    q_scale = HEAD_DIM ** -0.5 * LOG2E
    col_scale = jnp.ones((n_main,), F32)
    for grp in (0, 3, 6):
        col_scale = col_scale.at[grp * BRANCH_WIDTH:(grp + 1) * BRANCH_WIDTH].set(q_scale)

    cos_t, sin_t = _rope_tables(positions)
    x2 = x.reshape(T, D)
    h = None
    out = None
    for l in range(depth):
        lam_init = 0.8 - 0.6 * math.exp(-0.3 * l)
        w_all = w_in[l, :, :n_main] * col_scale[None, :]
        group = lambda g: w_all[:, g * BRANCH_WIDTH:(g + 1) * BRANCH_WIDTH]
        w_main = jnp.concatenate([group(g) for g in ROW_GROUPS], axis=1).astype(BF16)
        w_ft = jnp.concatenate([group(g) for g in FEAT_GROUPS], axis=1).T.astype(BF16)
        w_fl = jnp.zeros((D, LANES), F32).at[:, :C_HEADS].set(w_in[l, :, n_main:]).astype(BF16)
        b_fl = jnp.zeros((1, LANES), F32).at[0, :C_HEADS].set(b_forget[l].astype(F32))

        if l == 0:
            proj, ft, h = _inproj(x2, w_main, w_ft, cos_t, sin_t, norm_gain=norm_mix[0])
        else:
            proj, ft = _inproj(h, w_main, w_ft, cos_t, sin_t)
        proj = proj.reshape(-1, B, S, LANES)
        f_tok, k_bias = _forget_cumsum(h.reshape(B, S, D), w_fl, b_fl)
        f_row = f_tok[:, :, :C_HEADS].transpose(0, 2, 1).reshape(B, N_PAIRS, 2, S)

        ya = _band_attention(proj, ft, _band_bias(rel_bias[l], min(TQ_FLASH, S)), B, S)
        yb = _diff_attention(proj, ft, lambdas[l], sub_gain[l], lam_init, B, S)
        yc = _forget_attention(proj, ft, f_row, k_bias, B, S)

        w_br = jnp.stack([w_br_a[l], w_br_b[l], w_br_c[l]]).astype(BF16)
        x2 = _merge(x2, h, ya.reshape(GROUP_BLOCKS, T, LANES), yb.reshape(GROUP_BLOCKS, T, LANES),
                    yc.reshape(GROUP_BLOCKS, T, LANES), w_gate[l].astype(BF16),
                    b_gate[l].astype(F32).reshape(1, -1), w_br, w_out[l].astype(BF16))
        last = l == depth - 1
        g_next = final_norm if last else norm_mix[l + 1]
        res = _mlp(x2, norm_mlp[l], w_ff1[l].astype(BF16), w_ff2[l].astype(BF16), g_next, last)
        if last:
            out = res
        else:
            x2, h = res
    return out.reshape(B, S, D)
```

```python
import functools
import math

import jax
import jax.numpy as jnp
import numpy as np
from jax import lax
from jax.experimental import pallas as pl
from jax.experimental.pallas import tpu as pltpu

D_MODEL = 1024
CHUNK = 64
LEFT_CHUNKS = 8
HEAD_DIM = 64
C_HEADS = 8
BRANCH_WIDTH = 512
D_FF = 4 * D_MODEL
REL_CLIP = 128
ROPE_THETA = 500000.0
ROPE_DIM = HEAD_DIM // 4
ROPE_HALF = ROPE_DIM // 2
EPS = 1e-6

LANES = 128
LOG2E = 1.4426950408889634
NEG_BIG = -1e30

TM_DENSE = 1024
TM_FORGET = 512
TQ_FLASH = 512
VMEM_LIMIT = 56 * 1024 * 1024

BF16 = jnp.bfloat16
F32 = jnp.float32


def _params(*sem):
    return pltpu.CompilerParams(dimension_semantics=sem, vmem_limit_bytes=VMEM_LIMIT)


def _const_spec(shape):
    nd = len(shape)
    return pl.BlockSpec(shape, lambda *_: (0,) * nd, pipeline_mode=pl.Buffered(1))


def _rms(x, g):
    return x * lax.rsqrt(jnp.mean(x * x, axis=-1, keepdims=True) + EPS) * g


def _rope_table_kernel(ang_ref, cos_ref, sin_ref):
    ang = ang_ref[...]
    row = lax.broadcasted_iota(jnp.int32, ang.shape, 0)
    sign = jnp.where(row < ROPE_HALF, -1.0, 1.0)
    cos_ref[...] = jnp.cos(ang)
    sin_ref[...] = jnp.sin(ang) * sign


def _rope_tables(positions):
    T = positions.size
    inv_freq = jnp.power(ROPE_THETA, -jnp.arange(0, ROPE_DIM, 2, dtype=F32) / ROPE_DIM)
    ang = jnp.tile(inv_freq, 2)[:, None] * positions.reshape(1, T).astype(F32)
    tt = min(8192, T)
    spec = pl.BlockSpec((ROPE_DIM, tt), lambda i: (0, i))
    return pl.pallas_call(
        _rope_table_kernel,
        out_shape=(jax.ShapeDtypeStruct((ROPE_DIM, T), F32),) * 2,
        grid=(T // tt,), in_specs=[spec], out_specs=(spec, spec),
        compiler_params=_params("parallel"), name="rope_tables",
    )(ang)


ROW_GROUPS = (1, 7)
FEAT_GROUPS = (5, 8, 3, 6, 0, 2, 4)
FEAT_ROPE_GROUPS = (2, 6)
ROW_SLOTS = {0: 0, 1: 2}
KB_FEAT_GROUP, KB_ROW_SLOT = 6, 1
GROUP_BLOCKS = BRANCH_WIDTH // LANES


def _inproj_kernel(*refs, normalize):
    if normalize:
        x_ref, g_ref, w_ref, wft_ref, cos_t_ref, sin_t_ref, o_ref, ft_ref, h_ref = refs
        h = _rms(x_ref[...], g_ref[...]).astype(h_ref.dtype)
        h_ref[...] = h
    else:
        h_ref, w_ref, wft_ref, cos_t_ref, sin_t_ref, o_ref, ft_ref = refs
        h = h_ref[...]
    for c in range(len(ROW_GROUPS)):
        res = jnp.dot(h, w_ref[:, c * BRANCH_WIDTH:(c + 1) * BRANCH_WIDTH],
                      preferred_element_type=F32)
        for s in range(GROUP_BLOCKS):
            o_ref[ROW_SLOTS[c] * GROUP_BLOCKS + s] = res[:, s * LANES:(s + 1) * LANES].astype(o_ref.dtype)
    for c in range(len(FEAT_GROUPS)):
        res_t = lax.dot_general(wft_ref[c * BRANCH_WIDTH:(c + 1) * BRANCH_WIDTH, :], h,
                                (((1,), (1,)), ((), ())), preferred_element_type=F32)
        for s in range(GROUP_BLOCKS):
            blk = res_t[s * LANES:(s + 1) * LANES, :]
            if c in FEAT_ROPE_GROUPS:
                pieces = []
                for base in range(0, LANES, HEAD_DIM):
                    x1 = blk[base:base + ROPE_HALF]
                    x2 = blk[base + ROPE_HALF:base + ROPE_DIM]
                    pieces.append(jnp.concatenate([x1, x2], axis=0) * cos_t_ref[...]
                                  + jnp.concatenate([x2, x1], axis=0) * sin_t_ref[...])
                    pieces.append(blk[base + ROPE_DIM:base + HEAD_DIM])
                blk = jnp.concatenate(pieces, axis=0)
            if c == KB_FEAT_GROUP:
                o_ref[KB_ROW_SLOT * GROUP_BLOCKS + s] = blk.T.astype(o_ref.dtype)
            else:
                ft_ref[c * GROUP_BLOCKS + s] = blk.astype(ft_ref.dtype)


def _inproj(h_or_x, w_main, w_ft, cos_t, sin_t, norm_gain=None):
    T = h_or_x.shape[0]
    tm = min(TM_DENSE, T)
    n_row = (len(ROW_GROUPS) + 1) * GROUP_BLOCKS
    n_ft = (len(FEAT_GROUPS) - 1) * GROUP_BLOCKS
    row = pl.BlockSpec((tm, D_MODEL), lambda i: (i, 0))
    feat = pl.BlockSpec((ROPE_DIM, tm), lambda i: (0, i))
    normalize = norm_gain is not None
    out_shape = [jax.ShapeDtypeStruct((n_row, T, LANES), BF16), jax.ShapeDtypeStruct((n_ft, LANES, T), BF16)]
    out_specs = [pl.BlockSpec((n_row, tm, LANES), lambda i: (0, i, 0)),
                 pl.BlockSpec((n_ft, LANES, tm), lambda i: (0, 0, i))]
    lead_specs, lead_args = [row], [h_or_x]
    if normalize:
        out_shape.append(jax.ShapeDtypeStruct((T, D_MODEL), BF16))
        out_specs.append(row)
        lead_specs.append(_const_spec((1, D_MODEL)))
        lead_args.append(norm_gain.reshape(1, D_MODEL))
    return pl.pallas_call(
        functools.partial(_inproj_kernel, normalize=normalize),
        out_shape=tuple(out_shape), grid=(T // tm,),
        in_specs=lead_specs + [_const_spec(w_main.shape), _const_spec(w_ft.shape), feat, feat],
        out_specs=tuple(out_specs),
        compiler_params=_params("parallel"), name="inproj",
    )(*lead_args, w_main, w_ft, cos_t, sin_t)


N_PAIRS = C_HEADS // 2
FORGET_ROWS = 2


def _split3(x):
    hi = x.astype(BF16)
    r1 = x - hi.astype(F32)
    mid = r1.astype(BF16)
    lo = (r1 - mid.astype(F32)).astype(BF16)
    return hi, mid, lo


def _forget_constants(tri_ref, place_ref):
    tm = tri_ref.shape[0]
    row = lax.broadcasted_iota(jnp.int32, (tm, tm), 0)
    col = lax.broadcasted_iota(jnp.int32, (tm, tm), 1)
    tri_ref[...] = (col <= row).astype(BF16)
    src = lax.broadcasted_iota(jnp.int32, place_ref.shape, 0)
    dst = lax.broadcasted_iota(jnp.int32, place_ref.shape, 1)
    pair, lane = dst // LANES, dst % LANES
    chain, piece = lane // 3, lane % 3
    place = (lane < 2 * 3) & (src == piece * LANES + 2 * pair + chain)
    place_ref[...] = jnp.where(place, -1.0, 0.0).astype(BF16)


def _forget_kernel(h_ref, w_ref, b_ref, f_ref, kbias_ref, tri_ref, place_ref, carry_ref):
    @pl.when((pl.program_id(0) == 0) & (pl.program_id(1) == 0))
    def _():
        _forget_constants(tri_ref, place_ref)

    @pl.when(pl.program_id(1) == 0)
    def _():
        carry_ref[...] = jnp.zeros_like(carry_ref)

    tm = h_ref.shape[1]
    for r in range(h_ref.shape[0]):
        fl = jnp.dot(h_ref[r], w_ref[...], preferred_element_type=F32) + b_ref[...]
        ls = jnp.minimum(fl, 0.0) - jnp.log1p(jnp.exp(-jnp.abs(fl)))
        cs = sum(jnp.dot(tri_ref[...], part, preferred_element_type=F32) for part in _split3(ls))
        f = cs + carry_ref[r]
        carry_ref[r] = f[tm - 1:tm, :]
        f2 = f * LOG2E
        f_ref[r] = f2
        parts = jnp.concatenate(_split3(f2), axis=1)
        placed = jnp.dot(parts, place_ref[...], preferred_element_type=F32)
        for p in range(N_PAIRS):
            kbias_ref[r, p] = placed[:, p * LANES:(p + 1) * LANES].astype(kbias_ref.dtype)


def _forget_cumsum(h3, w_fl, b_fl):
    B, S, _ = h3.shape
    tm = min(TM_FORGET, S)
    rows = FORGET_ROWS if B % FORGET_ROWS == 0 else 1
    return pl.pallas_call(
        _forget_kernel,
        out_shape=(jax.ShapeDtypeStruct((B, S, LANES), F32),
                   jax.ShapeDtypeStruct((B, N_PAIRS, S, LANES), BF16)),
        grid=(B // rows, S // tm),
        in_specs=[pl.BlockSpec((rows, tm, D_MODEL), lambda b, i: (b, i, 0)),
                  _const_spec(w_fl.shape), _const_spec(b_fl.shape)],
        out_specs=(pl.BlockSpec((rows, tm, LANES), lambda b, i: (b, i, 0)),
                   pl.BlockSpec((rows, N_PAIRS, tm, LANES), lambda b, i: (b, 0, i, 0))),
        scratch_shapes=[pltpu.VMEM((tm, tm), BF16), pltpu.VMEM((3 * LANES, N_PAIRS * LANES), BF16),
                        pltpu.VMEM((rows, 1, LANES), F32)],
        compiler_params=_params("arbitrary", "arbitrary"), name="forget_cumsum",
    )(h3, w_fl, b_fl)


ROW_T, ROW_ALPHA, ROW_MPREV = 0, 1, 2
FLASH_UNROLL = 12
SUM_ROWS = 16
Q_SPLIT = 2
BAND_ALL_MASKED = 2
assert FLASH_UNROLL % 2 == 0


def _flash_body(qt_ref, k_ref, vt_ref, fq_ref, kbias_ref, bias_ref, o_ref, finalize, s_refs, st_refs, mrun_ref,
                acc_ref, *, tq, chunk_causal, split_values):
    band = bias_ref is not None

    def key_rows(u, own_block):
        if not band or Q_SPLIT != 2:
            return slice(0, tk)
        first_half = units[u][1] == 0
        if own_block:
            return slice(0, tk // 2) if first_half else slice(0, tk)
        return slice(0, tk) if first_half else slice(tk // 2, tk)
    tk = tq
    nq = qt_ref.shape[2] // tq
    width = tq // Q_SPLIT
    units = [(c, n) for c in range(2) for n in range(Q_SPLIT)]
    feat = lax.broadcasted_iota(jnp.int32, (LANES, 1), 0)
    own = [feat < HEAD_DIM, feat >= HEAD_DIM]

    def queries(qi):
        q_off = pl.multiple_of(qi * tq, tq)
        qt = qt_ref[0, :, pl.ds(q_off, tq)]
        qs = [jnp.where(own[c], qt, jnp.zeros_like(qt)) for c in range(2)]
        if kbias_ref is not None:
            qs = [jnp.concatenate([qs[c], ((feat >= 3 * c) & (feat < 3 * c + 3)).astype(qt.dtype)
                                   * jnp.ones_like(qt)], axis=0) for c in range(2)]
        shifts = [None if fq_ref is None else fq_ref[0, 0, c:c + 1, pl.ds(q_off, tq)] for c in range(2)]
        cols = lambda x, n: None if x is None else x[:, n * width:(n + 1) * width]
        return [(cols(qs[c], n), cols(shifts[c], n)) for c, n in units]

    def stats(shift, col_max, m_prev):
        if shift is not None:
            col_max = col_max + shift
        m_new = jnp.maximum(m_prev, col_max)
        alpha = jnp.exp2(m_prev - m_new)
        t = m_new if shift is None else m_new - shift
        return m_new, alpha, t

    def keys(j):
        off = pl.multiple_of(j * tk, tk)
        k = k_ref[0, 0, pl.ds(off, tk), :]
        if kbias_ref is not None:
            k = jnp.concatenate([k, kbias_ref[0, 0, pl.ds(off, tk), :]], axis=1)
        return k

    def values(j):
        off = pl.multiple_of(j * tk, tk)
        vt = vt_ref[0, :, pl.ds(off, tk)]
        ones = jnp.ones((SUM_ROWS, tk), vt.dtype)
        if split_values:
            return [jnp.concatenate([vt[c * HEAD_DIM:(c + 1) * HEAD_DIM], ones], axis=0) for c in range(2)]
        return [jnp.concatenate([vt, ones], axis=0)] * 2

    def scores(u, k, slot, query, first=False, own_block=True, table=None):
        q_u, shift = query[u]
        rows = key_rows(u, own_block)
        s = jnp.dot(k[rows], q_u, preferred_element_type=F32)
        if band:
            c, n = units[u]
            s = s + bias_ref[0, c, int(own_block) if table is None else table, rows, n * width:(n + 1) * width]
        s_refs[slot][u, rows] = s
        m_prev = jnp.full((1, width), NEG_BIG, F32) if first else mrun_ref[u]
        m_new, alpha, t = stats(shift, jnp.max(s, axis=0, keepdims=True), m_prev)
        mrun_ref[u] = m_new
        st_refs[slot][u, ROW_T:ROW_T + 1] = t
        st_refs[slot][u, ROW_ALPHA:ROW_ALPHA + 1] = alpha
        st_refs[slot][u, ROW_MPREV:ROW_MPREV + 1] = m_prev

    def update(u, vts, slot, mask, query, own_block=True):
        rows = key_rows(u, own_block)
        s = s_refs[slot][u, rows]
        if mask is None:
            t = st_refs[slot][u, ROW_T:ROW_T + 1]
            alpha = st_refs[slot][u, ROW_ALPHA:ROW_ALPHA + 1]
        else:
            n = units[u][1]
            s = jnp.where(mask[:, n * width:(n + 1) * width], s, NEG_BIG)
            _, alpha, t = stats(query[u][1], jnp.max(s, axis=0, keepdims=True),
                                st_refs[slot][u, ROW_MPREV:ROW_MPREV + 1])
        p = jnp.exp2(s - t).astype(BF16)
        acc_ref[u] = alpha * acc_ref[u] + jnp.dot(vts[units[u][0]][:, rows], p, preferred_element_type=F32)

    key = lax.broadcasted_iota(jnp.int32, (tk, tq), 0)
    qry = lax.broadcasted_iota(jnp.int32, (tk, tq), 1)
    if band:
        mask = None
    elif chunk_causal:
        mask = (key // CHUNK) <= (qry // CHUNK)
    else:
        mask = key <= qry

    def run(j0, n_full, query):
        for b in range(n_full):
            k, vts = keys(j0 + b + 1), values(jnp.maximum(j0 + b, 0))
            for u in range(len(units)):
                scores(u, k, (b + 1) % 2, query)
                update(u, vts, b % 2, None, query, own_block=not band)

    k0 = keys(0)
    first_query = queries(0)
    for u in range(len(units)):
        if band:
            scores(u, k0, 0, first_query, first=True, own_block=False, table=BAND_ALL_MASKED)
        else:
            scores(u, k0, 0, first_query, first=True)

    def query_block(qi, carry):
        query = queries(qi)
        acc_ref[...] = jnp.zeros(acc_ref.shape, F32)

        def tail(rem):
            run(qi - rem, rem, query)
            nxt = jnp.minimum(qi + 1, nq - 1)
            vts, k_first = values(qi), keys(jnp.maximum(nxt - 1, 0) if band else 0)
            next_query = queries(nxt)
            for u in range(len(units)):
                update(u, vts, rem % 2, mask, query)
                scores(u, k_first, 0, next_query, first=True, own_block=False)
            o_ref[0, 0, pl.ds(pl.multiple_of(qi * tq, tq), tq), :] = finalize(acc_ref).astype(o_ref.dtype)

        if band:
            tail(1)
        else:
            def group(jj, inner):
                run(jj * FLASH_UNROLL, FLASH_UNROLL, query)
                return inner

            lax.fori_loop(0, qi // FLASH_UNROLL, group, 0)
            for rem in range(FLASH_UNROLL):
                pl.when(qi % FLASH_UNROLL == rem)(functools.partial(tail, rem))
        return carry

    lax.fori_loop(0, nq, query_block, 0)


def _diff_kernel(qt_ref, k_ref, vt_ref, lam_ref, gain_ref, o_ref, s0_ref, s1_ref, st0_ref, st1_ref,
                 mrun_ref, acc_ref, *, tq, lam_init):
    lf = lam_ref[...]
    lam = (jnp.exp(jnp.sum(lf[0:1] * lf[1:2], axis=-1, keepdims=True))
           - jnp.exp(jnp.sum(lf[2:3] * lf[3:4], axis=-1, keepdims=True)) + lam_init)
    gain = gain_ref[...]

    def finalize(acc):
        chain = lambda c: jnp.concatenate([acc[c * Q_SPLIT + n] for n in range(Q_SPLIT)], axis=1)
        maps = [a[:LANES] / a[LANES:LANES + 1] for a in (chain(0), chain(1))]
        o_t = maps[0] - lam * maps[1]
        return _rms(o_t.T, gain) * (1.0 - lam_init)

    _flash_body(qt_ref, k_ref, vt_ref, None, None, None, o_ref, finalize, (s0_ref, s1_ref), (st0_ref, st1_ref),
                mrun_ref, acc_ref, tq=tq, chunk_causal=True, split_values=False)


def _head_pair_output(acc):
    chain = lambda c: jnp.concatenate([acc[c * Q_SPLIT + n] for n in range(Q_SPLIT)], axis=1)
    heads = [a[:HEAD_DIM] / a[HEAD_DIM:HEAD_DIM + 1] for a in (chain(0), chain(1))]
    return jnp.concatenate(heads, axis=0).T


def _forget_attn_kernel(qt_ref, k_ref, vt_ref, fq_ref, kbias_ref, o_ref, s0_ref, s1_ref, st0_ref, st1_ref,
                        mrun_ref, acc_ref, *, tq):
    _flash_body(qt_ref, k_ref, vt_ref, fq_ref, kbias_ref, None, o_ref, _head_pair_output, (s0_ref, s1_ref),
                (st0_ref, st1_ref), mrun_ref, acc_ref, tq=tq, chunk_causal=False, split_values=True)


def _band_kernel(qt_ref, k_ref, vt_ref, bias_ref, o_ref, s0_ref, s1_ref, st0_ref, st1_ref, mrun_ref, acc_ref, *,
                 tq):
    _flash_body(qt_ref, k_ref, vt_ref, None, None, bias_ref, o_ref, _head_pair_output, (s0_ref, s1_ref),
                (st0_ref, st1_ref), mrun_ref, acc_ref, tq=tq, chunk_causal=False, split_values=True)


def _flash_scratch(tq, acc_rows):
    n_units, width = 2 * Q_SPLIT, tq // Q_SPLIT
    return [pltpu.VMEM((n_units, tq, width), F32), pltpu.VMEM((n_units, tq, width), F32),
            pltpu.VMEM((n_units, 3, width), F32), pltpu.VMEM((n_units, 3, width), F32),
            pltpu.VMEM((n_units, 1, width), F32), pltpu.VMEM((n_units, acc_rows, width), F32)]


def _flash_specs(qt_block, k_block, vt_block, S):
    return [pl.BlockSpec((1, LANES, S), lambda b, p: (qt_block + p, 0, b)),
            pl.BlockSpec((1, 1, S, LANES), lambda b, p: (k_block + p, b, 0, 0)),
            pl.BlockSpec((1, LANES, S), lambda b, p: (vt_block + p, 0, b))]


def _diff_attention(proj, ft, lambdas, sub_gain, lam_init, B, S):
    tq = min(TQ_FLASH, S)
    return pl.pallas_call(
        functools.partial(_diff_kernel, tq=tq, lam_init=lam_init),
        out_shape=jax.ShapeDtypeStruct((GROUP_BLOCKS, B, S, LANES), BF16),
        grid=(B, GROUP_BLOCKS),
        in_specs=_flash_specs(2 * GROUP_BLOCKS, GROUP_BLOCKS, 0, S) + [
            _const_spec(lambdas.shape), _const_spec((1, LANES))],
        out_specs=pl.BlockSpec((1, 1, S, LANES), lambda b, p: (p, b, 0, 0)),
        scratch_shapes=_flash_scratch(tq, LANES + SUM_ROWS),
        compiler_params=_params("parallel", "parallel"), name="diff_attention",
    )(ft, proj, ft, lambdas.astype(F32), sub_gain.astype(F32).reshape(1, LANES))


def _forget_attention(proj, ft, f_row, k_bias, B, S):
    tq = min(TQ_FLASH, S)
    return pl.pallas_call(
        functools.partial(_forget_attn_kernel, tq=tq),
        out_shape=jax.ShapeDtypeStruct((GROUP_BLOCKS, B, S, LANES), BF16),
        grid=(B, GROUP_BLOCKS),
        in_specs=_flash_specs(3 * GROUP_BLOCKS, 2 * GROUP_BLOCKS, GROUP_BLOCKS, S) + [
            pl.BlockSpec((1, 1, 2, S), lambda b, p: (b, p, 0, 0)),
            pl.BlockSpec((1, 1, S, LANES), lambda b, p: (b, p, 0, 0))],
        out_specs=pl.BlockSpec((1, 1, S, LANES), lambda b, p: (p, b, 0, 0)),
        scratch_shapes=_flash_scratch(tq, HEAD_DIM + SUM_ROWS),
        compiler_params=_params("parallel", "parallel"), name="forget_attention",
    )(ft, proj, ft, f_row, k_bias)


def _band_attention(proj, ft, bias, B, S):
    tq = min(TQ_FLASH, S)
    return pl.pallas_call(
        functools.partial(_band_kernel, tq=tq),
        out_shape=jax.ShapeDtypeStruct((GROUP_BLOCKS, B, S, LANES), BF16),
        grid=(B, GROUP_BLOCKS),
        in_specs=_flash_specs(4 * GROUP_BLOCKS, 0, 5 * GROUP_BLOCKS, S) + [
            pl.BlockSpec((1,) + bias.shape[1:], lambda b, p: (p, 0, 0, 0, 0))],
        out_specs=pl.BlockSpec((1, 1, S, LANES), lambda b, p: (p, b, 0, 0)),
        scratch_shapes=_flash_scratch(tq, HEAD_DIM + SUM_ROWS),
        compiler_params=_params("parallel", "parallel"), name="band_attention",
    )(ft, proj, ft, bias)


def _band_bias(rel_bias, tq):
    n_keys = 2 * tq
    period = n_keys + tq
    u = np.arange(period)
    diag = np.where(u < n_keys, u, u - period)
    rel = np.clip(diag - tq, -REL_CLIP, REL_CLIP) + REL_CLIP
    profile = rel_bias.astype(F32)[:, rel] * LOG2E
    heads = profile.shape[0]
    rows = min(tq, 256)
    pieces = []
    for first in range(0, tq, rows):
        shifted = jnp.roll(profile, first, axis=1)
        piece = jnp.tile(shifted, (1, rows))[:, :rows * (period - 1)]
        pieces.append(piece.reshape(heads, rows, period - 1)[:, :, :n_keys])
    table = jnp.concatenate(pieces, axis=1)
    q_chunk = tq // CHUNK + np.arange(tq)[:, None] // CHUNK
    k_chunk = np.arange(n_keys)[None, :] // CHUNK
    valid = (k_chunk <= q_chunk) & (k_chunk >= q_chunk - LEFT_CHUNKS)
    table = jnp.where(jnp.asarray(valid)[None], table, NEG_BIG).transpose(0, 2, 1)
    table = table.reshape(heads // 2, 2, 2, tq, tq)
    return jnp.concatenate([table, jnp.full_like(table[:, :, :1], NEG_BIG)], axis=2)


def _merge_kernel(x_ref, h_ref, ya_ref, yb_ref, yc_ref, wg_ref, bg_ref, wbr_ref, wo_ref, o_ref):
    h = h_ref[...]
    merged = None
    for b, y_ref in enumerate((ya_ref, yb_ref, yc_ref)):
        y = jnp.concatenate([y_ref[i] for i in range(y_ref.shape[0])], axis=1)
        t = jnp.dot(y, wbr_ref[b], preferred_element_type=F32)
        z = jnp.dot(h, wg_ref[:, b * D_MODEL:(b + 1) * D_MODEL], preferred_element_type=F32)
        g = jax.nn.sigmoid(z + bg_ref[:, b * D_MODEL:(b + 1) * D_MODEL])
        merged = g * t if merged is None else merged + g * t
    o_ref[...] = x_ref[...] + jnp.dot(merged.astype(BF16), wo_ref[...], preferred_element_type=F32)


def _merge(x2, h, ya, yb, yc, w_gate, b_gate, w_br, w_out):
    T = x2.shape[0]
    tm = min(TM_DENSE, T)
    nblk = ya.shape[0]
    row = pl.BlockSpec((tm, D_MODEL), lambda i: (i, 0))
    yspec = pl.BlockSpec((nblk, tm, LANES), lambda i: (0, i, 0))
    return pl.pallas_call(
        _merge_kernel,
        out_shape=jax.ShapeDtypeStruct((T, D_MODEL), F32),
        grid=(T // tm,),
        in_specs=[row, row, yspec, yspec, yspec, _const_spec(w_gate.shape), _const_spec(b_gate.shape),
                  _const_spec(w_br.shape), _const_spec(w_out.shape)],
        out_specs=row,
        compiler_params=_params("parallel"), name="merge_out",
    )(x2, h, ya, yb, yc, w_gate, b_gate, w_br, w_out)


FF_CHUNK = 1024


def _mlp_kernel(x_ref, gn_ref, w1_ref, w2_ref, gnext_ref, *out_refs):
    x = x_ref[...]
    h2 = _rms(x, gn_ref[...]).astype(BF16)
    acc = x
    for c in range(D_FF // FF_CHUNK):
        a = jnp.dot(h2, w1_ref[:, c * FF_CHUNK:(c + 1) * FF_CHUNK], preferred_element_type=F32)
        a = jnp.square(jnp.maximum(a, 0.0))
        acc = acc + jnp.dot(a.astype(BF16), w2_ref[c * FF_CHUNK:(c + 1) * FF_CHUNK, :],
                            preferred_element_type=F32)
    normed = _rms(acc, gnext_ref[...])
    if len(out_refs) == 2:
        out_refs[0][...] = acc
        out_refs[1][...] = normed.astype(out_refs[1].dtype)
    else:
        out_refs[0][...] = normed.astype(out_refs[0].dtype)


def _mlp(x2, g_mlp, w1, w2, g_next, last):
    T = x2.shape[0]
    tm = min(TM_DENSE, T)
    row = pl.BlockSpec((tm, D_MODEL), lambda i: (i, 0))
    if last:
        out_shape, out_specs = jax.ShapeDtypeStruct((T, D_MODEL), F32), row
    else:
        out_shape = (jax.ShapeDtypeStruct((T, D_MODEL), F32), jax.ShapeDtypeStruct((T, D_MODEL), BF16))
        out_specs = (row, row)
    return pl.pallas_call(
        _mlp_kernel, out_shape=out_shape, grid=(T // tm,),
        in_specs=[row, _const_spec((1, D_MODEL)), _const_spec(w1.shape), _const_spec(w2.shape),
                  _const_spec((1, D_MODEL))],
        out_specs=out_specs,
        compiler_params=_params("parallel"), name="mlp",
    )(x2, g_mlp.reshape(1, D_MODEL), w1, w2, g_next.reshape(1, D_MODEL))


def kernel(x, positions, norm_mix, w_in, rel_bias, lambdas, sub_gain, b_forget, w_br_a, w_br_b, w_br_c,
           w_gate, b_gate, w_out, norm_mlp, w_ff1, w_ff2, final_norm):
    B, S, D = x.shape
    T = B * S
    depth = w_in.shape[0]
    n_main = 9 * BRANCH_WIDTH
    q_scale = HEAD_DIM ** -0.5 * LOG2E
    col_scale = jnp.ones((n_main,), F32)
    for grp in (0, 3, 6):
        col_scale = col_scale.at[grp * BRANCH_WIDTH:(grp + 1) * BRANCH_WIDTH].set(q_scale)

    cos_t, sin_t = _rope_tables(positions)
    x2 = x.reshape(T, D)
    h = None
    out = None
    for l in range(depth):
        lam_init = 0.8 - 0.6 * math.exp(-0.3 * l)
        w_all = w_in[l, :, :n_main] * col_scale[None, :]
        group = lambda g: w_all[:, g * BRANCH_WIDTH:(g + 1) * BRANCH_WIDTH]
        w_main = jnp.concatenate([group(g) for g in ROW_GROUPS], axis=1).astype(BF16)
        w_ft = jnp.concatenate([group(g) for g in FEAT_GROUPS], axis=1).T.astype(BF16)
        w_fl = jnp.zeros((D, LANES), F32).at[:, :C_HEADS].set(w_in[l, :, n_main:]).astype(BF16)
        b_fl = jnp.zeros((1, LANES), F32).at[0, :C_HEADS].set(b_forget[l].astype(F32))

        if l == 0:
            proj, ft, h = _inproj(x2, w_main, w_ft, cos_t, sin_t, norm_gain=norm_mix[0])
        else:
            proj, ft = _inproj(h, w_main, w_ft, cos_t, sin_t)
        proj = proj.reshape(-1, B, S, LANES)
        f_tok, k_bias = _forget_cumsum(h.reshape(B, S, D), w_fl, b_fl)
        f_row = f_tok[:, :, :C_HEADS].transpose(0, 2, 1).reshape(B, N_PAIRS, 2, S)

        ya = _band_attention(proj, ft, _band_bias(rel_bias[l], min(TQ_FLASH, S)), B, S)
        yb = _diff_attention(proj, ft, lambdas[l], sub_gain[l], lam_init, B, S)
        yc = _forget_attention(proj, ft, f_row, k_bias, B, S)

        w_br = jnp.stack([w_br_a[l], w_br_b[l], w_br_c[l]]).astype(BF16)
        x2 = _merge(x2, h, ya.reshape(GROUP_BLOCKS, T, LANES), yb.reshape(GROUP_BLOCKS, T, LANES),
                    yc.reshape(GROUP_BLOCKS, T, LANES), w_gate[l].astype(BF16),
                    b_gate[l].astype(F32).reshape(1, -1), w_br, w_out[l].astype(BF16))
        last = l == depth - 1
        g_next = final_norm if last else norm_mix[l + 1]
        res = _mlp(x2, norm_mlp[l], w_ff1[l].astype(BF16), w_ff2[l].astype(BF16), g_next, last)
        if last:
            out = res
        else:
            x2, h = res
    return out.reshape(B, S, D)
```

```python
import functools
import math

import jax
import jax.numpy as jnp
import numpy as np
from jax import lax
from jax.experimental import pallas as pl
from jax.experimental.pallas import tpu as pltpu

D_MODEL = 1024
CHUNK = 64
LEFT_CHUNKS = 8
HEAD_DIM = 64
C_HEADS = 8
BRANCH_WIDTH = 512
D_FF = 4 * D_MODEL
REL_CLIP = 128
ROPE_THETA = 500000.0
ROPE_DIM = HEAD_DIM // 4
ROPE_HALF = ROPE_DIM // 2
EPS = 1e-6

LANES = 128
LOG2E = 1.4426950408889634
NEG_BIG = -1e30

TM_DENSE = 1024
TM_FORGET = 256
TQ_FLASH = 512
VMEM_LIMIT = 56 * 1024 * 1024

BF16 = jnp.bfloat16
F32 = jnp.float32


def _params(*sem):
    return pltpu.CompilerParams(dimension_semantics=sem, vmem_limit_bytes=VMEM_LIMIT)


def _const_spec(shape):
    nd = len(shape)
    return pl.BlockSpec(shape, lambda *_: (0,) * nd, pipeline_mode=pl.Buffered(1))


def _rms(x, g):
    return x * lax.rsqrt(jnp.mean(x * x, axis=-1, keepdims=True) + EPS) * g


def _rope_table_kernel(ang_ref, cos_ref, sin_ref):
    ang = ang_ref[...]
    row = lax.broadcasted_iota(jnp.int32, ang.shape, 0)
    sign = jnp.where(row < ROPE_HALF, -1.0, 1.0)
    cos_ref[...] = jnp.cos(ang)
    sin_ref[...] = jnp.sin(ang) * sign


def _rope_tables(positions):
    T = positions.size
    inv_freq = jnp.power(ROPE_THETA, -jnp.arange(0, ROPE_DIM, 2, dtype=F32) / ROPE_DIM)
    ang = jnp.tile(inv_freq, 2)[:, None] * positions.reshape(1, T).astype(F32)
    tt = min(8192, T)
    spec = pl.BlockSpec((ROPE_DIM, tt), lambda i: (0, i))
    return pl.pallas_call(
        _rope_table_kernel,
        out_shape=(jax.ShapeDtypeStruct((ROPE_DIM, T), F32),) * 2,
        grid=(T // tt,), in_specs=[spec], out_specs=(spec, spec),
        compiler_params=_params("parallel"), name="rope_tables",
    )(ang)


ROW_GROUPS = (1, 7)
FEAT_GROUPS = (5, 8, 3, 6, 0, 2, 4)
FEAT_ROPE_GROUPS = (2, 6)
ROW_SLOTS = {0: 0, 1: 2}
KB_FEAT_GROUP, KB_ROW_SLOT = 6, 1
GROUP_BLOCKS = BRANCH_WIDTH // LANES


def _inproj_kernel(*refs, normalize):
    if normalize:
        x_ref, g_ref, w_ref, wft_ref, cos_t_ref, sin_t_ref, o_ref, ft_ref, h_ref = refs
        h = _rms(x_ref[...], g_ref[...]).astype(h_ref.dtype)
        h_ref[...] = h
    else:
        h_ref, w_ref, wft_ref, cos_t_ref, sin_t_ref, o_ref, ft_ref = refs
        h = h_ref[...]
    for c in range(len(ROW_GROUPS)):
        res = jnp.dot(h, w_ref[:, c * BRANCH_WIDTH:(c + 1) * BRANCH_WIDTH],
                      preferred_element_type=F32)
        for s in range(GROUP_BLOCKS):
            o_ref[ROW_SLOTS[c] * GROUP_BLOCKS + s] = res[:, s * LANES:(s + 1) * LANES].astype(o_ref.dtype)
    for c in range(len(FEAT_GROUPS)):
        res_t = lax.dot_general(wft_ref[c * BRANCH_WIDTH:(c + 1) * BRANCH_WIDTH, :], h,
                                (((1,), (1,)), ((), ())), preferred_element_type=F32)
        for s in range(GROUP_BLOCKS):
            blk = res_t[s * LANES:(s + 1) * LANES, :]
            if c in FEAT_ROPE_GROUPS:
                pieces = []
                for base in range(0, LANES, HEAD_DIM):
                    x1 = blk[base:base + ROPE_HALF]
                    x2 = blk[base + ROPE_HALF:base + ROPE_DIM]
                    pieces.append(jnp.concatenate([x1, x2], axis=0) * cos_t_ref[...]
                                  + jnp.concatenate([x2, x1], axis=0) * sin_t_ref[...])
                    pieces.append(blk[base + ROPE_DIM:base + HEAD_DIM])
                blk = jnp.concatenate(pieces, axis=0)
            if c == KB_FEAT_GROUP:
                o_ref[KB_ROW_SLOT * GROUP_BLOCKS + s] = blk.T.astype(o_ref.dtype)
            else:
                ft_ref[c * GROUP_BLOCKS + s] = blk.astype(ft_ref.dtype)


def _inproj(h_or_x, w_main, w_ft, cos_t, sin_t, norm_gain=None):
    T = h_or_x.shape[0]
    tm = min(TM_DENSE, T)
    n_row = (len(ROW_GROUPS) + 1) * GROUP_BLOCKS
    n_ft = (len(FEAT_GROUPS) - 1) * GROUP_BLOCKS
    row = pl.BlockSpec((tm, D_MODEL), lambda i: (i, 0))
    feat = pl.BlockSpec((ROPE_DIM, tm), lambda i: (0, i))
    normalize = norm_gain is not None
    out_shape = [jax.ShapeDtypeStruct((n_row, T, LANES), BF16), jax.ShapeDtypeStruct((n_ft, LANES, T), BF16)]
    out_specs = [pl.BlockSpec((n_row, tm, LANES), lambda i: (0, i, 0)),
                 pl.BlockSpec((n_ft, LANES, tm), lambda i: (0, 0, i))]
    lead_specs, lead_args = [row], [h_or_x]
    if normalize:
        out_shape.append(jax.ShapeDtypeStruct((T, D_MODEL), BF16))
        out_specs.append(row)
        lead_specs.append(_const_spec((1, D_MODEL)))
        lead_args.append(norm_gain.reshape(1, D_MODEL))
    return pl.pallas_call(
        functools.partial(_inproj_kernel, normalize=normalize),
        out_shape=tuple(out_shape), grid=(T // tm,),
        in_specs=lead_specs + [_const_spec(w_main.shape), _const_spec(w_ft.shape), feat, feat],
        out_specs=tuple(out_specs),
        compiler_params=_params("parallel"), name="inproj",
    )(*lead_args, w_main, w_ft, cos_t, sin_t)


N_PAIRS = C_HEADS // 2
FORGET_ROWS = 4


def _split3(x):
    hi = x.astype(BF16)
    r1 = x - hi.astype(F32)
    mid = r1.astype(BF16)
    lo = (r1 - mid.astype(F32)).astype(BF16)
    return hi, mid, lo


def _forget_constants(tri_ref, place_ref):
    tm = tri_ref.shape[0]
    row = lax.broadcasted_iota(jnp.int32, (tm, tm), 0)
    col = lax.broadcasted_iota(jnp.int32, (tm, tm), 1)
    tri_ref[...] = (col <= row).astype(BF16)
    src = lax.broadcasted_iota(jnp.int32, place_ref.shape, 0)
    dst = lax.broadcasted_iota(jnp.int32, place_ref.shape, 1)
    pair, lane = dst // LANES, dst % LANES
    chain, piece = lane // 3, lane % 3
    place = (lane < 2 * 3) & (src == piece * LANES + 2 * pair + chain)
    place_ref[...] = jnp.where(place, -1.0, 0.0).astype(BF16)


def _forget_kernel(h_ref, w_ref, b_ref, f_ref, kbias_ref, tri_ref, place_ref, carry_ref):
    @pl.when((pl.program_id(0) == 0) & (pl.program_id(1) == 0))
    def _():
        _forget_constants(tri_ref, place_ref)

    @pl.when(pl.program_id(1) == 0)
    def _():
        carry_ref[...] = jnp.zeros_like(carry_ref)

    tm = h_ref.shape[1]
    for r in range(h_ref.shape[0]):
        fl = jnp.dot(h_ref[r], w_ref[...], preferred_element_type=F32) + b_ref[...]
        ls = jnp.minimum(fl, 0.0) - jnp.log1p(jnp.exp(-jnp.abs(fl)))
        cs = sum(jnp.dot(tri_ref[...], part, preferred_element_type=F32) for part in _split3(ls))
        f = cs + carry_ref[r]
        carry_ref[r] = f[tm - 1:tm, :]
        f2 = f * LOG2E
        f_ref[r] = f2
        parts = jnp.concatenate(_split3(f2), axis=1)
        placed = jnp.dot(parts, place_ref[...], preferred_element_type=F32)
        for p in range(N_PAIRS):
            kbias_ref[r, p] = placed[:, p * LANES:(p + 1) * LANES].astype(kbias_ref.dtype)


def _forget_cumsum(h3, w_fl, b_fl):
    B, S, _ = h3.shape
    tm = min(TM_FORGET, S)
    rows = FORGET_ROWS if B % FORGET_ROWS == 0 else 1
    return pl.pallas_call(
        _forget_kernel,
        out_shape=(jax.ShapeDtypeStruct((B, S, LANES), F32),
                   jax.ShapeDtypeStruct((B, N_PAIRS, S, LANES), BF16)),
        grid=(B // rows, S // tm),
        in_specs=[pl.BlockSpec((rows, tm, D_MODEL), lambda b, i: (b, i, 0)),
                  _const_spec(w_fl.shape), _const_spec(b_fl.shape)],
        out_specs=(pl.BlockSpec((rows, tm, LANES), lambda b, i: (b, i, 0)),
                   pl.BlockSpec((rows, N_PAIRS, tm, LANES), lambda b, i: (b, 0, i, 0))),
        scratch_shapes=[pltpu.VMEM((tm, tm), BF16), pltpu.VMEM((3 * LANES, N_PAIRS * LANES), BF16),
                        pltpu.VMEM((rows, 1, LANES), F32)],
        compiler_params=_params("arbitrary", "arbitrary"), name="forget_cumsum",
    )(h3, w_fl, b_fl)


ROW_T, ROW_ALPHA, ROW_MPREV = 0, 1, 2
FLASH_UNROLL = 8
SUM_ROWS = 16
Q_SPLIT = 2
BAND_ALL_MASKED = 2
assert FLASH_UNROLL % 2 == 0


def _flash_body(qt_ref, k_ref, vt_ref, fq_ref, kbias_ref, bias_ref, o_ref, finalize, s_refs, st_refs, mrun_ref,
                acc_ref, *, tq, chunk_causal, split_values):
    band = bias_ref is not None

    def key_rows(u, own_block):
        if not band or Q_SPLIT != 2:
            return slice(0, tk)
        first_half = units[u][1] == 0
        if own_block:
            return slice(0, tk // 2) if first_half else slice(0, tk)
        return slice(0, tk) if first_half else slice(tk // 2, tk)
    tk = tq
    nq = qt_ref.shape[2] // tq
    width = tq // Q_SPLIT
    units = [(c, n) for c in range(2) for n in range(Q_SPLIT)]
    feat = lax.broadcasted_iota(jnp.int32, (LANES, 1), 0)
    own = [feat < HEAD_DIM, feat >= HEAD_DIM]

    def queries(qi):
        q_off = pl.multiple_of(qi * tq, tq)
        qt = qt_ref[0, :, pl.ds(q_off, tq)]
        qs = [jnp.where(own[c], qt, jnp.zeros_like(qt)) for c in range(2)]
        if kbias_ref is not None:
            qs = [jnp.concatenate([qs[c], ((feat >= 3 * c) & (feat < 3 * c + 3)).astype(qt.dtype)
                                   * jnp.ones_like(qt)], axis=0) for c in range(2)]
        shifts = [None if fq_ref is None else fq_ref[0, 0, c:c + 1, pl.ds(q_off, tq)] for c in range(2)]
        cols = lambda x, n: None if x is None else x[:, n * width:(n + 1) * width]
        return [(cols(qs[c], n), cols(shifts[c], n)) for c, n in units]

    def stats(shift, col_max, m_prev):
        if shift is not None:
            col_max = col_max + shift
        m_new = jnp.maximum(m_prev, col_max)
        alpha = jnp.exp2(m_prev - m_new)
        t = m_new if shift is None else m_new - shift
        return m_new, alpha, t

    def keys(j):
        off = pl.multiple_of(j * tk, tk)
        k = k_ref[0, 0, pl.ds(off, tk), :]
        if kbias_ref is not None:
            k = jnp.concatenate([k, kbias_ref[0, 0, pl.ds(off, tk), :]], axis=1)
        return k

    def values(j):
        off = pl.multiple_of(j * tk, tk)
        vt = vt_ref[0, :, pl.ds(off, tk)]
        ones = jnp.ones((SUM_ROWS, tk), vt.dtype)
        if split_values:
            return [jnp.concatenate([vt[c * HEAD_DIM:(c + 1) * HEAD_DIM], ones], axis=0) for c in range(2)]
        return [jnp.concatenate([vt, ones], axis=0)] * 2

    def scores(u, k, slot, query, first=False, own_block=True, table=None):
        q_u, shift = query[u]
        rows = key_rows(u, own_block)
        s = jnp.dot(k[rows], q_u, preferred_element_type=F32)
        if band:
            c, n = units[u]
            s = s + bias_ref[0, c, int(own_block) if table is None else table, rows, n * width:(n + 1) * width]
        s_refs[slot][u, rows] = s
        m_prev = jnp.full((1, width), NEG_BIG, F32) if first else mrun_ref[u]
        m_new, alpha, t = stats(shift, jnp.max(s, axis=0, keepdims=True), m_prev)
        mrun_ref[u] = m_new
        st_refs[slot][u, ROW_T:ROW_T + 1] = t
        st_refs[slot][u, ROW_ALPHA:ROW_ALPHA + 1] = alpha
        st_refs[slot][u, ROW_MPREV:ROW_MPREV + 1] = m_prev

    def update(u, vts, slot, mask, query, own_block=True):
        rows = key_rows(u, own_block)
        s = s_refs[slot][u, rows]
        if mask is None:
            t = st_refs[slot][u, ROW_T:ROW_T + 1]
            alpha = st_refs[slot][u, ROW_ALPHA:ROW_ALPHA + 1]
        else:
            n = units[u][1]
            s = jnp.where(mask[:, n * width:(n + 1) * width], s, NEG_BIG)
            _, alpha, t = stats(query[u][1], jnp.max(s, axis=0, keepdims=True),
                                st_refs[slot][u, ROW_MPREV:ROW_MPREV + 1])
        p = jnp.exp2(s - t).astype(BF16)
        acc_ref[u] = alpha * acc_ref[u] + jnp.dot(vts[units[u][0]][:, rows], p, preferred_element_type=F32)

    key = lax.broadcasted_iota(jnp.int32, (tk, tq), 0)
    qry = lax.broadcasted_iota(jnp.int32, (tk, tq), 1)
    if band:
        mask = None
    elif chunk_causal:
        mask = (key // CHUNK) <= (qry // CHUNK)
    else:
        mask = key <= qry

    def run(j0, n_full, query):
        for b in range(n_full):
            k, vts = keys(j0 + b + 1), values(jnp.maximum(j0 + b, 0))
            for u in range(len(units)):
                scores(u, k, (b + 1) % 2, query)
                update(u, vts, b % 2, None, query, own_block=not band)

    k0 = keys(0)
    first_query = queries(0)
    for u in range(len(units)):
        if band:
            scores(u, k0, 0, first_query, first=True, own_block=False, table=BAND_ALL_MASKED)
        else:
            scores(u, k0, 0, first_query, first=True)

    def query_block(qi, carry):
        query = queries(qi)
        acc_ref[...] = jnp.zeros(acc_ref.shape, F32)

        def tail(rem):
            run(qi - rem, rem, query)
            nxt = jnp.minimum(qi + 1, nq - 1)
            vts, k_first = values(qi), keys(jnp.maximum(nxt - 1, 0) if band else 0)
            next_query = queries(nxt)
            for u in range(len(units)):
                update(u, vts, rem % 2, mask, query)
                scores(u, k_first, 0, next_query, first=True, own_block=False)
            o_ref[0, 0, pl.ds(pl.multiple_of(qi * tq, tq), tq), :] = finalize(acc_ref).astype(o_ref.dtype)

        if band:
            tail(1)
        else:
            def group(jj, inner):
                run(jj * FLASH_UNROLL, FLASH_UNROLL, query)
                return inner

            lax.fori_loop(0, qi // FLASH_UNROLL, group, 0)
            for rem in range(FLASH_UNROLL):
                pl.when(qi % FLASH_UNROLL == rem)(functools.partial(tail, rem))
        return carry

    lax.fori_loop(0, nq, query_block, 0)


def _diff_kernel(qt_ref, k_ref, vt_ref, lam_ref, gain_ref, o_ref, s0_ref, s1_ref, st0_ref, st1_ref,
                 mrun_ref, acc_ref, *, tq, lam_init):
    lf = lam_ref[...]
    lam = (jnp.exp(jnp.sum(lf[0:1] * lf[1:2], axis=-1, keepdims=True))
           - jnp.exp(jnp.sum(lf[2:3] * lf[3:4], axis=-1, keepdims=True)) + lam_init)
    gain = gain_ref[...]

    def finalize(acc):
        chain = lambda c: jnp.concatenate([acc[c * Q_SPLIT + n] for n in range(Q_SPLIT)], axis=1)
        maps = [a[:LANES] / a[LANES:LANES + 1] for a in (chain(0), chain(1))]
        o_t = maps[0] - lam * maps[1]
        return _rms(o_t.T, gain) * (1.0 - lam_init)

    _flash_body(qt_ref, k_ref, vt_ref, None, None, None, o_ref, finalize, (s0_ref, s1_ref), (st0_ref, st1_ref),
                mrun_ref, acc_ref, tq=tq, chunk_causal=True, split_values=False)


def _head_pair_output(acc):
    chain = lambda c: jnp.concatenate([acc[c * Q_SPLIT + n] for n in range(Q_SPLIT)], axis=1)
    heads = [a[:HEAD_DIM] / a[HEAD_DIM:HEAD_DIM + 1] for a in (chain(0), chain(1))]
    return jnp.concatenate(heads, axis=0).T


def _forget_attn_kernel(qt_ref, k_ref, vt_ref, fq_ref, kbias_ref, o_ref, s0_ref, s1_ref, st0_ref, st1_ref,
                        mrun_ref, acc_ref, *, tq):
    _flash_body(qt_ref, k_ref, vt_ref, fq_ref, kbias_ref, None, o_ref, _head_pair_output, (s0_ref, s1_ref),
                (st0_ref, st1_ref), mrun_ref, acc_ref, tq=tq, chunk_causal=False, split_values=True)


def _band_kernel(qt_ref, k_ref, vt_ref, bias_ref, o_ref, s0_ref, s1_ref, st0_ref, st1_ref, mrun_ref, acc_ref, *,
                 tq):
    _flash_body(qt_ref, k_ref, vt_ref, None, None, bias_ref, o_ref, _head_pair_output, (s0_ref, s1_ref),
                (st0_ref, st1_ref), mrun_ref, acc_ref, tq=tq, chunk_causal=False, split_values=True)


def _flash_scratch(tq, acc_rows):
    n_units, width = 2 * Q_SPLIT, tq // Q_SPLIT
    return [pltpu.VMEM((n_units, tq, width), F32), pltpu.VMEM((n_units, tq, width), F32),
            pltpu.VMEM((n_units, 3, width), F32), pltpu.VMEM((n_units, 3, width), F32),
            pltpu.VMEM((n_units, 1, width), F32), pltpu.VMEM((n_units, acc_rows, width), F32)]


def _flash_specs(qt_block, k_block, vt_block, S):
    return [pl.BlockSpec((1, LANES, S), lambda b, p: (qt_block + p, 0, b)),
            pl.BlockSpec((1, 1, S, LANES), lambda b, p: (k_block + p, b, 0, 0)),
            pl.BlockSpec((1, LANES, S), lambda b, p: (vt_block + p, 0, b))]


def _diff_attention(proj, ft, lambdas, sub_gain, lam_init, B, S):
    tq = min(TQ_FLASH, S)
    return pl.pallas_call(
        functools.partial(_diff_kernel, tq=tq, lam_init=lam_init),
        out_shape=jax.ShapeDtypeStruct((GROUP_BLOCKS, B, S, LANES), BF16),
        grid=(B, GROUP_BLOCKS),
        in_specs=_flash_specs(2 * GROUP_BLOCKS, GROUP_BLOCKS, 0, S) + [
            _const_spec(lambdas.shape), _const_spec((1, LANES))],
        out_specs=pl.BlockSpec((1, 1, S, LANES), lambda b, p: (p, b, 0, 0)),
        scratch_shapes=_flash_scratch(tq, LANES + SUM_ROWS),
        compiler_params=_params("parallel", "parallel"), name="diff_attention",
    )(ft, proj, ft, lambdas.astype(F32), sub_gain.astype(F32).reshape(1, LANES))


def _forget_attention(proj, ft, f_row, k_bias, B, S):
    tq = min(TQ_FLASH, S)
    return pl.pallas_call(
        functools.partial(_forget_attn_kernel, tq=tq),
        out_shape=jax.ShapeDtypeStruct((GROUP_BLOCKS, B, S, LANES), BF16),
        grid=(B, GROUP_BLOCKS),
        in_specs=_flash_specs(3 * GROUP_BLOCKS, 2 * GROUP_BLOCKS, GROUP_BLOCKS, S) + [
            pl.BlockSpec((1, 1, 2, S), lambda b, p: (b, p, 0, 0)),
            pl.BlockSpec((1, 1, S, LANES), lambda b, p: (b, p, 0, 0))],
        out_specs=pl.BlockSpec((1, 1, S, LANES), lambda b, p: (p, b, 0, 0)),
        scratch_shapes=_flash_scratch(tq, HEAD_DIM + SUM_ROWS),
        compiler_params=_params("parallel", "parallel"), name="forget_attention",
    )(ft, proj, ft, f_row, k_bias)


def _band_attention(proj, ft, bias, B, S):
    tq = min(TQ_FLASH, S)
    return pl.pallas_call(
        functools.partial(_band_kernel, tq=tq),
        out_shape=jax.ShapeDtypeStruct((GROUP_BLOCKS, B, S, LANES), BF16),
        grid=(B, GROUP_BLOCKS),
        in_specs=_flash_specs(4 * GROUP_BLOCKS, 0, 5 * GROUP_BLOCKS, S) + [
            pl.BlockSpec((1,) + bias.shape[1:], lambda b, p: (p, 0, 0, 0, 0))],
        out_specs=pl.BlockSpec((1, 1, S, LANES), lambda b, p: (p, b, 0, 0)),
        scratch_shapes=_flash_scratch(tq, HEAD_DIM + SUM_ROWS),
        compiler_params=_params("parallel", "parallel"), name="band_attention",
    )(ft, proj, ft, bias)


def _band_bias(rel_bias, tq):
    n_keys = 2 * tq
    period = n_keys + tq
    u = np.arange(period)
    diag = np.where(u < n_keys, u, u - period)
    rel = np.clip(diag - tq, -REL_CLIP, REL_CLIP) + REL_CLIP
    profile = rel_bias.astype(F32)[:, rel] * LOG2E
    heads = profile.shape[0]
    rows = min(tq, 256)
    pieces = []
    for first in range(0, tq, rows):
        shifted = jnp.roll(profile, first, axis=1)
        piece = jnp.tile(shifted, (1, rows))[:, :rows * (period - 1)]
        pieces.append(piece.reshape(heads, rows, period - 1)[:, :, :n_keys])
    table = jnp.concatenate(pieces, axis=1)
    q_chunk = tq // CHUNK + np.arange(tq)[:, None] // CHUNK
    k_chunk = np.arange(n_keys)[None, :] // CHUNK
    valid = (k_chunk <= q_chunk) & (k_chunk >= q_chunk - LEFT_CHUNKS)
    table = jnp.where(jnp.asarray(valid)[None], table, NEG_BIG).transpose(0, 2, 1)
    table = table.reshape(heads // 2, 2, 2, tq, tq)
    return jnp.concatenate([table, jnp.full_like(table[:, :, :1], NEG_BIG)], axis=2)


def _merge_kernel(x_ref, h_ref, ya_ref, yb_ref, yc_ref, wg_ref, bg_ref, wbr_ref, wo_ref, o_ref):
    h = h_ref[...]
    merged = None
    for b, y_ref in enumerate((ya_ref, yb_ref, yc_ref)):
        y = jnp.concatenate([y_ref[i] for i in range(y_ref.shape[0])], axis=1)
        t = jnp.dot(y, wbr_ref[b], preferred_element_type=F32)
        z = jnp.dot(h, wg_ref[:, b * D_MODEL:(b + 1) * D_MODEL], preferred_element_type=F32)
        g = jax.nn.sigmoid(z + bg_ref[:, b * D_MODEL:(b + 1) * D_MODEL])
        merged = g * t if merged is None else merged + g * t
    o_ref[...] = x_ref[...] + jnp.dot(merged.astype(BF16), wo_ref[...], preferred_element_type=F32)


def _merge(x2, h, ya, yb, yc, w_gate, b_gate, w_br, w_out):
    T = x2.shape[0]
    tm = min(TM_DENSE, T)
    nblk = ya.shape[0]
    row = pl.BlockSpec((tm, D_MODEL), lambda i: (i, 0))
    yspec = pl.BlockSpec((nblk, tm, LANES), lambda i: (0, i, 0))
    return pl.pallas_call(
        _merge_kernel,
        out_shape=jax.ShapeDtypeStruct((T, D_MODEL), F32),
        grid=(T // tm,),
        in_specs=[row, row, yspec, yspec, yspec, _const_spec(w_gate.shape), _const_spec(b_gate.shape),
                  _const_spec(w_br.shape), _const_spec(w_out.shape)],
        out_specs=row,
        compiler_params=_params("parallel"), name="merge_out",
    )(x2, h, ya, yb, yc, w_gate, b_gate, w_br, w_out)


FF_CHUNK = 1024


def _mlp_kernel(x_ref, gn_ref, w1_ref, w2_ref, gnext_ref, *out_refs):
    x = x_ref[...]
    h2 = _rms(x, gn_ref[...]).astype(BF16)
    acc = x
    for c in range(D_FF // FF_CHUNK):
        a = jnp.dot(h2, w1_ref[:, c * FF_CHUNK:(c + 1) * FF_CHUNK], preferred_element_type=F32)
        a = jnp.square(jnp.maximum(a, 0.0))
        acc = acc + jnp.dot(a.astype(BF16), w2_ref[c * FF_CHUNK:(c + 1) * FF_CHUNK, :],
                            preferred_element_type=F32)
    normed = _rms(acc, gnext_ref[...])
    if len(out_refs) == 2:
        out_refs[0][...] = acc
        out_refs[1][...] = normed.astype(out_refs[1].dtype)
    else:
        out_refs[0][...] = normed.astype(out_refs[0].dtype)


def _mlp(x2, g_mlp, w1, w2, g_next, last):
    T = x2.shape[0]
    tm = min(TM_DENSE, T)
    row = pl.BlockSpec((tm, D_MODEL), lambda i: (i, 0))
    if last:
        out_shape, out_specs = jax.ShapeDtypeStruct((T, D_MODEL), F32), row
    else:
        out_shape = (jax.ShapeDtypeStruct((T, D_MODEL), F32), jax.ShapeDtypeStruct((T, D_MODEL), BF16))
        out_specs = (row, row)
    return pl.pallas_call(
        _mlp_kernel, out_shape=out_shape, grid=(T // tm,),
        in_specs=[row, _const_spec((1, D_MODEL)), _const_spec(w1.shape), _const_spec(w2.shape),
                  _const_spec((1, D_MODEL))],
        out_specs=out_specs,
        compiler_params=_params("parallel"), name="mlp",
    )(x2, g_mlp.reshape(1, D_MODEL), w1, w2, g_next.reshape(1, D_MODEL))


def kernel(x, positions, norm_mix, w_in, rel_bias, lambdas, sub_gain, b_forget, w_br_a, w_br_b, w_br_c,
           w_gate, b_gate, w_out, norm_mlp, w_ff1, w_ff2, final_norm):
    B, S, D = x.shape
    T = B * S
    depth = w_in.shape[0]
    n_main = 9 * BRANCH_WIDTH
    q_scale = HEAD_DIM ** -0.5 * LOG2E
    col_scale = jnp.ones((n_main,), F32)
    for grp in (0, 3, 6):
        col_scale = col_scale.at[grp * BRANCH_WIDTH:(grp + 1) * BRANCH_WIDTH].set(q_scale)

    cos_t, sin_t = _rope_tables(positions)
    x2 = x.reshape(T, D)
    h = None
    out = None
    for l in range(depth):
        lam_init = 0.8 - 0.6 * math.exp(-0.3 * l)
        w_all = w_in[l, :, :n_main] * col_scale[None, :]
        group = lambda g: w_all[:, g * BRANCH_WIDTH:(g + 1) * BRANCH_WIDTH]
        w_main = jnp.concatenate([group(g) for g in ROW_GROUPS], axis=1).astype(BF16)
        w_ft = jnp.concatenate([group(g) for g in FEAT_GROUPS], axis=1).T.astype(BF16)
        w_fl = jnp.zeros((D, LANES), F32).at[:, :C_HEADS].set(w_in[l, :, n_main:]).astype(BF16)
        b_fl = jnp.zeros((1, LANES), F32).at[0, :C_HEADS].set(b_forget[l].astype(F32))

        if l == 0:
            proj, ft, h = _inproj(x2, w_main, w_ft, cos_t, sin_t, norm_gain=norm_mix[0])
        else:
            proj, ft = _inproj(h, w_main, w_ft, cos_t, sin_t)
        proj = proj.reshape(-1, B, S, LANES)
        f_tok, k_bias = _forget_cumsum(h.reshape(B, S, D), w_fl, b_fl)
        f_row = f_tok[:, :, :C_HEADS].transpose(0, 2, 1).reshape(B, N_PAIRS, 2, S)

        ya = _band_attention(proj, ft, _band_bias(rel_bias[l], min(TQ_FLASH, S)), B, S)
        yb = _diff_attention(proj, ft, lambdas[l], sub_gain[l], lam_init, B, S)
        yc = _forget_attention(proj, ft, f_row, k_bias, B, S)

        w_br = jnp.stack([w_br_a[l], w_br_b[l], w_br_c[l]]).astype(BF16)
        x2 = _merge(x2, h, ya.reshape(GROUP_BLOCKS, T, LANES), yb.reshape(GROUP_BLOCKS, T, LANES),
                    yc.reshape(GROUP_BLOCKS, T, LANES), w_gate[l].astype(BF16),
                    b_gate[l].astype(F32).reshape(1, -1), w_br, w_out[l].astype(BF16))
        last = l == depth - 1
        g_next = final_norm if last else norm_mix[l + 1]
        res = _mlp(x2, norm_mlp[l], w_ff1[l].astype(BF16), w_ff2[l].astype(BF16), g_next, last)
        if last:
            out = res
        else:
            x2, h = res
    return out.reshape(B, S, D)
```
